```python
import jax, jax.numpy as jnp
from jax import lax
import numpy as np

D_MODEL = 1024
BATCH = 16
SEQ = 4096
DEPTH = 4

N_A_LAYERS = DEPTH // 2
N_B_LAYERS = DEPTH - N_A_LAYERS
N_DENSE = (DEPTH + 1) // 2
N_MOE = DEPTH // 2
CONV_WIDTH = 3
ATT_GROUPS = ((128, 1), (512, 4), (2048, 16))
N_GROUPS = len(ATT_GROUPS)
HEAD_DIM = 64
HEADS_PER_GROUP = D_MODEL // 128
ATT_WIDTH = HEADS_PER_GROUP * HEAD_DIM
BLK = 128
D_FF = 11 * D_MODEL // 4
N_EXPERTS = 8
TOP_K = 2
D_EXPERT = 7 * D_MODEL // 2
ALPHA = (2.0 * DEPTH) ** 0.25
BETA = (8.0 * DEPTH) ** -0.25
LN_EPS = 1e-5

kernel_name = "hybrid_shortconv_dilatedattn_moe_deepnorm"


def layer_norm(x, g, b):
    xf = x.astype(jnp.float32)
    mu = jnp.mean(xf, axis=-1, keepdims=True)
    var = jnp.mean(jnp.square(xf - mu), axis=-1, keepdims=True)
    y = (xf - mu) * lax.rsqrt(var + LN_EPS) * g.astype(jnp.float32) + b.astype(jnp.float32)
    return y.astype(x.dtype)


def short_conv_mixer(x, w_in, conv_w, w_out):
    S = x.shape[1]
    gate_b, gate_c, u = jnp.split(jnp.einsum('bsd,dc->bsc', x, w_in), 3, axis=-1)
    u = gate_c * u
    up = jnp.pad(u, ((0, 0), (CONV_WIDTH - 1, 0), (0, 0)))
    conv = sum(conv_w[k] * up[:, k:k + S] for k in range(CONV_WIDTH))
    return jnp.einsum('bsc,cd->bsd', gate_b * conv, w_out)


def swiglu(x, w_gate, w_up, w_down):
    h = jax.nn.silu(jnp.einsum('bsd,df->bsf', x, w_gate)) * jnp.einsum('bsd,df->bsf', x, w_up)
    return jnp.einsum('bsf,fd->bsd', h, w_down)


def moe_swiglu(x, w_router, w_gate, w_up, w_down):
    logits = jnp.einsum('bsd,de->bse', x, w_router, preferred_element_type=jnp.float32)
    top_val, top_idx = lax.top_k(logits, TOP_K)
    gates = jax.nn.softmax(top_val, axis=-1)
    combine = jnp.einsum('bsk,bske->bse', gates,
                         jax.nn.one_hot(top_idx, N_EXPERTS, dtype=jnp.float32)).astype(x.dtype)
    out = jnp.zeros_like(x)
    for e in range(N_EXPERTS):
        out = out + combine[..., e:e + 1] * swiglu(x, w_gate[e], w_up[e], w_down[e])
    return out


def residue_blocks(t, d):
    B, S, H, E = t.shape
    L = -(-S // d)
    nblk = -(-L // BLK)
    Lp = nblk * BLK
    t = jnp.pad(t, ((0, 0), (0, d * Lp - S), (0, 0), (0, 0)))
    t = t.reshape(B, Lp, d, H, E).transpose(0, 2, 1, 3, 4)
    return t.reshape(B, d, nblk, BLK, H, E)


def from_residue_blocks(t, S):
    B, d, nblk, blk = t.shape[:4]
    rest = t.shape[4:]
    t = jnp.moveaxis(t.reshape((B, d, nblk * blk) + rest), 1, 2)
    return t.reshape((B, nblk * blk * d) + rest)[:, :S]


def banded(t):
    prev = jnp.pad(t, ((0, 0), (0, 0), (1, 0), (0, 0), (0, 0), (0, 0)))[:, :, :-1]
    return jnp.concatenate([prev, t], axis=3)


def band_mask(nblk, span):
    q = jnp.arange(BLK)[:, None]
    k = jnp.arange(2 * BLK)[None, :]
    dist = BLK + q - k
    n = jnp.arange(nblk)[:, None, None]
    return (dist >= 0) & (dist <= span) & ((n > 0) | (k >= BLK))


def shared_kv(x, kv_w):
    B, S, _ = x.shape
    kv = jnp.einsum('bsd,dc->bsc', x, kv_w).reshape(B, S, N_GROUPS, 2, HEADS_PER_GROUP, HEAD_DIM)
    blocks = []
    for g, (window, d) in enumerate(ATT_GROUPS):
        blocks.append((residue_blocks(kv[:, :, g, 0], d), residue_blocks(kv[:, :, g, 1], d)))
    return tuple(blocks)


def dilated_group_attention(q, k_res, v_res, d, span):
    S = q.shape[1]
    qb = residue_blocks(q, d)
    kb = banded(k_res)
    vb = banded(v_res)
    nblk = qb.shape[2]
    s = jnp.einsum('bdnqhe,bdnkhe->bdnhqk', qb, kb,
                   preferred_element_type=jnp.float32) * (HEAD_DIM ** -0.5)
    s = jnp.where(band_mask(nblk, span)[None, None, :, None], s, -jnp.inf)
    lse = jax.nn.logsumexp(s, axis=-1)
    p = jnp.exp(s - lse[..., None])
    o = jnp.einsum('bdnhqk,bdnkhe->bdnqhe', p.astype(vb.dtype), vb,
                   preferred_element_type=jnp.float32)
    o = from_residue_blocks(o, S)
    lse = from_residue_blocks(jnp.moveaxis(lse, 3, 4), S)
    return o, lse


def dilated_mixer(x, w_q, w_o, kv_blocks):
    B, S, _ = x.shape
    q = jnp.einsum('bsd,dc->bsc', x, w_q).reshape(B, S, N_GROUPS, HEADS_PER_GROUP, HEAD_DIM)
    outs, lses = [], []
    for g, (window, d) in enumerate(ATT_GROUPS):
        k_res, v_res = kv_blocks[g]
        o, lse = dilated_group_attention(q[:, :, g], k_res, v_res, d, window // d)
        outs.append(o)
        lses.append(lse)
    w = jax.nn.softmax(jnp.stack(lses, axis=-1), axis=-1)
    o = jnp.einsum('bshg,bshge->bshe', w, jnp.stack(outs, axis=3))
    o = o.reshape(B, S, ATT_WIDTH).astype(x.dtype)
    return jnp.einsum('bsc,cd->bsd', o, w_o)


def setup_inputs(seed: int = 0) -> dict:
    key = jax.random.key(seed)
    ks = jax.random.split(key, 16)
    D = D_MODEL

    def nrm(k, shape, fan_in, scale=1.0):
        return jax.random.normal(k, shape, jnp.float32) * (scale * fan_in ** -0.5)

    return {
        'x': jax.random.normal(ks[0], (BATCH, SEQ, D), jnp.float32),
        'a_w_in': nrm(ks[1], (N_A_LAYERS, D, 3 * D), D),
        'a_conv': nrm(ks[2], (N_A_LAYERS, CONV_WIDTH, D), CONV_WIDTH),
        'a_w_out': nrm(ks[3], (N_A_LAYERS, D, D), D, BETA),
        'kv_w': nrm(ks[4], (D, N_GROUPS * 2 * ATT_WIDTH), D),
        'b_w_q': nrm(ks[5], (N_B_LAYERS, D, N_GROUPS * ATT_WIDTH), D),
        'b_w_o': nrm(ks[6], (N_B_LAYERS, ATT_WIDTH, D), ATT_WIDTH, BETA),
        'ffn_w_gate': nrm(ks[7], (N_DENSE, D, D_FF), D),
        'ffn_w_up': nrm(ks[8], (N_DENSE, D, D_FF), D),
        'ffn_w_down': nrm(ks[9], (N_DENSE, D_FF, D), D_FF, BETA),
        'moe_w_router': nrm(ks[10], (N_MOE, D, N_EXPERTS), D),
        'moe_w_gate': nrm(ks[11], (N_MOE, N_EXPERTS, D, D_EXPERT), D),
        'moe_w_up': nrm(ks[12], (N_MOE, N_EXPERTS, D, D_EXPERT), D),
        'moe_w_down': nrm(ks[13], (N_MOE, N_EXPERTS, D_EXPERT, D), D_EXPERT, BETA),
        'ln_g': 1.0 + 0.02 * jax.random.normal(ks[14], (DEPTH, 2, D), jnp.float32),
        'ln_b': 0.02 * jax.random.normal(ks[15], (DEPTH, 2, D), jnp.float32),
    }


def reference(x, a_w_in, a_conv, a_w_out, kv_w, b_w_q, b_w_o, ffn_w_gate, ffn_w_up, ffn_w_down,
              moe_w_router, moe_w_gate, moe_w_up, moe_w_down, ln_g, ln_b):
    kv_blocks = None
    for l in range(DEPTH):
        if l < N_A_LAYERS:
            mix = short_conv_mixer(x, a_w_in[l], a_conv[l], a_w_out[l])
        else:
            if kv_blocks is None:
                kv_blocks = shared_kv(x, kv_w)
            j = l - N_A_LAYERS
            mix = dilated_mixer(x, b_w_q[j], b_w_o[j], kv_blocks)
        x = layer_norm(ALPHA * x + mix, ln_g[l, 0], ln_b[l, 0])
        i = l // 2
        if l % 2 == 0:
            ff = swiglu(x, ffn_w_gate[i], ffn_w_up[i], ffn_w_down[i])
        else:
            ff = moe_swiglu(x, moe_w_router[i], moe_w_gate[i], moe_w_up[i], moe_w_down[i])
        x = layer_norm(ALPHA * x + ff, ln_g[l, 1], ln_b[l, 1])
    return x
```

```python
import functools

import jax
import jax.numpy as jnp
from jax import lax
from jax.experimental import pallas as pl
from jax.experimental.pallas import tpu as pltpu

F32 = jnp.float32
BF16 = jnp.bfloat16
I32 = jnp.int32

DEPTH = 4
N_A_LAYERS = DEPTH // 2
CONV_WIDTH = 3
ATT_GROUPS = ((128, 1), (512, 4), (2048, 16))
N_GROUPS = len(ATT_GROUPS)
HEAD_DIM = 64
N_EXPERTS = 8
TOP_K = 2
ALPHA = (2.0 * DEPTH) ** 0.25
LN_EPS = 1e-5

QBLK = 128
assert all(w // d == QBLK for w, d in ATT_GROUPS)

LANES = 128
SUBLANES = 8
VMEM_LIMIT = 56 * 1024 * 1024
NEG_BIG = -1e30

TM_MIX = 512
TM_FFN = 1024
TM_EXP = 1024
ROWS_PER_COPY_STEP = 4096

_NT = (((1,), (1,)), ((), ()))


def _layer_norm(z, g, b):
    mu = jnp.mean(z, axis=-1, keepdims=True)
    zc = z - mu
    var = jnp.mean(zc * zc, axis=-1, keepdims=True)
    return zc * lax.rsqrt(var + LN_EPS) * g + b


def _params(*semantics):
    return pltpu.CompilerParams(dimension_semantics=semantics, vmem_limit_bytes=VMEM_LIMIT)


def _a_mixer_kernel(x_ref, win_ref, conv_ref, wout_ref, g_ref, b_ref, o_ref, ubuf, *, tiles_per_seq):
    tm, d = x_ref.shape

    @pl.when(pl.program_id(0) % tiles_per_seq == 0)
    def _():
        ubuf[0:SUBLANES, :] = jnp.zeros((SUBLANES, d), F32)

    x = x_ref[...]
    p = jnp.dot(x.astype(BF16), win_ref[...], preferred_element_type=F32)
    ubuf[SUBLANES:SUBLANES + tm, :] = p[:, d:2 * d] * p[:, 2 * d:]
    cw = conv_ref[...]
    conv = (cw[2:3, :] * ubuf[SUBLANES:SUBLANES + tm, :]
            + cw[1:2, :] * ubuf[SUBLANES - 1:SUBLANES - 1 + tm, :]
            + cw[0:1, :] * ubuf[SUBLANES - 2:SUBLANES - 2 + tm, :])
    y = jnp.dot((p[:, :d] * conv).astype(BF16), wout_ref[...], preferred_element_type=F32)
    o_ref[...] = _layer_norm(ALPHA * x + y, g_ref[...], b_ref[...])
    ubuf[0:SUBLANES, :] = ubuf[tm:tm + SUBLANES, :]


def _a_mixer(x, w_in, conv_w, w_out, g, b, seq_len):
    t, d = x.shape
    tm = TM_MIX
    const = lambda i: (0, 0)
    return pl.pallas_call(
        functools.partial(_a_mixer_kernel, tiles_per_seq=seq_len // tm),
        grid=(t // tm,),
        in_specs=[
            pl.BlockSpec((tm, d), lambda i: (i, 0)),
            pl.BlockSpec((d, 3 * d), const),
            pl.BlockSpec((CONV_WIDTH, d), const),
            pl.BlockSpec((d, d), const),
            pl.BlockSpec((1, d), const),
            pl.BlockSpec((1, d), const),
        ],
        out_specs=pl.BlockSpec((tm, d), lambda i: (i, 0)),
        out_shape=jax.ShapeDtypeStruct((t, d), F32),
        scratch_shapes=[pltpu.VMEM((tm + SUBLANES, d), F32)],
        compiler_params=_params("arbitrary"),
        name="a_mixer",
    )(x, w_in, conv_w, w_out, g, b)


def _swiglu_partial(xb, wg_ref, wu_ref, wd_ref):
    gate = jnp.dot(xb, wg_ref[...], preferred_element_type=F32)
    up = jnp.dot(xb, wu_ref[...], preferred_element_type=F32)
    h = (gate * jax.nn.sigmoid(gate) * up).astype(BF16)
    return jnp.dot(h, wd_ref[...], preferred_element_type=F32)


def _dense_ffn_kernel(x_ref, wg_ref, wu_ref, wd_ref, g_ref, b_ref, o_ref, xb_ref, acc_ref):
    f = pl.program_id(1)

    @pl.when(f == 0)
    def _():
        xb_ref[...] = x_ref[...].astype(BF16)

    part = _swiglu_partial(xb_ref[...], wg_ref, wu_ref, wd_ref)

    @pl.when(f == 0)
    def _():
        acc_ref[...] = part

    @pl.when(f > 0)
    def _():
        acc_ref[...] += part

    @pl.when(f == pl.num_programs(1) - 1)
    def _():
        o_ref[...] = _layer_norm(ALPHA * x_ref[...] + acc_ref[...], g_ref[...], b_ref[...])


def _dense_ffn(x, w_gate, w_up, w_down, g, b, tf):
    t, d = x.shape
    d_ff = w_gate.shape[1]
    tm = TM_FFN
    const = lambda i, f: (0, 0)
    return pl.pallas_call(
        _dense_ffn_kernel,
        grid=(t // tm, d_ff // tf),
        in_specs=[
            pl.BlockSpec((tm, d), lambda i, f: (i, 0)),
            pl.BlockSpec((d, tf), lambda i, f: (0, f)),
            pl.BlockSpec((d, tf), lambda i, f: (0, f)),
            pl.BlockSpec((tf, d), lambda i, f: (f, 0)),
            pl.BlockSpec((1, d), const),
            pl.BlockSpec((1, d), const),
        ],
        out_specs=pl.BlockSpec((tm, d), lambda i, f: (i, 0)),
        out_shape=jax.ShapeDtypeStruct((t, d), F32),
        scratch_shapes=[pltpu.VMEM((tm, d), BF16), pltpu.VMEM((tm, d), F32)],
        compiler_params=_params("arbitrary", "arbitrary"),
        name="dense_ffn",
    )(x, w_gate, w_up, w_down, g, b)


def _expert_ffn_kernel(te_ref, nv_ref, x_ref, wg_ref, wu_ref, wd_ref, o_ref, xb_ref, acc_ref):
    del te_ref
    i = pl.program_id(0)
    f = pl.program_id(1)
    last = f == pl.num_programs(1) - 1
    valid = i < nv_ref[0]

    @pl.when(valid)
    def _():
        @pl.when(f == 0)
        def _():
            xb_ref[...] = x_ref[...].astype(BF16)

        part = _swiglu_partial(xb_ref[...], wg_ref, wu_ref, wd_ref)

        @pl.when(f == 0)
        def _():
            acc_ref[...] = part

        @pl.when(f > 0)
        def _():
            acc_ref[...] += part

        @pl.when(last)
        def _():
            o_ref[...] = acc_ref[...]

    @pl.when(jnp.logical_and(last, jnp.logical_not(valid)))
    def _():
        o_ref[...] = jnp.zeros(o_ref.shape, F32)


def _expert_ffn(xs, tile_expert, n_valid, w_gate, w_up, w_down, tf):
    n_rows, d = xs.shape
    d_exp = w_gate.shape[2]
    tm = TM_EXP
    nf = d_exp // tf

    def w_col(i, f, te, nv):
        return (te[i], 0, jnp.where(i < nv[0], f, nf - 1))

    def w_row(i, f, te, nv):
        return (te[i], jnp.where(i < nv[0], f, nf - 1), 0)

    return pl.pallas_call(
        _expert_ffn_kernel,
        grid_spec=pltpu.PrefetchScalarGridSpec(
            num_scalar_prefetch=2,
            grid=(n_rows // tm, nf),
            in_specs=[
                pl.BlockSpec((tm, d), lambda i, f, te, nv: (i, 0)),
                pl.BlockSpec((None, d, tf), w_col),
                pl.BlockSpec((None, d, tf), w_col),
                pl.BlockSpec((None, tf, d), w_row),
            ],
            out_specs=pl.BlockSpec((tm, d), lambda i, f, te, nv: (i, 0)),
            scratch_shapes=[pltpu.VMEM((tm, d), BF16), pltpu.VMEM((tm, d), F32)],
        ),
        out_shape=jax.ShapeDtypeStruct((n_rows, d), F32),
        compiler_params=_params("arbitrary", "arbitrary"),
        name="expert_ffn",
    )(tile_expert, n_valid, xs, w_gate, w_up, w_down)


def _split_bf16(v):
    hi = v.astype(BF16)
    lo = (v - hi.astype(F32)).astype(BF16)
    return hi, lo


def _router_kernel(x_ref, wr_ref, idx_ref, gate_ref):
    xh, xl = _split_bf16(x_ref[...])
    wh, wl = _split_bf16(wr_ref[...])
    logits = (lax.dot_general(wh, xh, _NT, preferred_element_type=F32)
              + lax.dot_general(wh, xl, _NT, preferred_element_type=F32)
              + lax.dot_general(wl, xh, _NT, preferred_element_type=F32))
    e = lax.broadcasted_iota(I32, logits.shape, 0)
    m1 = jnp.max(logits, axis=0, keepdims=True)
    i1 = jnp.min(jnp.where(logits == m1, e, N_EXPERTS), axis=0, keepdims=True)
    rest = jnp.where(e == i1, -jnp.inf, logits)
    m2 = jnp.max(rest, axis=0, keepdims=True)
    i2 = jnp.min(jnp.where(rest == m2, e, N_EXPERTS), axis=0, keepdims=True)
    r = jnp.exp(m2 - m1)
    idx_ref[...] = jnp.concatenate([i1, i2], axis=0)
    gate_ref[...] = jnp.concatenate([1.0 / (1.0 + r), r / (1.0 + r)], axis=0)


def _router(x, w_router_t):
    t, d = x.shape
    tm = TM_MIX
    return pl.pallas_call(
        _router_kernel,
        grid=(t // tm,),
        in_specs=[
            pl.BlockSpec((tm, d), lambda i: (i, 0)),
            pl.BlockSpec((N_EXPERTS, d), lambda i: (0, 0)),
        ],
        out_specs=[
            pl.BlockSpec((TOP_K, tm), lambda i: (0, i)),
            pl.BlockSpec((TOP_K, tm), lambda i: (0, i)),
        ],
        out_shape=[jax.ShapeDtypeStruct((TOP_K, t), I32), jax.ShapeDtypeStruct((TOP_K, t), F32)],
        compiler_params=_params("arbitrary"),
        name="router",
    )(x, w_router_t)


def _row_copy_kernel(sidx_ref, didx_ref, src_ref, *rest):
    dst_ref, sem = rest[-2], rest[-1]
    rows = sidx_ref.shape[-1]

    def row_copy(s, d):
        return pltpu.make_async_copy(src_ref.at[pl.ds(s, 1)], dst_ref.at[pl.ds(d, 1)], sem)

    def issue(j, carry):
        row_copy(sidx_ref[0, 0, j], didx_ref[0, 0, j]).start()
        return carry

    lax.fori_loop(0, rows, issue, 0, unroll=8)
    pltpu.make_async_copy(src_ref.at[pl.ds(0, rows)], dst_ref.at[pl.ds(0, rows)], sem).wait()


def _row_copy(src, src_idx, dst_idx, n_dst, dst_init=None):
    n = src_idx.shape[0]
    d = src.shape[1]
    rows = ROWS_PER_COPY_STEP
    steps = n // rows
    idx_spec = pl.BlockSpec((1, 1, rows), lambda i: (i, 0, 0), memory_space=pltpu.SMEM)
    any_spec = pl.BlockSpec(memory_space=pl.ANY)
    operands = [src_idx.reshape(steps, 1, rows), dst_idx.reshape(steps, 1, rows), src]
    in_specs = [idx_spec, idx_spec, any_spec]
    aliases = {}
    if dst_init is not None:
        operands.append(dst_init)
        in_specs.append(any_spec)
        aliases = {3: 0}
    return pl.pallas_call(
        _row_copy_kernel,
        grid=(steps,),
        in_specs=in_specs,
        out_specs=any_spec,
        out_shape=jax.ShapeDtypeStruct((n_dst, d), src.dtype),
        scratch_shapes=[pltpu.SemaphoreType.DMA(())],
        input_output_aliases=aliases,
        compiler_params=_params("arbitrary"),
        name="row_copy",
    )(*operands)


def _moe_combine_kernel(x_ref, y_ref, gate_ref, g_ref, b_ref, o_ref):
    d = x_ref.shape[1]
    gates = gate_ref[...]
    ff = gates[:, 0:1] * y_ref[:, :d] + gates[:, 1:2] * y_ref[:, d:]
    o_ref[...] = _layer_norm(ALPHA * x_ref[...] + ff, g_ref[...], b_ref[...])


def _moe_combine(x, y_pair, gates, g, b):
    t, d = x.shape
    tm = TM_MIX
    const = lambda i: (0, 0)
    return pl.pallas_call(
        _moe_combine_kernel,
        grid=(t // tm,),
        in_specs=[
            pl.BlockSpec((tm, d), lambda i: (i, 0)),
            pl.BlockSpec((tm, TOP_K * d), lambda i: (i, 0)),
            pl.BlockSpec((tm, TOP_K), lambda i: (i, 0)),
            pl.BlockSpec((1, d), const),
            pl.BlockSpec((1, d), const),
        ],
        out_specs=pl.BlockSpec((tm, d), lambda i: (i, 0)),
        out_shape=jax.ShapeDtypeStruct((t, d), F32),
        compiler_params=_params("arbitrary"),
        name="moe_combine",
    )(x, y_pair, gates, g, b)


def _moe_layer(x, w_router, w_gate, w_up, w_down, g, b):
    t, d = x.shape
    n_assign = TOP_K * t
    idx, gates = _router(x, w_router.T)

    e_flat = idx.T.reshape(n_assign)
    onehot = (e_flat[:, None] == jnp.arange(N_EXPERTS, dtype=I32)[None, :]).astype(I32)
    csum = jnp.cumsum(onehot, axis=0)
    rank = jnp.sum(csum * onehot, axis=1) - 1
    counts = csum[-1]
    padded = ((counts + TM_EXP - 1) // TM_EXP) * TM_EXP
    ends = jnp.cumsum(padded)
    pos = (jnp.sum((ends - padded)[None, :] * onehot, axis=1) + rank).astype(I32)
    n_tiles = n_assign // TM_EXP + N_EXPERTS
    tile_ids = jnp.arange(n_tiles, dtype=I32)
    tile_expert = jnp.minimum(
        jnp.sum((tile_ids[:, None] >= (ends // TM_EXP)[None, :]).astype(I32), axis=1), N_EXPERTS - 1).astype(I32)
    n_valid = (ends[-1:] // TM_EXP).astype(I32)
    n_rows = n_tiles * TM_EXP

    assign = jnp.arange(n_assign, dtype=I32)
    xs = _row_copy(x, assign // TOP_K, pos, n_rows, dst_init=jnp.zeros((n_rows, d), F32))
    ys = _expert_ffn(xs, tile_expert, n_valid, w_gate, w_up, w_down, tf=512)
    y_pair = _row_copy(ys, pos, assign, n_assign).reshape(t, TOP_K * d)
    return _moe_combine(x, y_pair, gates.T, g, b)


def _proj_kernel(x_ref, w_ref, o_ref):
    o_ref[...] = jnp.dot(x_ref[...].astype(BF16), w_ref[...], preferred_element_type=F32).astype(o_ref.dtype)


def _proj(x, w):
    t, d = x.shape
    n = w.shape[1]
    tm = TM_MIX
    return pl.pallas_call(
        _proj_kernel,
        grid=(t // tm,),
        in_specs=[pl.BlockSpec((tm, d), lambda i: (i, 0)), pl.BlockSpec((d, n), lambda i: (0, 0))],
        out_specs=pl.BlockSpec((tm, n), lambda i: (i, 0)),
        out_shape=jax.ShapeDtypeStruct((t, n), BF16),
        compiler_params=_params("arbitrary"),
        name="proj",
    )(x, w)


def _attn_kernel(q_ref, kp_ref, kc_ref, vp_ref, vc_ref, o_ref, lse_ref):
    n_heads = q_ref.shape[1] // HEAD_DIM
    qi = lax.broadcasted_iota(I32, (QBLK, 2 * QBLK), 0)
    ki = lax.broadcasted_iota(I32, (QBLK, 2 * QBLK), 1)
    first_ok = qi + jnp.where(pl.program_id(2) > 0, 0, QBLK)
    mask = jnp.logical_and(jnp.logical_or(ki >= first_ok, ki >= QBLK), ki - QBLK <= qi)
    lse_ref[...] = jnp.zeros(lse_ref.shape, F32)
    for h in range(n_heads):
        sl = slice(h * HEAD_DIM, (h + 1) * HEAD_DIM)
        k = jnp.concatenate([kp_ref[:, sl], kc_ref[:, sl]], axis=0)
        v = jnp.concatenate([vp_ref[:, sl], vc_ref[:, sl]], axis=0)
        s = lax.dot_general(q_ref[:, sl], k, _NT, preferred_element_type=F32) * (HEAD_DIM ** -0.5)
        s = jnp.where(mask, s, NEG_BIG)
        m = jnp.max(s, axis=-1, keepdims=True)
        p = jnp.exp(s - m)
        l = jnp.sum(p, axis=-1, keepdims=True)
        o = jnp.dot(p.astype(BF16), v, preferred_element_type=F32)
        o_ref[:, sl] = o / l
        lse_ref[:, h:h + 1] = m + jnp.log(l)


def _group_attention(q, k, v):
    bsz, dil, length, width = q.shape
    blk = (None, None, QBLK, width)
    cur = lambda b, r, n: (b, r, n, 0)
    prev = lambda b, r, n: (b, r, jnp.maximum(n - 1, 0), 0)
    return pl.pallas_call(
        _attn_kernel,
        grid=(bsz, dil, length // QBLK),
        in_specs=[
            pl.BlockSpec(blk, cur),
            pl.BlockSpec(blk, prev),
            pl.BlockSpec(blk, cur),
            pl.BlockSpec(blk, prev),
            pl.BlockSpec(blk, cur),
        ],
        out_specs=[pl.BlockSpec(blk, cur), pl.BlockSpec((None, None, QBLK, LANES), cur)],
        out_shape=[
            jax.ShapeDtypeStruct((bsz, dil, length, width), F32),
            jax.ShapeDtypeStruct((bsz, dil, length, LANES), F32),
        ],
        compiler_params=_params("arbitrary", "arbitrary", "arbitrary"),
        name="group_attention",
    )(q, k, k, v, v)


def _attn_out_kernel(x_ref, o0_ref, o1_ref, o2_ref, l0_ref, l1_ref, l2_ref, wo_ref, g_ref, b_ref, out_ref):
    width = o0_ref.shape[1]
    lses = [l0_ref[...], l1_ref[...], l2_ref[...]]
    top = jnp.maximum(jnp.maximum(lses[0], lses[1]), lses[2])
    es = [jnp.exp(l - top) for l in lses]
    den = es[0] + es[1] + es[2]
    head = lax.broadcasted_iota(I32, (LANES, width), 0)
    lane = lax.broadcasted_iota(I32, (LANES, width), 1)
    spread = (lane // HEAD_DIM == head).astype(BF16)
    mixed = jnp.zeros(o0_ref.shape, F32)
    for e, o_ref in zip(es, (o0_ref, o1_ref, o2_ref)):
        hi, lo = _split_bf16(e / den)
        wide = (jnp.dot(hi, spread, preferred_element_type=F32)
                + jnp.dot(lo, spread, preferred_element_type=F32))
        mixed = mixed + wide * o_ref[...]
    y = jnp.dot(mixed.astype(BF16), wo_ref[...], preferred_element_type=F32)
    out_ref[...] = _layer_norm(ALPHA * x_ref[...] + y, g_ref[...], b_ref[...])


def _attn_out(x, outs, lses, w_o, g, b):
    t, d = x.shape
    width = w_o.shape[0]
    tm = TM_MIX
    const = lambda i: (0, 0)
    row = lambda i: (i, 0)
    return pl.pallas_call(
        _attn_out_kernel,
        grid=(t // tm,),
        in_specs=([pl.BlockSpec((tm, d), row)]
                  + [pl.BlockSpec((tm, width), row)] * N_GROUPS
                  + [pl.BlockSpec((tm, LANES), row)] * N_GROUPS
                  + [pl.BlockSpec((width, d), const), pl.BlockSpec((1, d), const), pl.BlockSpec((1, d), const)]),
        out_specs=pl.BlockSpec((tm, d), row),
        out_shape=jax.ShapeDtypeStruct((t, d), F32),
        compiler_params=_params("arbitrary"),
        name="attn_out",
    )(x, *outs, *lses, w_o, g, b)


def _to_residue(a, dil):
    bsz, seq, width = a.shape
    return a.reshape(bsz, seq // dil, dil, width).transpose(0, 2, 1, 3)


def _from_residue(a):
    bsz, dil, length, width = a.shape
    return a.transpose(0, 2, 1, 3).reshape(bsz * length * dil, width)


def _attention_layer(x, bsz, w_q, w_o, kv_groups, g, b):
    t, d = x.shape
    width = w_o.shape[0]
    q = _proj(x, w_q).reshape(bsz, t // bsz, N_GROUPS * width)
    outs, lses = [], []
    for gi, (_, dil) in enumerate(ATT_GROUPS):
        k_res, v_res = kv_groups[gi]
        o, lse = _group_attention(_to_residue(q[:, :, gi * width:(gi + 1) * width], dil), k_res, v_res)
        outs.append(_from_residue(o))
        lses.append(_from_residue(lse))
    return _attn_out(x, outs, lses, w_o, g, b)


def kernel(x, a_w_in, a_conv, a_w_out, kv_w, b_w_q, b_w_o, ffn_w_gate, ffn_w_up, ffn_w_down,
           moe_w_router, moe_w_gate, moe_w_up, moe_w_down, ln_g, ln_b):
    bsz, seq, d = x.shape
    width = b_w_o.shape[1]
    h = x.reshape(bsz * seq, d)
    ln_g = ln_g.reshape(DEPTH, 2, 1, d)
    ln_b = ln_b.reshape(DEPTH, 2, 1, d)
    kv_groups = None
    for l in range(DEPTH):
        if l < N_A_LAYERS:
            h = _a_mixer(h, a_w_in[l].astype(BF16), a_conv[l], a_w_out[l].astype(BF16),
                         ln_g[l, 0], ln_b[l, 0], seq)
        else:
            if kv_groups is None:
                kv = _proj(h, kv_w.astype(BF16)).reshape(bsz, seq, N_GROUPS, 2, width)
                kv_groups = [(_to_residue(kv[:, :, gi, 0], dil), _to_residue(kv[:, :, gi, 1], dil))
                             for gi, (_, dil) in enumerate(ATT_GROUPS)]
            j = l - N_A_LAYERS
            h = _attention_layer(h, bsz, b_w_q[j].astype(BF16), b_w_o[j].astype(BF16), kv_groups,
                                 ln_g[l, 0], ln_b[l, 0])
        i = l // 2
        if l % 2 == 0:
            h = _dense_ffn(h, ffn_w_gate[i].astype(BF16), ffn_w_up[i].astype(BF16),
                           ffn_w_down[i].astype(BF16), ln_g[l, 1], ln_b[l, 1], tf=1408)
        else:
            h = _moe_layer(h, moe_w_router[i], moe_w_gate[i].astype(BF16), moe_w_up[i].astype(BF16),
                           moe_w_down[i].astype(BF16), ln_g[l, 1], ln_b[l, 1])
    return h.reshape(bsz, seq, d)
```

```python
import functools

import jax
import jax.numpy as jnp
from jax import lax
from jax.experimental import pallas as pl
from jax.experimental.pallas import tpu as pltpu

F32 = jnp.float32
BF16 = jnp.bfloat16
I32 = jnp.int32

DEPTH = 4
N_A_LAYERS = DEPTH // 2
CONV_WIDTH = 3
ATT_GROUPS = ((128, 1), (512, 4), (2048, 16))
N_GROUPS = len(ATT_GROUPS)
HEAD_DIM = 64
N_EXPERTS = 8
TOP_K = 2
ALPHA = (2.0 * DEPTH) ** 0.25
LN_EPS = 1e-5

QBLK = 128
assert all(w // d == QBLK for w, d in ATT_GROUPS)

LANES = 128
SUBLANES = 8
VMEM_LIMIT = 56 * 1024 * 1024
NEG_BIG = -1e30

TM_MIX = 512
TM_FFN = 1024
TM_EXP = 1024

_NT = (((1,), (1,)), ((), ()))


def _layer_norm(z, g, b):
    mu = jnp.mean(z, axis=-1, keepdims=True)
    zc = z - mu
    var = jnp.mean(zc * zc, axis=-1, keepdims=True)
    return zc * lax.rsqrt(var + LN_EPS) * g + b


def _params(*semantics):
    return pltpu.CompilerParams(dimension_semantics=semantics, vmem_limit_bytes=VMEM_LIMIT)


def _a_mixer_kernel(x_ref, win_ref, conv_ref, wout_ref, g_ref, b_ref, o_ref, ubuf, *, tiles_per_seq):
    tm, d = x_ref.shape

    @pl.when(pl.program_id(0) % tiles_per_seq == 0)
    def _():
        ubuf[0:SUBLANES, :] = jnp.zeros((SUBLANES, d), F32)

    x = x_ref[...]
    p = jnp.dot(x.astype(BF16), win_ref[...], preferred_element_type=F32)
    ubuf[SUBLANES:SUBLANES + tm, :] = p[:, d:2 * d] * p[:, 2 * d:]
    cw = conv_ref[...]
    conv = (cw[2:3, :] * ubuf[SUBLANES:SUBLANES + tm, :]
            + cw[1:2, :] * ubuf[SUBLANES - 1:SUBLANES - 1 + tm, :]
            + cw[0:1, :] * ubuf[SUBLANES - 2:SUBLANES - 2 + tm, :])
    y = jnp.dot((p[:, :d] * conv).astype(BF16), wout_ref[...], preferred_element_type=F32)
    o_ref[...] = _layer_norm(ALPHA * x + y, g_ref[...], b_ref[...])
    ubuf[0:SUBLANES, :] = ubuf[tm:tm + SUBLANES, :]


def _a_mixer(x, w_in, conv_w, w_out, g, b, seq_len):
    t, d = x.shape
    tm = TM_MIX
    const = lambda i: (0, 0)
    return pl.pallas_call(
        functools.partial(_a_mixer_kernel, tiles_per_seq=seq_len // tm),
        grid=(t // tm,),
        in_specs=[
            pl.BlockSpec((tm, d), lambda i: (i, 0)),
            pl.BlockSpec((d, 3 * d), const),
            pl.BlockSpec((CONV_WIDTH, d), const),
            pl.BlockSpec((d, d), const),
            pl.BlockSpec((1, d), const),
            pl.BlockSpec((1, d), const),
        ],
        out_specs=pl.BlockSpec((tm, d), lambda i: (i, 0)),
        out_shape=jax.ShapeDtypeStruct((t, d), F32),
        scratch_shapes=[pltpu.VMEM((tm + SUBLANES, d), F32)],
        compiler_params=_params("arbitrary"),
        name="a_mixer",
    )(x, w_in, conv_w, w_out, g, b)


def _swiglu_partial(xb, wg_ref, wu_ref, wd_ref):
    gate = jnp.dot(xb, wg_ref[...], preferred_element_type=F32)
    up = jnp.dot(xb, wu_ref[...], preferred_element_type=F32)
    h = (gate * jax.nn.sigmoid(gate) * up).astype(BF16)
    return jnp.dot(h, wd_ref[...], preferred_element_type=F32)


def _dense_ffn_kernel(x_ref, wg_ref, wu_ref, wd_ref, g_ref, b_ref, o_ref, xb_ref, acc_ref):
    f = pl.program_id(1)

    @pl.when(f == 0)
    def _():
        xb_ref[...] = x_ref[...].astype(BF16)

    part = _swiglu_partial(xb_ref[...], wg_ref, wu_ref, wd_ref)

    @pl.when(f == 0)
    def _():
        acc_ref[...] = part

    @pl.when(f > 0)
    def _():
        acc_ref[...] += part

    @pl.when(f == pl.num_programs(1) - 1)
    def _():
        o_ref[...] = _layer_norm(ALPHA * x_ref[...] + acc_ref[...], g_ref[...], b_ref[...])


def _dense_ffn(x, w_gate, w_up, w_down, g, b, tf):
    t, d = x.shape
    d_ff = w_gate.shape[1]
    tm = TM_FFN
    const = lambda i, f: (0, 0)
    return pl.pallas_call(
        _dense_ffn_kernel,
        grid=(t // tm, d_ff // tf),
        in_specs=[
            pl.BlockSpec((tm, d), lambda i, f: (i, 0)),
            pl.BlockSpec((d, tf), lambda i, f: (0, f)),
            pl.BlockSpec((d, tf), lambda i, f: (0, f)),
            pl.BlockSpec((tf, d), lambda i, f: (f, 0)),
            pl.BlockSpec((1, d), const),
            pl.BlockSpec((1, d), const),
        ],
        out_specs=pl.BlockSpec((tm, d), lambda i, f: (i, 0)),
        out_shape=jax.ShapeDtypeStruct((t, d), F32),
        scratch_shapes=[pltpu.VMEM((tm, d), BF16), pltpu.VMEM((tm, d), F32)],
        compiler_params=_params("arbitrary", "arbitrary"),
        name="dense_ffn",
    )(x, w_gate, w_up, w_down, g, b)


def _rows_to_slabs(slab_ref, rows):
    for c in range(slab_ref.shape[1]):
        slab_ref[:, c, :] = rows[:, c * LANES:(c + 1) * LANES]


def _slabs_to_rows(slab_ref):
    return jnp.concatenate([slab_ref[:, c, :] for c in range(slab_ref.shape[1])], axis=1)


def _expert_ffn_kernel(te_ref, nv_ref, x_ref, wg_ref, wu_ref, wd_ref, o_ref, xb_ref, acc_ref):
    del te_ref
    i = pl.program_id(0)
    f = pl.program_id(1)
    last = f == pl.num_programs(1) - 1
    valid = i < nv_ref[0]

    @pl.when(valid)
    def _():
        @pl.when(f == 0)
        def _():
            xb_ref[...] = _slabs_to_rows(x_ref).astype(BF16)

        part = _swiglu_partial(xb_ref[...], wg_ref, wu_ref, wd_ref)

        @pl.when(f == 0)
        def _():
            acc_ref[...] = part

        @pl.when(f > 0)
        def _():
            acc_ref[...] += part

        @pl.when(last)
        def _():
            _rows_to_slabs(o_ref, acc_ref[...])

    @pl.when(jnp.logical_and(last, jnp.logical_not(valid)))
    def _():
        o_ref[...] = jnp.zeros(o_ref.shape, F32)


def _expert_ffn(xs, tile_expert, n_valid, w_gate, w_up, w_down, tf):
    n_rows = xs.shape[0]
    d = w_gate.shape[1]
    d_exp = w_gate.shape[2]
    tm = TM_EXP
    nf = d_exp // tf
    slab = (tm,) + xs.shape[1:]

    def w_col(i, f, te, nv):
        return (te[i], 0, jnp.where(i < nv[0], f, nf - 1))

    def w_row(i, f, te, nv):
        return (te[i], jnp.where(i < nv[0], f, nf - 1), 0)

    return pl.pallas_call(
        _expert_ffn_kernel,
        grid_spec=pltpu.PrefetchScalarGridSpec(
            num_scalar_prefetch=2,
            grid=(n_rows // tm, nf),
            in_specs=[
                pl.BlockSpec(slab, lambda i, f, te, nv: (i, 0, 0)),
                pl.BlockSpec((None, d, tf), w_col),
                pl.BlockSpec((None, d, tf), w_col),
                pl.BlockSpec((None, tf, d), w_row),
            ],
            out_specs=pl.BlockSpec(slab, lambda i, f, te, nv: (i, 0, 0)),
            scratch_shapes=[pltpu.VMEM((tm, d), BF16), pltpu.VMEM((tm, d), F32)],
        ),
        out_shape=jax.ShapeDtypeStruct(xs.shape, F32),
        compiler_params=_params("arbitrary", "arbitrary"),
        name="expert_ffn",
    )(tile_expert, n_valid, xs, w_gate, w_up, w_down)


def _split_bf16(v):
    hi = v.astype(BF16)
    lo = (v - hi.astype(F32)).astype(BF16)
    return hi, lo


def _router_kernel(x_ref, wr_ref, idx_ref, gate_ref):
    xh, xl = _split_bf16(x_ref[...])
    wh, wl = _split_bf16(wr_ref[...])
    logits = (lax.dot_general(wh, xh, _NT, preferred_element_type=F32)
              + lax.dot_general(wh, xl, _NT, preferred_element_type=F32)
              + lax.dot_general(wl, xh, _NT, preferred_element_type=F32))
    e = lax.broadcasted_iota(I32, logits.shape, 0)
    m1 = jnp.max(logits, axis=0, keepdims=True)
    i1 = jnp.min(jnp.where(logits == m1, e, N_EXPERTS), axis=0, keepdims=True)
    rest = jnp.where(e == i1, -jnp.inf, logits)
    m2 = jnp.max(rest, axis=0, keepdims=True)
    i2 = jnp.min(jnp.where(rest == m2, e, N_EXPERTS), axis=0, keepdims=True)
    r = jnp.exp(m2 - m1)
    idx_ref[...] = jnp.concatenate([i1, i2], axis=0)
    gate_ref[...] = jnp.concatenate([1.0 / (1.0 + r), r / (1.0 + r)], axis=0)


def _router(x, w_router_t):
    t, d = x.shape
    tm = TM_MIX
    return pl.pallas_call(
        _router_kernel,
        grid=(t // tm,),
        in_specs=[
            pl.BlockSpec((tm, d), lambda i: (i, 0)),
            pl.BlockSpec((N_EXPERTS, d), lambda i: (0, 0)),
        ],
        out_specs=[
            pl.BlockSpec((TOP_K, tm), lambda i: (0, i)),
            pl.BlockSpec((TOP_K, tm), lambda i: (0, i)),
        ],
        out_shape=[jax.ShapeDtypeStruct((TOP_K, t), I32), jax.ShapeDtypeStruct((TOP_K, t), F32)],
        compiler_params=_params("arbitrary"),
        name="router",
    )(x, w_router_t)


def _dispatch_kernel(pos_ref, x_ref, init_ref, xs_ref, slab_ref, sem):
    del init_ref
    tm = x_ref.shape[0]
    _rows_to_slabs(slab_ref, x_ref[...])

    def issue(j, carry):
        for k in range(TOP_K):
            pltpu.make_async_copy(slab_ref.at[j], xs_ref.at[pos_ref[0, 0, TOP_K * j + k]], sem).start()
        return carry

    lax.fori_loop(0, tm, issue, 0, unroll=4)
    for k in range(TOP_K):
        pltpu.make_async_copy(slab_ref, xs_ref.at[pl.ds(0, tm)], sem).wait()


def _dispatch(x, pos, n_rows):
    t, d = x.shape
    tm = TM_MIX
    slab = (d // LANES, LANES)
    any_spec = pl.BlockSpec(memory_space=pl.ANY)
    return pl.pallas_call(
        _dispatch_kernel,
        grid=(t // tm,),
        in_specs=[
            pl.BlockSpec((1, 1, TOP_K * tm), lambda i: (i, 0, 0), memory_space=pltpu.SMEM),
            pl.BlockSpec((tm, d), lambda i: (i, 0)),
            any_spec,
        ],
        out_specs=any_spec,
        out_shape=jax.ShapeDtypeStruct((n_rows,) + slab, F32),
        scratch_shapes=[pltpu.VMEM((tm,) + slab, F32), pltpu.SemaphoreType.DMA(())],
        input_output_aliases={2: 0},
        compiler_params=_params("arbitrary"),
        name="moe_dispatch",
    )(pos.reshape(t // tm, 1, TOP_K * tm), x, jnp.zeros((n_rows,) + slab, F32))


def _moe_combine_kernel(pos_ref, pos_next_ref, x_ref, gate_ref, g_ref, b_ref, ys_ref, o_ref, ybuf, sem):
    tm, d = x_ref.shape
    i = pl.program_id(0)
    slot = i % 2

    def gather(idx_ref, s):
        def issue(j, carry):
            for k in range(TOP_K):
                pltpu.make_async_copy(ys_ref.at[idx_ref[0, 0, TOP_K * j + k]], ybuf.at[s, k, j], sem.at[s]).start()
            return carry

        lax.fori_loop(0, tm, issue, 0, unroll=4)

    @pl.when(i == 0)
    def _():
        gather(pos_ref, 0)

    @pl.when(i + 1 < pl.num_programs(0))
    def _():
        gather(pos_next_ref, 1 - slot)

    for k in range(TOP_K):
        pltpu.make_async_copy(ys_ref.at[pl.ds(0, tm)], ybuf.at[slot, k], sem.at[slot]).wait()
    gates = gate_ref[...]
    ff = jnp.zeros((tm, d), F32)
    for k in range(TOP_K):
        ff = ff + gates[:, k:k + 1] * _slabs_to_rows(ybuf.at[slot, k])
    o_ref[...] = _layer_norm(ALPHA * x_ref[...] + ff, g_ref[...], b_ref[...])


def _moe_combine(x, ys, pos, gates, g, b):
    t, d = x.shape
    tm = TM_MIX
    steps = t // tm
    const = lambda i: (0, 0)
    pos = pos.reshape(steps, 1, TOP_K * tm)
    return pl.pallas_call(
        _moe_combine_kernel,
        grid=(steps,),
        in_specs=[
            pl.BlockSpec((1, 1, TOP_K * tm), lambda i: (i, 0, 0), memory_space=pltpu.SMEM),
            pl.BlockSpec((1, 1, TOP_K * tm), lambda i: (jnp.minimum(i + 1, steps - 1), 0, 0), memory_space=pltpu.SMEM),
            pl.BlockSpec((tm, d), lambda i: (i, 0)),
            pl.BlockSpec((tm, TOP_K), lambda i: (i, 0)),
            pl.BlockSpec((1, d), const),
            pl.BlockSpec((1, d), const),
            pl.BlockSpec(memory_space=pl.ANY),
        ],
        out_specs=pl.BlockSpec((tm, d), lambda i: (i, 0)),
        out_shape=jax.ShapeDtypeStruct((t, d), F32),
        scratch_shapes=[pltpu.VMEM((2, TOP_K, tm) + ys.shape[1:], F32), pltpu.SemaphoreType.DMA((2,))],
        compiler_params=_params("arbitrary"),
        name="moe_combine",
    )(pos, pos, x, gates, g, b, ys)


def _moe_layer(x, w_router, w_gate, w_up, w_down, g, b):
    t, d = x.shape
    n_assign = TOP_K * t
    idx, gates = _router(x, w_router.T)

    e_flat = idx.T.reshape(n_assign)
    onehot = (e_flat[:, None] == jnp.arange(N_EXPERTS, dtype=I32)[None, :]).astype(I32)
    csum = jnp.cumsum(onehot, axis=0)
    rank = jnp.sum(csum * onehot, axis=1) - 1
    counts = csum[-1]
    padded = ((counts + TM_EXP - 1) // TM_EXP) * TM_EXP
    ends = jnp.cumsum(padded)
    pos = (jnp.sum((ends - padded)[None, :] * onehot, axis=1) + rank).astype(I32)
    n_tiles = n_assign // TM_EXP + N_EXPERTS
    tile_ids = jnp.arange(n_tiles, dtype=I32)
    tile_expert = jnp.minimum(
        jnp.sum((tile_ids[:, None] >= (ends // TM_EXP)[None, :]).astype(I32), axis=1), N_EXPERTS - 1).astype(I32)
    n_valid = (ends[-1:] // TM_EXP).astype(I32)

    xs = _dispatch(x, pos, n_tiles * TM_EXP)
    ys = _expert_ffn(xs, tile_expert, n_valid, w_gate, w_up, w_down, tf=512)
    return _moe_combine(x, ys, pos, gates.T, g, b)


def _proj_kernel(x_ref, w_ref, o_ref):
    o_ref[...] = jnp.dot(x_ref[...].astype(BF16), w_ref[...], preferred_element_type=F32).astype(o_ref.dtype)


def _proj(x, w):
    t, d = x.shape
    n = w.shape[1]
    tm = TM_MIX
    return pl.pallas_call(
        _proj_kernel,
        grid=(t // tm,),
        in_specs=[pl.BlockSpec((tm, d), lambda i: (i, 0)), pl.BlockSpec((d, n), lambda i: (0, 0))],
        out_specs=pl.BlockSpec((tm, n), lambda i: (i, 0)),
        out_shape=jax.ShapeDtypeStruct((t, n), BF16),
        compiler_params=_params("arbitrary"),
        name="proj",
    )(x, w)


def _attn_kernel(q_ref, kp_ref, kc_ref, vp_ref, vc_ref, o_ref, lse_ref):
    n_heads = q_ref.shape[1] // HEAD_DIM
    qi = lax.broadcasted_iota(I32, (QBLK, 2 * QBLK), 0)
    ki = lax.broadcasted_iota(I32, (QBLK, 2 * QBLK), 1)
    first_ok = qi + jnp.where(pl.program_id(2) > 0, 0, QBLK)
    mask = jnp.logical_and(jnp.logical_or(ki >= first_ok, ki >= QBLK), ki - QBLK <= qi)
    lse_ref[...] = jnp.zeros(lse_ref.shape, F32)
    for h in range(n_heads):
        sl = slice(h * HEAD_DIM, (h + 1) * HEAD_DIM)
        k = jnp.concatenate([kp_ref[:, sl], kc_ref[:, sl]], axis=0)
        v = jnp.concatenate([vp_ref[:, sl], vc_ref[:, sl]], axis=0)
        s = lax.dot_general(q_ref[:, sl], k, _NT, preferred_element_type=F32) * (HEAD_DIM ** -0.5)
        s = jnp.where(mask, s, NEG_BIG)
        m = jnp.max(s, axis=-1, keepdims=True)
        p = jnp.exp(s - m)
        l = jnp.sum(p, axis=-1, keepdims=True)
        o = jnp.dot(p.astype(BF16), v, preferred_element_type=F32)
        o_ref[:, sl] = o / l
        lse_ref[:, h:h + 1] = m + jnp.log(l)


def _group_attention(q, k, v):
    bsz, dil, length, width = q.shape
    blk = (None, None, QBLK, width)
    cur = lambda b, r, n: (b, r, n, 0)
    prev = lambda b, r, n: (b, r, jnp.maximum(n - 1, 0), 0)
    return pl.pallas_call(
        _attn_kernel,
        grid=(bsz, dil, length // QBLK),
        in_specs=[
            pl.BlockSpec(blk, cur),
            pl.BlockSpec(blk, prev),
            pl.BlockSpec(blk, cur),
            pl.BlockSpec(blk, prev),
            pl.BlockSpec(blk, cur),
        ],
        out_specs=[pl.BlockSpec(blk, cur), pl.BlockSpec((None, None, QBLK, LANES), cur)],
        out_shape=[
            jax.ShapeDtypeStruct((bsz, dil, length, width), F32),
            jax.ShapeDtypeStruct((bsz, dil, length, LANES), F32),
        ],
        compiler_params=_params("arbitrary", "arbitrary", "arbitrary"),
        name="group_attention",
    )(q, k, k, v, v)


def _attn_out_kernel(x_ref, o0_ref, o1_ref, o2_ref, l0_ref, l1_ref, l2_ref, wo_ref, g_ref, b_ref, out_ref):
    width = o0_ref.shape[1]
    lses = [l0_ref[...], l1_ref[...], l2_ref[...]]
    top = jnp.maximum(jnp.maximum(lses[0], lses[1]), lses[2])
    es = [jnp.exp(l - top) for l in lses]
    den = es[0] + es[1] + es[2]
    head = lax.broadcasted_iota(I32, (LANES, width), 0)
    lane = lax.broadcasted_iota(I32, (LANES, width), 1)
    spread = (lane // HEAD_DIM == head).astype(BF16)
    mixed = jnp.zeros(o0_ref.shape, F32)
    for e, o_ref in zip(es, (o0_ref, o1_ref, o2_ref)):
        hi, lo = _split_bf16(e / den)
        wide = (jnp.dot(hi, spread, preferred_element_type=F32)
                + jnp.dot(lo, spread, preferred_element_type=F32))
        mixed = mixed + wide * o_ref[...]
    y = jnp.dot(mixed.astype(BF16), wo_ref[...], preferred_element_type=F32)
    out_ref[...] = _layer_norm(ALPHA * x_ref[...] + y, g_ref[...], b_ref[...])


def _attn_out(x, outs, lses, w_o, g, b):
    t, d = x.shape
    width = w_o.shape[0]
    tm = TM_MIX
    const = lambda i: (0, 0)
    row = lambda i: (i, 0)
    return pl.pallas_call(
        _attn_out_kernel,
        grid=(t // tm,),
        in_specs=([pl.BlockSpec((tm, d), row)]
                  + [pl.BlockSpec((tm, width), row)] * N_GROUPS
                  + [pl.BlockSpec((tm, LANES), row)] * N_GROUPS
                  + [pl.BlockSpec((width, d), const), pl.BlockSpec((1, d), const), pl.BlockSpec((1, d), const)]),
        out_specs=pl.BlockSpec((tm, d), row),
        out_shape=jax.ShapeDtypeStruct((t, d), F32),
        compiler_params=_params("arbitrary"),
        name="attn_out",
    )(x, *outs, *lses, w_o, g, b)


def _to_residue(a, dil):
    bsz, seq, width = a.shape
    return a.reshape(bsz, seq // dil, dil, width).transpose(0, 2, 1, 3)


def _from_residue(a):
    bsz, dil, length, width = a.shape
    return a.transpose(0, 2, 1, 3).reshape(bsz * length * dil, width)


def _attention_layer(x, bsz, w_q, w_o, kv_groups, g, b):
    t, d = x.shape
    width = w_o.shape[0]
    q = _proj(x, w_q).reshape(bsz, t // bsz, N_GROUPS * width)
    outs, lses = [], []
    for gi, (_, dil) in enumerate(ATT_GROUPS):
        k_res, v_res = kv_groups[gi]
        o, lse = _group_attention(_to_residue(q[:, :, gi * width:(gi + 1) * width], dil), k_res, v_res)
        outs.append(_from_residue(o))
        lses.append(_from_residue(lse))
    return _attn_out(x, outs, lses, w_o, g, b)


def kernel(x, a_w_in, a_conv, a_w_out, kv_w, b_w_q, b_w_o, ffn_w_gate, ffn_w_up, ffn_w_down,
           moe_w_router, moe_w_gate, moe_w_up, moe_w_down, ln_g, ln_b):
    bsz, seq, d = x.shape
    width = b_w_o.shape[1]
    h = x.reshape(bsz * seq, d)
    ln_g = ln_g.reshape(DEPTH, 2, 1, d)
    ln_b = ln_b.reshape(DEPTH, 2, 1, d)
    kv_groups = None
    for l in range(DEPTH):
        if l < N_A_LAYERS:
            h = _a_mixer(h, a_w_in[l].astype(BF16), a_conv[l], a_w_out[l].astype(BF16),
                         ln_g[l, 0], ln_b[l, 0], seq)
        else:
            if kv_groups is None:
                kv = _proj(h, kv_w.astype(BF16)).reshape(bsz, seq, N_GROUPS * 2 * width)
                kv_groups = [tuple(_to_residue(kv[:, :, (2 * gi + j) * width:(2 * gi + j + 1) * width], dil)
                                   for j in range(2))
                             for gi, (_, dil) in enumerate(ATT_GROUPS)]
            j = l - N_A_LAYERS
            h = _attention_layer(h, bsz, b_w_q[j].astype(BF16), b_w_o[j].astype(BF16), kv_groups,
                                 ln_g[l, 0], ln_b[l, 0])
        i = l // 2
        if l % 2 == 0:
            h = _dense_ffn(h, ffn_w_gate[i].astype(BF16), ffn_w_up[i].astype(BF16),
                           ffn_w_down[i].astype(BF16), ln_g[l, 1], ln_b[l, 1], tf=1408)
        else:
            h = _moe_layer(h, moe_w_router[i], moe_w_gate[i].astype(BF16), moe_w_up[i].astype(BF16),
                           moe_w_down[i].astype(BF16), ln_g[l, 1], ln_b[l, 1])
    return h.reshape(bsz, seq, d)
```

```python
import functools

import jax
import jax.numpy as jnp
from jax import lax
from jax.experimental import pallas as pl
from jax.experimental.pallas import tpu as pltpu

F32 = jnp.float32
BF16 = jnp.bfloat16
I32 = jnp.int32

DEPTH = 4
N_A_LAYERS = DEPTH // 2
CONV_WIDTH = 3
ATT_GROUPS = ((128, 1), (512, 4), (2048, 16))
N_GROUPS = len(ATT_GROUPS)
HEAD_DIM = 64
N_EXPERTS = 8
TOP_K = 2
ALPHA = (2.0 * DEPTH) ** 0.25
LN_EPS = 1e-5

QBLK = 128
assert all(w // d == QBLK for w, d in ATT_GROUPS)

LANES = 128
SUBLANES = 8
VMEM_LIMIT = 56 * 1024 * 1024
NEG_BIG = -1e30

TM_MIX = 512
TM_FFN = 1024
TM_EXP = 1024

_NT = (((1,), (1,)), ((), ()))


def _layer_norm(z, g, b):
    mu = jnp.mean(z, axis=-1, keepdims=True)
    zc = z - mu
    var = jnp.mean(zc * zc, axis=-1, keepdims=True)
    return zc * lax.rsqrt(var + LN_EPS) * g + b


def _params(*semantics):
    return pltpu.CompilerParams(dimension_semantics=semantics, vmem_limit_bytes=VMEM_LIMIT)


def _a_mixer_kernel(x_ref, win_ref, conv_ref, wout_ref, g_ref, b_ref, o_ref, ubuf, *, tiles_per_seq):
    tm, d = x_ref.shape

    @pl.when(pl.program_id(0) % tiles_per_seq == 0)
    def _():
        ubuf[0:SUBLANES, :] = jnp.zeros((SUBLANES, d), F32)

    x = x_ref[...]
    p = jnp.dot(x.astype(BF16), win_ref[...], preferred_element_type=F32)
    ubuf[SUBLANES:SUBLANES + tm, :] = p[:, d:2 * d] * p[:, 2 * d:]
    cw = conv_ref[...]
    conv = (cw[2:3, :] * ubuf[SUBLANES:SUBLANES + tm, :]
            + cw[1:2, :] * ubuf[SUBLANES - 1:SUBLANES - 1 + tm, :]
            + cw[0:1, :] * ubuf[SUBLANES - 2:SUBLANES - 2 + tm, :])
    y = jnp.dot((p[:, :d] * conv).astype(BF16), wout_ref[...], preferred_element_type=F32)
    o_ref[...] = _layer_norm(ALPHA * x + y, g_ref[...], b_ref[...])
    ubuf[0:SUBLANES, :] = ubuf[tm:tm + SUBLANES, :]


def _a_mixer(x, w_in, conv_w, w_out, g, b, seq_len):
    t, d = x.shape
    tm = TM_MIX
    const = lambda i: (0, 0)
    return pl.pallas_call(
        functools.partial(_a_mixer_kernel, tiles_per_seq=seq_len // tm),
        grid=(t // tm,),
        in_specs=[
            pl.BlockSpec((tm, d), lambda i: (i, 0)),
            pl.BlockSpec((d, 3 * d), const),
            pl.BlockSpec((CONV_WIDTH, d), const),
            pl.BlockSpec((d, d), const),
            pl.BlockSpec((1, d), const),
            pl.BlockSpec((1, d), const),
        ],
        out_specs=pl.BlockSpec((tm, d), lambda i: (i, 0)),
        out_shape=jax.ShapeDtypeStruct((t, d), F32),
        scratch_shapes=[pltpu.VMEM((tm + SUBLANES, d), F32)],
        compiler_params=_params("arbitrary"),
        name="a_mixer",
    )(x, w_in, conv_w, w_out, g, b)


def _swiglu_partial(xb, wg_ref, wu_ref, wd_ref):
    gate = jnp.dot(xb, wg_ref[...], preferred_element_type=F32)
    up = jnp.dot(xb, wu_ref[...], preferred_element_type=F32)
    h = (gate * jax.nn.sigmoid(gate) * up).astype(BF16)
    return jnp.dot(h, wd_ref[...], preferred_element_type=F32)


def _dense_ffn_kernel(x_ref, wg_ref, wu_ref, wd_ref, g_ref, b_ref, o_ref, xb_ref, acc_ref):
    f = pl.program_id(1)

    @pl.when(f == 0)
    def _():
        xb_ref[...] = x_ref[...].astype(BF16)

    part = _swiglu_partial(xb_ref[...], wg_ref, wu_ref, wd_ref)

    @pl.when(f == 0)
    def _():
        acc_ref[...] = part

    @pl.when(f > 0)
    def _():
        acc_ref[...] += part

    @pl.when(f == pl.num_programs(1) - 1)
    def _():
        o_ref[...] = _layer_norm(ALPHA * x_ref[...] + acc_ref[...], g_ref[...], b_ref[...])


def _dense_ffn(x, w_gate, w_up, w_down, g, b, tf):
    t, d = x.shape
    d_ff = w_gate.shape[1]
    tm = TM_FFN
    const = lambda i, f: (0, 0)
    return pl.pallas_call(
        _dense_ffn_kernel,
        grid=(t // tm, d_ff // tf),
        in_specs=[
            pl.BlockSpec((tm, d), lambda i, f: (i, 0)),
            pl.BlockSpec((d, tf), lambda i, f: (0, f)),
            pl.BlockSpec((d, tf), lambda i, f: (0, f)),
            pl.BlockSpec((tf, d), lambda i, f: (f, 0)),
            pl.BlockSpec((1, d), const),
            pl.BlockSpec((1, d), const),
        ],
        out_specs=pl.BlockSpec((tm, d), lambda i, f: (i, 0)),
        out_shape=jax.ShapeDtypeStruct((t, d), F32),
        scratch_shapes=[pltpu.VMEM((tm, d), BF16), pltpu.VMEM((tm, d), F32)],
        compiler_params=_params("arbitrary", "arbitrary"),
        name="dense_ffn",
    )(x, w_gate, w_up, w_down, g, b)


def _expert_ffn_kernel(te_ref, nv_ref, x_ref, wg_ref, wu_ref, wd_ref, o_ref, xb_ref, acc_ref):
    del te_ref
    i = pl.program_id(0)
    f = pl.program_id(1)
    last = f == pl.num_programs(1) - 1
    valid = i < nv_ref[0]

    @pl.when(valid)
    def _():
        @pl.when(f == 0)
        def _():
            xb_ref[...] = x_ref[...].astype(BF16)

        part = _swiglu_partial(xb_ref[...], wg_ref, wu_ref, wd_ref)

        @pl.when(f == 0)
        def _():
            acc_ref[...] = part

        @pl.when(f > 0)
        def _():
            acc_ref[...] += part

        @pl.when(last)
        def _():
            o_ref[...] = acc_ref[...]

    @pl.when(jnp.logical_and(last, jnp.logical_not(valid)))
    def _():
        o_ref[...] = jnp.zeros(o_ref.shape, F32)


def _expert_ffn(xs, tile_expert, n_valid, w_gate, w_up, w_down, tf):
    n_rows = xs.shape[0]
    d = w_gate.shape[1]
    d_exp = w_gate.shape[2]
    tm = TM_EXP
    nf = d_exp // tf

    def w_col(i, f, te, nv):
        return (te[i], 0, jnp.where(i < nv[0], f, nf - 1))

    def w_row(i, f, te, nv):
        return (te[i], jnp.where(i < nv[0], f, nf - 1), 0)

    return pl.pallas_call(
        _expert_ffn_kernel,
        grid_spec=pltpu.PrefetchScalarGridSpec(
            num_scalar_prefetch=2,
            grid=(n_rows // tm, nf),
            in_specs=[
                pl.BlockSpec((tm, d), lambda i, f, te, nv: (i, 0)),
                pl.BlockSpec((None, d, tf), w_col),
                pl.BlockSpec((None, d, tf), w_col),
                pl.BlockSpec((None, tf, d), w_row),
            ],
            out_specs=pl.BlockSpec((tm, d), lambda i, f, te, nv: (i, 0)),
            scratch_shapes=[pltpu.VMEM((tm, d), BF16), pltpu.VMEM((tm, d), F32)],
        ),
        out_shape=jax.ShapeDtypeStruct(xs.shape, F32),
        compiler_params=_params("arbitrary", "arbitrary"),
        name="expert_ffn",
    )(tile_expert, n_valid, xs, w_gate, w_up, w_down)


def _split_bf16(v):
    hi = v.astype(BF16)
    lo = (v - hi.astype(F32)).astype(BF16)
    return hi, lo


def _router_kernel(x_ref, wr_ref, idx_ref, gate_ref):
    xh, xl = _split_bf16(x_ref[...])
    wh, wl = _split_bf16(wr_ref[...])
    logits = (lax.dot_general(wh, xh, _NT, preferred_element_type=F32)
              + lax.dot_general(wh, xl, _NT, preferred_element_type=F32)
              + lax.dot_general(wl, xh, _NT, preferred_element_type=F32))
    e = lax.broadcasted_iota(I32, logits.shape, 0)
    m1 = jnp.max(logits, axis=0, keepdims=True)
    i1 = jnp.min(jnp.where(logits == m1, e, N_EXPERTS), axis=0, keepdims=True)
    rest = jnp.where(e == i1, -jnp.inf, logits)
    m2 = jnp.max(rest, axis=0, keepdims=True)
    i2 = jnp.min(jnp.where(rest == m2, e, N_EXPERTS), axis=0, keepdims=True)
    r = jnp.exp(m2 - m1)
    idx_ref[...] = jnp.concatenate([i1, i2], axis=0)
    gate_ref[...] = jnp.concatenate([1.0 / (1.0 + r), r / (1.0 + r)], axis=0)


def _router(x, w_router_t):
    t, d = x.shape
    tm = TM_MIX
    return pl.pallas_call(
        _router_kernel,
        grid=(t // tm,),
        in_specs=[
            pl.BlockSpec((tm, d), lambda i: (i, 0)),
            pl.BlockSpec((N_EXPERTS, d), lambda i: (0, 0)),
        ],
        out_specs=[
            pl.BlockSpec((TOP_K, tm), lambda i: (0, i)),
            pl.BlockSpec((TOP_K, tm), lambda i: (0, i)),
        ],
        out_shape=[jax.ShapeDtypeStruct((TOP_K, t), I32), jax.ShapeDtypeStruct((TOP_K, t), F32)],
        compiler_params=_params("arbitrary"),
        name="router",
    )(x, w_router_t)


def _dispatch_kernel(pos_ref, x_ref, init_ref, xs_ref, sem):
    del init_ref
    tm = x_ref.shape[0]

    def issue(j, carry):
        for k in range(TOP_K):
            pltpu.make_async_copy(x_ref.at[pl.ds(j, 1)], xs_ref.at[pl.ds(pos_ref[0, 0, TOP_K * j + k], 1)], sem).start()
        return carry

    lax.fori_loop(0, tm, issue, 0, unroll=4)
    for k in range(TOP_K):
        pltpu.make_async_copy(x_ref, xs_ref.at[pl.ds(0, tm)], sem).wait()


def _dispatch(x, pos, n_rows):
    t, d = x.shape
    tm = TM_MIX
    any_spec = pl.BlockSpec(memory_space=pl.ANY)
    return pl.pallas_call(
        _dispatch_kernel,
        grid=(t // tm,),
        in_specs=[
            pl.BlockSpec((1, 1, TOP_K * tm), lambda i: (i, 0, 0), memory_space=pltpu.SMEM),
            pl.BlockSpec((tm, d), lambda i: (i, 0)),
            any_spec,
        ],
        out_specs=any_spec,
        out_shape=jax.ShapeDtypeStruct((n_rows, d), F32),
        scratch_shapes=[pltpu.SemaphoreType.DMA(())],
        input_output_aliases={2: 0},
        compiler_params=_params("arbitrary"),
        name="moe_dispatch",
    )(pos.reshape(t // tm, 1, TOP_K * tm), x, jnp.zeros((n_rows, d), F32))


def _moe_combine_kernel(pos_ref, pos_next_ref, x_ref, gate_ref, g_ref, b_ref, ys_ref, o_ref, ybuf, sem):
    tm, d = x_ref.shape
    i = pl.program_id(0)
    slot = i % 2

    def gather(idx_ref, s):
        def issue(j, carry):
            for k in range(TOP_K):
                pltpu.make_async_copy(ys_ref.at[pl.ds(idx_ref[0, 0, TOP_K * j + k], 1)], ybuf.at[s, k, pl.ds(j, 1)],
                                      sem.at[s]).start()
            return carry

        lax.fori_loop(0, tm, issue, 0, unroll=4)

    @pl.when(i == 0)
    def _():
        gather(pos_ref, 0)

    @pl.when(i + 1 < pl.num_programs(0))
    def _():
        gather(pos_next_ref, 1 - slot)

    for k in range(TOP_K):
        pltpu.make_async_copy(ys_ref.at[pl.ds(0, tm)], ybuf.at[slot, k], sem.at[slot]).wait()
    gates = gate_ref[...]
    ff = jnp.zeros((tm, d), F32)
    for k in range(TOP_K):
        ff = ff + gates[:, k:k + 1] * ybuf[slot, k]
    o_ref[...] = _layer_norm(ALPHA * x_ref[...] + ff, g_ref[...], b_ref[...])


def _moe_combine(x, ys, pos, gates, g, b):
    t, d = x.shape
    tm = TM_MIX
    steps = t // tm
    const = lambda i: (0, 0)
    pos = pos.reshape(steps, 1, TOP_K * tm)
    return pl.pallas_call(
        _moe_combine_kernel,
        grid=(steps,),
        in_specs=[
            pl.BlockSpec((1, 1, TOP_K * tm), lambda i: (i, 0, 0), memory_space=pltpu.SMEM),
            pl.BlockSpec((1, 1, TOP_K * tm), lambda i: (jnp.minimum(i + 1, steps - 1), 0, 0), memory_space=pltpu.SMEM),
            pl.BlockSpec((tm, d), lambda i: (i, 0)),
            pl.BlockSpec((tm, TOP_K), lambda i: (i, 0)),
            pl.BlockSpec((1, d), const),
            pl.BlockSpec((1, d), const),
            pl.BlockSpec(memory_space=pl.ANY),
        ],
        out_specs=pl.BlockSpec((tm, d), lambda i: (i, 0)),
        out_shape=jax.ShapeDtypeStruct((t, d), F32),
        scratch_shapes=[pltpu.VMEM((2, TOP_K, tm, d), F32), pltpu.SemaphoreType.DMA((2,))],
        compiler_params=_params("arbitrary"),
        name="moe_combine",
    )(pos, pos, x, gates, g, b, ys)


def _moe_layer(x, w_router, w_gate, w_up, w_down, g, b):
    t, d = x.shape
    n_assign = TOP_K * t
    idx, gates = _router(x, w_router.T)

    e_flat = idx.T.reshape(n_assign)
    onehot = (e_flat[:, None] == jnp.arange(N_EXPERTS, dtype=I32)[None, :]).astype(I32)
    csum = jnp.cumsum(onehot, axis=0)
    rank = jnp.sum(csum * onehot, axis=1) - 1
    counts = csum[-1]
    padded = ((counts + TM_EXP - 1) // TM_EXP) * TM_EXP
    ends = jnp.cumsum(padded)
    pos = (jnp.sum((ends - padded)[None, :] * onehot, axis=1) + rank).astype(I32)
    n_tiles = n_assign // TM_EXP + N_EXPERTS
    tile_ids = jnp.arange(n_tiles, dtype=I32)
    tile_expert = jnp.minimum(
        jnp.sum((tile_ids[:, None] >= (ends // TM_EXP)[None, :]).astype(I32), axis=1), N_EXPERTS - 1).astype(I32)
    n_valid = (ends[-1:] // TM_EXP).astype(I32)

    xs = _dispatch(x, pos, n_tiles * TM_EXP)
    ys = _expert_ffn(xs, tile_expert, n_valid, w_gate, w_up, w_down, tf=512)
    return _moe_combine(x, ys, pos, gates.T, g, b)


def _proj_kernel(x_ref, w_ref, o_ref, *, scale):
    y = jnp.dot(x_ref[...].astype(BF16), w_ref[...], preferred_element_type=F32)
    o_ref[...] = (y if scale == 1.0 else y * scale).astype(o_ref.dtype)


def _proj(x, w, scale=1.0):
    t, d = x.shape
    n = w.shape[1]
    tm = TM_MIX
    return pl.pallas_call(
        functools.partial(_proj_kernel, scale=scale),
        grid=(t // tm,),
        in_specs=[pl.BlockSpec((tm, d), lambda i: (i, 0)), pl.BlockSpec((d, n), lambda i: (0, 0))],
        out_specs=pl.BlockSpec((tm, n), lambda i: (i, 0)),
        out_shape=jax.ShapeDtypeStruct((t, n), BF16),
        compiler_params=_params("arbitrary"),
        name="proj",
    )(x, w)


QBLKS_PER_STEP = 2


def _attn_kernel(q_ref, kp_ref, kc_ref, vp_ref, vc_ref, o_ref, m_ref, l_ref, kcat_ref, vcat_ref):
    n_pairs = q_ref.shape[1] // LANES
    qi = lax.broadcasted_iota(I32, (QBLK, 2 * QBLK), 0)
    ki = lax.broadcasted_iota(I32, (QBLK, 2 * QBLK), 1)
    in_band = ki - QBLK <= qi
    lane = lax.broadcasted_iota(I32, (QBLK, LANES), 1)
    low_half = lane < HEAD_DIM
    kcat_ref[0:QBLK, :] = kp_ref[...]
    kcat_ref[QBLK:, :] = kc_ref[...]
    for pr in range(n_pairs):
        vcat_ref[0:QBLK, 2 * pr * LANES:(2 * pr + 1) * LANES] = vp_ref[:, pr * LANES:(pr + 1) * LANES]
        vcat_ref[QBLK:, 2 * pr * LANES:(2 * pr + 1) * LANES] = vc_ref[:, pr * LANES:(pr + 1) * LANES]
        vcat_ref[:, (2 * pr + 1) * LANES:(2 * pr + 2) * LANES] = jnp.ones((vcat_ref.shape[0], LANES), BF16)
    m_ref[...] = jnp.zeros(m_ref.shape, F32)
    l_ref[...] = jnp.ones(l_ref.shape, F32)
    for j in range(QBLKS_PER_STEP):
        rows = slice(j * QBLK, (j + 1) * QBLK)
        keys = slice(j * QBLK, (j + 2) * QBLK)
        first_ok = qi + jnp.where(pl.program_id(2) > 0, 0, QBLK) if j == 0 else qi
        mask = jnp.logical_and(jnp.logical_or(ki >= first_ok, ki >= QBLK), in_band)
        for pr in range(n_pairs):
            cols = slice(pr * LANES, (pr + 1) * LANES)
            q2 = q_ref[rows, cols]
            k2 = kcat_ref[keys, cols]
            v_ext = vcat_ref[keys, 2 * pr * LANES:(2 * pr + 2) * LANES]
            o_pair = None
            for a in range(2):
                own = low_half if a == 0 else jnp.logical_not(low_half)
                qm = jnp.where(own, q2, jnp.zeros_like(q2))
                s = lax.dot_general(qm, k2, _NT, preferred_element_type=F32)
                s = jnp.where(mask, s, NEG_BIG)
                m = jnp.max(s, axis=-1, keepdims=True)
                p = jnp.exp(s - m).astype(BF16)
                oe = jnp.dot(p, v_ext, preferred_element_type=F32)
                o_pair = oe[:, :LANES] if a == 0 else jnp.where(low_half, o_pair, oe[:, :LANES])
                h = 2 * pr + a
                m_ref[rows, h:h + 1] = m
                l_ref[rows, h:h + 1] = oe[:, LANES + h:LANES + h + 1]
            o_ref[rows, cols] = o_pair


def _group_attention(q, k, v):
    bsz, dil, length, width = q.shape
    step = QBLKS_PER_STEP * QBLK
    blk = (None, None, step, width)
    cur = lambda b, r, n: (b, r, n, 0)
    prev_blk = (None, None, QBLK, width)
    prev = lambda b, r, n: (b, r, jnp.maximum(QBLKS_PER_STEP * n - 1, 0), 0)
    stat_spec = pl.BlockSpec((None, None, step, LANES), cur)
    stat_shape = jax.ShapeDtypeStruct((bsz, dil, length, LANES), F32)
    return pl.pallas_call(
        _attn_kernel,
        grid=(bsz, dil, length // step),
        in_specs=[
            pl.BlockSpec(blk, cur),
            pl.BlockSpec(prev_blk, prev),
            pl.BlockSpec(blk, cur),
            pl.BlockSpec(prev_blk, prev),
            pl.BlockSpec(blk, cur),
        ],
        out_specs=[pl.BlockSpec(blk, cur), stat_spec, stat_spec],
        out_shape=[jax.ShapeDtypeStruct((bsz, dil, length, width), F32), stat_shape, stat_shape],
        scratch_shapes=[pltpu.VMEM((step + QBLK, width), BF16), pltpu.VMEM((step + QBLK, 2 * width), BF16)],
        compiler_params=_params("arbitrary", "arbitrary", "arbitrary"),
        name="group_attention",
    )(q, k, k, v, v)


def _attn_out_kernel(x_ref, o0_ref, o1_ref, o2_ref, m0_ref, m1_ref, m2_ref, l0_ref, l1_ref, l2_ref,
                     wo_ref, g_ref, b_ref, out_ref):
    width = o0_ref.shape[1]
    ms = [m0_ref[...], m1_ref[...], m2_ref[...]]
    top = jnp.maximum(jnp.maximum(ms[0], ms[1]), ms[2])
    es = [jnp.exp(m - top) for m in ms]
    den = es[0] * l0_ref[...] + es[1] * l1_ref[...] + es[2] * l2_ref[...]
    head = lax.broadcasted_iota(I32, (LANES, width), 0)
    lane = lax.broadcasted_iota(I32, (LANES, width), 1)
    spread = (lane // HEAD_DIM == head).astype(BF16)
    mixed = jnp.zeros(o0_ref.shape, F32)
    for e, o_ref in zip(es, (o0_ref, o1_ref, o2_ref)):
        hi, lo = _split_bf16(e / den)
        wide = (jnp.dot(hi, spread, preferred_element_type=F32)
                + jnp.dot(lo, spread, preferred_element_type=F32))
        mixed = mixed + wide * o_ref[...]
    y = jnp.dot(mixed.astype(BF16), wo_ref[...], preferred_element_type=F32)
    out_ref[...] = _layer_norm(ALPHA * x_ref[...] + y, g_ref[...], b_ref[...])


def _attn_out(x, outs, stats, w_o, g, b):
    t, d = x.shape
    width = w_o.shape[0]
    tm = TM_MIX
    const = lambda i: (0, 0)
    row = lambda i: (i, 0)
    return pl.pallas_call(
        _attn_out_kernel,
        grid=(t // tm,),
        in_specs=([pl.BlockSpec((tm, d), row)]
                  + [pl.BlockSpec((tm, width), row)] * N_GROUPS
                  + [pl.BlockSpec((tm, LANES), row)] * (2 * N_GROUPS)
                  + [pl.BlockSpec((width, d), const), pl.BlockSpec((1, d), const), pl.BlockSpec((1, d), const)]),
        out_specs=pl.BlockSpec((tm, d), row),
        out_shape=jax.ShapeDtypeStruct((t, d), F32),
        compiler_params=_params("arbitrary"),
        name="attn_out",
    )(x, *outs, *stats, w_o, g, b)


def _to_residue(a, dil):
    bsz, seq, width = a.shape
    return a.reshape(bsz, seq // dil, dil, width).transpose(0, 2, 1, 3)


def _from_residue(a):
    bsz, dil, length, width = a.shape
    return a.transpose(0, 2, 1, 3).reshape(bsz * length * dil, width)


def _attention_layer(x, bsz, w_q, w_o, kv_groups, g, b):
    t, d = x.shape
    width = w_o.shape[0]
    q = _proj(x, w_q, scale=HEAD_DIM ** -0.5).reshape(bsz, t // bsz, N_GROUPS * width)
    outs, maxes, dens = [], [], []
    for gi, (_, dil) in enumerate(ATT_GROUPS):
        k_res, v_res = kv_groups[gi]
        o, m, l = _group_attention(_to_residue(q[:, :, gi * width:(gi + 1) * width], dil), k_res, v_res)
        outs.append(_from_residue(o))
        maxes.append(_from_residue(m))
        dens.append(_from_residue(l))
    return _attn_out(x, outs, maxes + dens, w_o, g, b)


def kernel(x, a_w_in, a_conv, a_w_out, kv_w, b_w_q, b_w_o, ffn_w_gate, ffn_w_up, ffn_w_down,
           moe_w_router, moe_w_gate, moe_w_up, moe_w_down, ln_g, ln_b):
    bsz, seq, d = x.shape
    width = b_w_o.shape[1]
    h = x.reshape(bsz * seq, d)
    ln_g = ln_g.reshape(DEPTH, 2, 1, d)
    ln_b = ln_b.reshape(DEPTH, 2, 1, d)
    kv_groups = None
    for l in range(DEPTH):
        if l < N_A_LAYERS:
            h = _a_mixer(h, a_w_in[l].astype(BF16), a_conv[l], a_w_out[l].astype(BF16),
                         ln_g[l, 0], ln_b[l, 0], seq)
        else:
            if kv_groups is None:
                kv = _proj(h, kv_w.astype(BF16)).reshape(bsz, seq, N_GROUPS * 2 * width)
                kv_groups = [tuple(_to_residue(kv[:, :, (2 * gi + j) * width:(2 * gi + j + 1) * width], dil)
                                   for j in range(2))
                             for gi, (_, dil) in enumerate(ATT_GROUPS)]
            j = l - N_A_LAYERS
            h = _attention_layer(h, bsz, b_w_q[j].astype(BF16), b_w_o[j].astype(BF16), kv_groups,
                                 ln_g[l, 0], ln_b[l, 0])
        i = l // 2
        if l % 2 == 0:
            h = _dense_ffn(h, ffn_w_gate[i].astype(BF16), ffn_w_up[i].astype(BF16),
                           ffn_w_down[i].astype(BF16), ln_g[l, 1], ln_b[l, 1], tf=1408)
        else:
            h = _moe_layer(h, moe_w_router[i], moe_w_gate[i].astype(BF16), moe_w_up[i].astype(BF16),
                           moe_w_down[i].astype(BF16), ln_g[l, 1], ln_b[l, 1])
    return h.reshape(bsz, seq, d)
```

```python
import functools

import jax
import jax.numpy as jnp
from jax import lax
from jax.experimental import pallas as pl
from jax.experimental.pallas import tpu as pltpu

F32 = jnp.float32
BF16 = jnp.bfloat16
I32 = jnp.int32

DEPTH = 4
N_A_LAYERS = DEPTH // 2
CONV_WIDTH = 3
ATT_GROUPS = ((128, 1), (512, 4), (2048, 16))
N_GROUPS = len(ATT_GROUPS)
HEAD_DIM = 64
N_EXPERTS = 8
TOP_K = 2
ALPHA = (2.0 * DEPTH) ** 0.25
LN_EPS = 1e-5

QBLK = 128
assert all(w // d == QBLK for w, d in ATT_GROUPS)

LANES = 128
SUBLANES = 8
VMEM_LIMIT = 56 * 1024 * 1024
NEG_BIG = -1e30

TM_MIX = 512
TM_FFN = 1024
TM_EXP = 1024
TM_RES = 512

_NT = (((1,), (1,)), ((), ()))


def _layer_norm(z, g, b):
    mu = jnp.mean(z, axis=-1, keepdims=True)
    zc = z - mu
    var = jnp.mean(zc * zc, axis=-1, keepdims=True)
    return zc * lax.rsqrt(var + LN_EPS) * g + b


def _params(*semantics):
    return pltpu.CompilerParams(dimension_semantics=semantics, vmem_limit_bytes=VMEM_LIMIT)


def _a_mixer_kernel(x_ref, win_ref, conv_ref, wout_ref, g_ref, b_ref, o_ref, ubuf, *, tiles_per_seq):
    tm, d = x_ref.shape

    @pl.when(pl.program_id(0) % tiles_per_seq == 0)
    def _():
        ubuf[0:SUBLANES, :] = jnp.zeros((SUBLANES, d), F32)

    x = x_ref[...]
    p = jnp.dot(x.astype(BF16), win_ref[...], preferred_element_type=F32)
    ubuf[SUBLANES:SUBLANES + tm, :] = p[:, d:2 * d] * p[:, 2 * d:]
    cw = conv_ref[...]
    conv = (cw[2:3, :] * ubuf[SUBLANES:SUBLANES + tm, :]
            + cw[1:2, :] * ubuf[SUBLANES - 1:SUBLANES - 1 + tm, :]
            + cw[0:1, :] * ubuf[SUBLANES - 2:SUBLANES - 2 + tm, :])
    y = jnp.dot((p[:, :d] * conv).astype(BF16), wout_ref[...], preferred_element_type=F32)
    o_ref[...] = _layer_norm(ALPHA * x + y, g_ref[...], b_ref[...])
    ubuf[0:SUBLANES, :] = ubuf[tm:tm + SUBLANES, :]


def _a_mixer(x, w_in, conv_w, w_out, g, b, seq_len):
    t, d = x.shape
    tm = TM_MIX
    const = lambda i: (0, 0)
    return pl.pallas_call(
        functools.partial(_a_mixer_kernel, tiles_per_seq=seq_len // tm),
        grid=(t // tm,),
        in_specs=[
            pl.BlockSpec((tm, d), lambda i: (i, 0)),
            pl.BlockSpec((d, 3 * d), const),
            pl.BlockSpec((CONV_WIDTH, d), const),
            pl.BlockSpec((d, d), const),
            pl.BlockSpec((1, d), const),
            pl.BlockSpec((1, d), const),
        ],
        out_specs=pl.BlockSpec((tm, d), lambda i: (i, 0)),
        out_shape=jax.ShapeDtypeStruct((t, d), F32),
        scratch_shapes=[pltpu.VMEM((tm + SUBLANES, d), F32)],
        compiler_params=_params("arbitrary"),
        name="a_mixer",
    )(x, w_in, conv_w, w_out, g, b)


def _swiglu_partial(xb, wg_ref, wu_ref, wd_ref):
    gate = jnp.dot(xb, wg_ref[...], preferred_element_type=F32)
    up = jnp.dot(xb, wu_ref[...], preferred_element_type=F32)
    h = (gate * jax.nn.sigmoid(gate) * up).astype(BF16)
    return jnp.dot(h, wd_ref[...], preferred_element_type=F32)


def _dense_ffn_kernel(x_ref, wg_ref, wu_ref, wd_ref, g_ref, b_ref, o_ref, xb_ref, acc_ref):
    f = pl.program_id(1)

    @pl.when(f == 0)
    def _():
        xb_ref[...] = x_ref[...].astype(BF16)

    part = _swiglu_partial(xb_ref[...], wg_ref, wu_ref, wd_ref)

    @pl.when(f == 0)
    def _():
        acc_ref[...] = part

    @pl.when(f > 0)
    def _():
        acc_ref[...] += part

    @pl.when(f == pl.num_programs(1) - 1)
    def _():
        o_ref[...] = _layer_norm(ALPHA * x_ref[...] + acc_ref[...], g_ref[...], b_ref[...])


def _dense_ffn(x, w_gate, w_up, w_down, g, b, tf):
    t, d = x.shape
    d_ff = w_gate.shape[1]
    tm = TM_FFN
    const = lambda i, f: (0, 0)
    return pl.pallas_call(
        _dense_ffn_kernel,
        grid=(t // tm, d_ff // tf),
        in_specs=[
            pl.BlockSpec((tm, d), lambda i, f: (i, 0)),
            pl.BlockSpec((d, tf), lambda i, f: (0, f)),
            pl.BlockSpec((d, tf), lambda i, f: (0, f)),
            pl.BlockSpec((tf, d), lambda i, f: (f, 0)),
            pl.BlockSpec((1, d), const),
            pl.BlockSpec((1, d), const),
        ],
        out_specs=pl.BlockSpec((tm, d), lambda i, f: (i, 0)),
        out_shape=jax.ShapeDtypeStruct((t, d), F32),
        scratch_shapes=[pltpu.VMEM((tm, d), BF16), pltpu.VMEM((tm, d), F32)],
        compiler_params=_params("arbitrary", "arbitrary"),
        name="dense_ffn",
    )(x, w_gate, w_up, w_down, g, b)


def _expert_ffn_kernel(te_ref, nv_ref, x_ref, wg_ref, wu_ref, wd_ref, o_ref, xb_ref, acc_ref):
    del te_ref
    i = pl.program_id(0)
    f = pl.program_id(1)
    last = f == pl.num_programs(1) - 1
    valid = i < nv_ref[0]

    @pl.when(valid)
    def _():
        @pl.when(f == 0)
        def _():
            xb_ref[...] = x_ref[...].astype(BF16)

        part = _swiglu_partial(xb_ref[...], wg_ref, wu_ref, wd_ref)

        @pl.when(f == 0)
        def _():
            acc_ref[...] = part

        @pl.when(f > 0)
        def _():
            acc_ref[...] += part

        @pl.when(last)
        def _():
            o_ref[...] = acc_ref[...]

    @pl.when(jnp.logical_and(last, jnp.logical_not(valid)))
    def _():
        o_ref[...] = jnp.zeros(o_ref.shape, F32)


def _expert_ffn(xs, tile_expert, n_valid, w_gate, w_up, w_down, layer, tf):
    n_rows = xs.shape[0]
    d = w_gate.shape[2]
    d_exp = w_gate.shape[3]
    tm = TM_EXP
    nf = d_exp // tf

    def w_col(i, f, te, nv):
        return (layer, te[i], 0, jnp.where(i < nv[0], f, nf - 1))

    def w_row(i, f, te, nv):
        return (layer, te[i], jnp.where(i < nv[0], f, nf - 1), 0)

    return pl.pallas_call(
        _expert_ffn_kernel,
        grid_spec=pltpu.PrefetchScalarGridSpec(
            num_scalar_prefetch=2,
            grid=(n_rows // tm, nf),
            in_specs=[
                pl.BlockSpec((tm, d), lambda i, f, te, nv: (i, 0)),
                pl.BlockSpec((None, None, d, tf), w_col),
                pl.BlockSpec((None, None, d, tf), w_col),
                pl.BlockSpec((None, None, tf, d), w_row),
            ],
            out_specs=pl.BlockSpec((tm, d), lambda i, f, te, nv: (i, 0)),
            scratch_shapes=[pltpu.VMEM((tm, d), BF16), pltpu.VMEM((tm, d), F32)],
        ),
        out_shape=jax.ShapeDtypeStruct(xs.shape, F32),
        compiler_params=_params("arbitrary", "arbitrary"),
        name="expert_ffn",
    )(tile_expert, n_valid, xs, w_gate, w_up, w_down)


def _split_bf16(v):
    hi = v.astype(BF16)
    lo = (v - hi.astype(F32)).astype(BF16)
    return hi, lo


def _router_kernel(x_ref, wr_ref, idx_ref, gate_ref):
    xh, xl = _split_bf16(x_ref[...])
    wh, wl = _split_bf16(wr_ref[...])
    logits = (lax.dot_general(wh, xh, _NT, preferred_element_type=F32)
              + lax.dot_general(wh, xl, _NT, preferred_element_type=F32)
              + lax.dot_general(wl, xh, _NT, preferred_element_type=F32))
    e = lax.broadcasted_iota(I32, logits.shape, 0)
    m1 = jnp.max(logits, axis=0, keepdims=True)
    i1 = jnp.min(jnp.where(logits == m1, e, N_EXPERTS), axis=0, keepdims=True)
    rest = jnp.where(e == i1, -jnp.inf, logits)
    m2 = jnp.max(rest, axis=0, keepdims=True)
    i2 = jnp.min(jnp.where(rest == m2, e, N_EXPERTS), axis=0, keepdims=True)
    r = jnp.exp(m2 - m1)
    idx_ref[...] = jnp.concatenate([i1, i2], axis=0)
    gate_ref[...] = jnp.concatenate([1.0 / (1.0 + r), r / (1.0 + r)], axis=0)


def _router(x, w_router_t):
    t, d = x.shape
    tm = TM_MIX
    return pl.pallas_call(
        _router_kernel,
        grid=(t // tm,),
        in_specs=[
            pl.BlockSpec((tm, d), lambda i: (i, 0)),
            pl.BlockSpec((N_EXPERTS, d), lambda i: (0, 0)),
        ],
        out_specs=[
            pl.BlockSpec((TOP_K, tm), lambda i: (0, i)),
            pl.BlockSpec((TOP_K, tm), lambda i: (0, i)),
        ],
        out_shape=[jax.ShapeDtypeStruct((TOP_K, t), I32), jax.ShapeDtypeStruct((TOP_K, t), F32)],
        compiler_params=_params("arbitrary"),
        name="router",
    )(x, w_router_t)


def _dispatch_kernel(ends_ref, nv_ref, pos_ref, x_ref, xs_ref, zero_ref, sem, zero_sem):
    tm = x_ref.shape[0]

    @pl.when(pl.program_id(0) == 0)
    def _():
        zr = zero_ref.shape[0]
        zero_ref[...] = jnp.zeros(zero_ref.shape, F32)

        def zero_tile(tile):
            copies = [pltpu.make_async_copy(zero_ref, xs_ref.at[pl.ds(tile * TM_EXP + q * zr, zr)], zero_sem)
                      for q in range(TM_EXP // zr)]
            for c in copies:
                c.start()
            for c in copies:
                c.wait()

        for e in range(N_EXPERTS):
            zero_tile(jnp.maximum(ends_ref[e] // TM_EXP - 1, 0))
        for u in range(N_EXPERTS):
            @pl.when(nv_ref[0] + u < xs_ref.shape[0] // TM_EXP)
            def _():
                zero_tile(nv_ref[0] + u)

    def issue(j, carry):
        for k in range(TOP_K):
            pltpu.make_async_copy(x_ref.at[pl.ds(j, 1)], xs_ref.at[pl.ds(pos_ref[0, 0, TOP_K * j + k], 1)], sem).start()
        return carry

    lax.fori_loop(0, tm, issue, 0, unroll=4)
    for k in range(TOP_K):
        pltpu.make_async_copy(x_ref, xs_ref.at[pl.ds(0, tm)], sem).wait()


def _dispatch(x, pos, ends, n_valid, n_rows):
    t, d = x.shape
    tm = TM_MIX
    return pl.pallas_call(
        _dispatch_kernel,
        grid_spec=pltpu.PrefetchScalarGridSpec(
            num_scalar_prefetch=2,
            grid=(t // tm,),
            in_specs=[
                pl.BlockSpec((1, 1, TOP_K * tm), lambda i, ends, nv: (i, 0, 0), memory_space=pltpu.SMEM),
                pl.BlockSpec((tm, d), lambda i, ends, nv: (i, 0)),
            ],
            out_specs=pl.BlockSpec(memory_space=pl.ANY),
            scratch_shapes=[pltpu.VMEM((tm, d), F32), pltpu.SemaphoreType.DMA(()), pltpu.SemaphoreType.DMA(())],
        ),
        out_shape=jax.ShapeDtypeStruct((n_rows, d), F32),
        compiler_params=_params("arbitrary"),
        name="moe_dispatch",
    )(ends, n_valid, pos.reshape(t // tm, 1, TOP_K * tm), x)


def _moe_combine_kernel(pos_ref, pos_next_ref, x_ref, gate_ref, g_ref, b_ref, ys_ref, o_ref, ybuf, sem):
    tm, d = x_ref.shape
    i = pl.program_id(0)
    slot = i % 2

    def gather(idx_ref, s):
        def issue(j, carry):
            for k in range(TOP_K):
                pltpu.make_async_copy(ys_ref.at[pl.ds(idx_ref[0, 0, TOP_K * j + k], 1)], ybuf.at[s, k, pl.ds(j, 1)],
                                      sem.at[s]).start()
            return carry

        lax.fori_loop(0, tm, issue, 0, unroll=4)

    @pl.when(i == 0)
    def _():
        gather(pos_ref, 0)

    @pl.when(i + 1 < pl.num_programs(0))
    def _():
        gather(pos_next_ref, 1 - slot)

    for k in range(TOP_K):
        pltpu.make_async_copy(ys_ref.at[pl.ds(0, tm)], ybuf.at[slot, k], sem.at[slot]).wait()
    gates = gate_ref[...]
    ff = jnp.zeros((tm, d), F32)
    for k in range(TOP_K):
        ff = ff + gates[:, k:k + 1] * ybuf[slot, k]
    o_ref[...] = _layer_norm(ALPHA * x_ref[...] + ff, g_ref[...], b_ref[...])


def _moe_combine(x, ys, pos, gates, g, b):
    t, d = x.shape
    tm = TM_MIX
    steps = t // tm
    const = lambda i: (0, 0)
    pos = pos.reshape(steps, 1, TOP_K * tm)
    return pl.pallas_call(
        _moe_combine_kernel,
        grid=(steps,),
        in_specs=[
            pl.BlockSpec((1, 1, TOP_K * tm), lambda i: (i, 0, 0), memory_space=pltpu.SMEM),
            pl.BlockSpec((1, 1, TOP_K * tm), lambda i: (jnp.minimum(i + 1, steps - 1), 0, 0), memory_space=pltpu.SMEM),
            pl.BlockSpec((tm, d), lambda i: (i, 0)),
            pl.BlockSpec((tm, TOP_K), lambda i: (i, 0)),
            pl.BlockSpec((1, d), const),
            pl.BlockSpec((1, d), const),
            pl.BlockSpec(memory_space=pl.ANY),
        ],
        out_specs=pl.BlockSpec((tm, d), lambda i: (i, 0)),
        out_shape=jax.ShapeDtypeStruct((t, d), F32),
        scratch_shapes=[pltpu.VMEM((2, TOP_K, tm, d), F32), pltpu.SemaphoreType.DMA((2,))],
        compiler_params=_params("arbitrary"),
        name="moe_combine",
    )(pos, pos, x, gates, g, b, ys)


def _moe_layer(x, w_router, w_gate, w_up, w_down, layer, g, b):
    t, d = x.shape
    n_assign = TOP_K * t
    idx, gates = _router(x, w_router.T)

    e_flat = idx.T.reshape(n_assign)
    onehot = (e_flat[:, None] == jnp.arange(N_EXPERTS, dtype=I32)[None, :]).astype(I32)
    csum = jnp.cumsum(onehot, axis=0)
    rank = jnp.sum(csum * onehot, axis=1) - 1
    counts = csum[-1]
    padded = ((counts + TM_EXP - 1) // TM_EXP) * TM_EXP
    ends = jnp.cumsum(padded)
    pos = (jnp.sum((ends - padded)[None, :] * onehot, axis=1) + rank).astype(I32)
    n_tiles = n_assign // TM_EXP + N_EXPERTS
    tile_ids = jnp.arange(n_tiles, dtype=I32)
    tile_expert = jnp.minimum(
        jnp.sum((tile_ids[:, None] >= (ends // TM_EXP)[None, :]).astype(I32), axis=1), N_EXPERTS - 1).astype(I32)
    n_valid = (ends[-1:] // TM_EXP).astype(I32)

    xs = _dispatch(x, pos, ends.astype(I32), n_valid, n_tiles * TM_EXP)
    ys = _expert_ffn(xs, tile_expert, n_valid, w_gate, w_up, w_down, layer, tf=512)
    return _moe_combine(x, ys, pos, gates.T, g, b)


QBLKS_PER_STEP = 2


def _attn_kernel(q_ref, kp_ref, kc_ref, vp_ref, vc_ref, o_ref, m_ref, l_ref, kcat_ref, vcat_ref):
    n_pairs = q_ref.shape[1] // LANES
    qi = lax.broadcasted_iota(I32, (QBLK, 2 * QBLK), 0)
    ki = lax.broadcasted_iota(I32, (QBLK, 2 * QBLK), 1)
    in_band = ki - QBLK <= qi
    lane = lax.broadcasted_iota(I32, (QBLK, LANES), 1)
    low_half = lane < HEAD_DIM
    kcat_ref[0:QBLK, :] = kp_ref[...]
    kcat_ref[QBLK:, :] = kc_ref[...]
    for pr in range(n_pairs):
        vcat_ref[0:QBLK, 2 * pr * LANES:(2 * pr + 1) * LANES] = vp_ref[:, pr * LANES:(pr + 1) * LANES]
        vcat_ref[QBLK:, 2 * pr * LANES:(2 * pr + 1) * LANES] = vc_ref[:, pr * LANES:(pr + 1) * LANES]
        vcat_ref[:, (2 * pr + 1) * LANES:(2 * pr + 2) * LANES] = jnp.ones((vcat_ref.shape[0], LANES), BF16)
    m_ref[...] = jnp.zeros(m_ref.shape, F32)
    l_ref[...] = jnp.ones(l_ref.shape, F32)
    for j in range(QBLKS_PER_STEP):
        rows = slice(j * QBLK, (j + 1) * QBLK)
        keys = slice(j * QBLK, (j + 2) * QBLK)
        first_ok = qi + jnp.where(pl.program_id(2) > 0, 0, QBLK) if j == 0 else qi
        mask = jnp.logical_and(jnp.logical_or(ki >= first_ok, ki >= QBLK), in_band)
        for pr in range(n_pairs):
            cols = slice(pr * LANES, (pr + 1) * LANES)
            q2 = q_ref[rows, cols]
            k2 = kcat_ref[keys, cols]
            v_ext = vcat_ref[keys, 2 * pr * LANES:(2 * pr + 2) * LANES]
            o_pair = None
            for a in range(2):
                own = low_half if a == 0 else jnp.logical_not(low_half)
                qm = jnp.where(own, q2, jnp.zeros_like(q2))
                s = lax.dot_general(qm, k2, _NT, preferred_element_type=F32)
                s = jnp.where(mask, s, NEG_BIG)
                m = jnp.max(s, axis=-1, keepdims=True)
                p = jnp.exp(s - m).astype(BF16)
                oe = jnp.dot(p, v_ext, preferred_element_type=F32)
                o_pair = oe[:, :LANES] if a == 0 else jnp.where(low_half, o_pair, oe[:, :LANES])
                h = 2 * pr + a
                m_ref[rows, h:h + 1] = m
                l_ref[rows, h:h + 1] = oe[:, LANES + h:LANES + h + 1]
            o_ref[rows, cols] = o_pair


def _group_attention(q, k, v):
    bsz, dil, length, width = q.shape
    step = QBLKS_PER_STEP * QBLK
    blk = (None, None, step, width)
    cur = lambda b, r, n: (b, r, n, 0)
    prev_blk = (None, None, QBLK, width)
    prev = lambda b, r, n: (b, r, jnp.maximum(QBLKS_PER_STEP * n - 1, 0), 0)
    stat_spec = pl.BlockSpec((None, None, step, LANES), cur)
    stat_shape = jax.ShapeDtypeStruct((bsz, dil, length, LANES), F32)
    return pl.pallas_call(
        _attn_kernel,
        grid=(bsz, dil, length // step),
        in_specs=[
            pl.BlockSpec(blk, cur),
            pl.BlockSpec(prev_blk, prev),
            pl.BlockSpec(blk, cur),
            pl.BlockSpec(prev_blk, prev),
            pl.BlockSpec(blk, cur),
        ],
        out_specs=[pl.BlockSpec(blk, cur), stat_spec, stat_spec],
        out_shape=[jax.ShapeDtypeStruct((bsz, dil, length, width), F32), stat_shape, stat_shape],
        scratch_shapes=[pltpu.VMEM((step + QBLK, width), BF16), pltpu.VMEM((step + QBLK, 2 * width), BF16)],
        compiler_params=_params("arbitrary", "arbitrary", "arbitrary"),
        name="group_attention",
    )(q, k, k, v, v)


def _to_token_order(src_ref, dst_ref):
    dil, n, w = src_ref.shape
    for r in range(dil):
        for c in range(w // LANES):
            dst_ref[c, pl.ds(r, n, stride=dil), :] = src_ref[r, :, c * LANES:(c + 1) * LANES]
    return jnp.concatenate([dst_ref[c] for c in range(w // LANES)], axis=1)


def _attn_out_kernel(x_ref, o0_ref, o1_ref, o2_ref, m0_ref, m1_ref, m2_ref, l0_ref, l1_ref, l2_ref,
                     wo_ref, g_ref, b_ref, out_ref, o_tok, stat_tok):
    width = o0_ref.shape[-1]

    def token_order(ref, scratch):
        if ref.shape[0] == 1:
            return ref[0]
        return _to_token_order(ref, scratch)

    ms = [token_order(r, stat_tok.at[i]) for i, r in enumerate((m0_ref, m1_ref, m2_ref))]
    ls = [token_order(r, stat_tok.at[N_GROUPS + i]) for i, r in enumerate((l0_ref, l1_ref, l2_ref))]
    top = jnp.maximum(jnp.maximum(ms[0], ms[1]), ms[2])
    es = [jnp.exp(m - top) for m in ms]
    den = es[0] * ls[0] + es[1] * ls[1] + es[2] * ls[2]
    head = lax.broadcasted_iota(I32, (LANES, width), 0)
    lane = lax.broadcasted_iota(I32, (LANES, width), 1)
    spread = (lane // HEAD_DIM == head).astype(BF16)
    mixed = jnp.zeros(out_ref.shape[:1] + (width,), F32)
    for e, o_ref in zip(es, (o0_ref, o1_ref, o2_ref)):
        hi, lo = _split_bf16(e / den)
        wide = (jnp.dot(hi, spread, preferred_element_type=F32)
                + jnp.dot(lo, spread, preferred_element_type=F32))
        mixed = mixed + wide * token_order(o_ref, o_tok)
    y = jnp.dot(mixed.astype(BF16), wo_ref[...], preferred_element_type=F32)
    out_ref[...] = _layer_norm(ALPHA * x_ref[...] + y, g_ref[...], b_ref[...])


def _attn_out(x, bsz, outs, stats, w_o, g, b):
    t, d = x.shape
    width = w_o.shape[0]
    tm = TM_RES
    tiles_per_seq = t // bsz // tm
    const = lambda bi, c: (0, 0)
    row = lambda bi, c: (bi * tiles_per_seq + c, 0)

    def res_spec(a):
        dil, w = a.shape[1], a.shape[3]
        return pl.BlockSpec((None, dil, tm // dil, w), lambda bi, c: (bi, 0, c, 0))

    return pl.pallas_call(
        _attn_out_kernel,
        grid=(bsz, tiles_per_seq),
        in_specs=([pl.BlockSpec((tm, d), row)] + [res_spec(a) for a in outs] + [res_spec(a) for a in stats]
                  + [pl.BlockSpec((width, d), const), pl.BlockSpec((1, d), const), pl.BlockSpec((1, d), const)]),
        out_specs=pl.BlockSpec((tm, d), row),
        out_shape=jax.ShapeDtypeStruct((t, d), F32),
        scratch_shapes=[pltpu.VMEM((width // LANES, tm, LANES), F32), pltpu.VMEM((2 * N_GROUPS, 1, tm, LANES), F32)],
        compiler_params=_params("arbitrary", "arbitrary"),
        name="attn_out",
    )(x, *outs, *stats, w_o, g, b)


def _proj_residue_kernel(x_ref, w_ref, *rest, dils, scale):
    outs, y_ref = rest[:-1], rest[-1]
    n_slabs, tm, _ = y_ref.shape
    width = n_slabs * LANES
    xb = x_ref[...].astype(BF16)
    for c, (o_ref, dil) in enumerate(zip(outs, dils)):
        y = jnp.dot(xb, w_ref[:, c * width:(c + 1) * width], preferred_element_type=F32)
        if scale != 1.0:
            y = y * scale
        if dil == 1:
            o_ref[0] = y.astype(o_ref.dtype)
        else:
            for s in range(n_slabs):
                y_ref[s] = y[:, s * LANES:(s + 1) * LANES]
            for r in range(dil):
                rows = [y_ref[s, pl.ds(r, tm // dil, stride=dil), :] for s in range(n_slabs)]
                o_ref[r] = jnp.concatenate(rows, axis=1).astype(o_ref.dtype)


def _proj_residue(x, bsz, w, dils, scale=1.0):
    t, d = x.shape
    seq = t // bsz
    width = w.shape[1] // len(dils)
    tm = TM_RES
    tiles_per_seq = seq // tm
    return pl.pallas_call(
        functools.partial(_proj_residue_kernel, dils=dils, scale=scale),
        grid=(bsz, tiles_per_seq),
        in_specs=[pl.BlockSpec((tm, d), lambda bi, c: (bi * tiles_per_seq + c, 0)),
                  pl.BlockSpec(w.shape, lambda bi, c: (0, 0))],
        out_specs=[pl.BlockSpec((None, dil, tm // dil, width), lambda bi, c: (bi, 0, c, 0)) for dil in dils],
        out_shape=[jax.ShapeDtypeStruct((bsz, dil, seq // dil, width), BF16) for dil in dils],
        scratch_shapes=[pltpu.VMEM((width // LANES, tm, LANES), F32)],
        compiler_params=_params("arbitrary", "arbitrary"),
        name="proj_residue",
    )(x, w)


def _attention_layer(x, bsz, w_q, w_o, kv_groups, g, b):
    dils = tuple(dil for _, dil in ATT_GROUPS)
    qs = _proj_residue(x, bsz, w_q, dils, scale=HEAD_DIM ** -0.5)
    outs, maxes, dens = [], [], []
    for q, (k_res, v_res) in zip(qs, kv_groups):
        o, m, l = _group_attention(q, k_res, v_res)
        outs.append(o)
        maxes.append(m)
        dens.append(l)
    return _attn_out(x, bsz, outs, maxes + dens, w_o, g, b)


def kernel(x, a_w_in, a_conv, a_w_out, kv_w, b_w_q, b_w_o, ffn_w_gate, ffn_w_up, ffn_w_down,
           moe_w_router, moe_w_gate, moe_w_up, moe_w_down, ln_g, ln_b):
    bsz, seq, d = x.shape
    h = x.reshape(bsz * seq, d)
    ln_g = ln_g.reshape(DEPTH, 2, 1, d)
    ln_b = ln_b.reshape(DEPTH, 2, 1, d)
    kv_groups = None
    for l in range(DEPTH):
        if l < N_A_LAYERS:
            h = _a_mixer(h, a_w_in[l].astype(BF16), a_conv[l], a_w_out[l].astype(BF16),
                         ln_g[l, 0], ln_b[l, 0], seq)
        else:
            if kv_groups is None:
                kv = _proj_residue(h, bsz, kv_w.astype(BF16), tuple(dil for _, dil in ATT_GROUPS for _ in range(2)))
                kv_groups = [(kv[2 * gi], kv[2 * gi + 1]) for gi in range(N_GROUPS)]
            j = l - N_A_LAYERS
            h = _attention_layer(h, bsz, b_w_q[j].astype(BF16), b_w_o[j].astype(BF16), kv_groups,
                                 ln_g[l, 0], ln_b[l, 0])
        i = l // 2
        if l % 2 == 0:
            h = _dense_ffn(h, ffn_w_gate[i].astype(BF16), ffn_w_up[i].astype(BF16),
                           ffn_w_down[i].astype(BF16), ln_g[l, 1], ln_b[l, 1], tf=1408)
        else:
            h = _moe_layer(h, moe_w_router[i], moe_w_gate.astype(BF16), moe_w_up.astype(BF16),
                           moe_w_down.astype(BF16), i, ln_g[l, 1], ln_b[l, 1])
    return h.reshape(bsz, seq, d)
```

```python
import functools

import jax
import jax.numpy as jnp
from jax import lax
from jax.experimental import pallas as pl
from jax.experimental.pallas import tpu as pltpu
from jax.experimental.pallas import tpu_sc as plsc

F32 = jnp.float32
BF16 = jnp.bfloat16
I32 = jnp.int32

DEPTH = 4
N_A_LAYERS = DEPTH // 2
CONV_WIDTH = 3
ATT_GROUPS = ((128, 1), (512, 4), (2048, 16))
N_GROUPS = len(ATT_GROUPS)
HEAD_DIM = 64
N_EXPERTS = 8
TOP_K = 2
ALPHA = (2.0 * DEPTH) ** 0.25
LN_EPS = 1e-5

QBLK = 128
assert all(w // d == QBLK for w, d in ATT_GROUPS)

LANES = 128
SUBLANES = 8
VMEM_LIMIT = 56 * 1024 * 1024
NEG_BIG = -1e30

TM_MIX = 512
TM_FFN = 1024
TM_EXP = 1024
TM_RES = 512

_NT = (((1,), (1,)), ((), ()))


def _layer_norm(z, g, b):
    mu = jnp.mean(z, axis=-1, keepdims=True)
    zc = z - mu
    var = jnp.mean(zc * zc, axis=-1, keepdims=True)
    return zc * lax.rsqrt(var + LN_EPS) * g + b


def _params(*semantics):
    return pltpu.CompilerParams(dimension_semantics=semantics, vmem_limit_bytes=VMEM_LIMIT)


def _a_mixer_kernel(x_ref, win_ref, conv_ref, wout_ref, g_ref, b_ref, o_ref, ubuf, *, tiles_per_seq):
    tm, d = x_ref.shape

    @pl.when(pl.program_id(0) % tiles_per_seq == 0)
    def _():
        ubuf[0:SUBLANES, :] = jnp.zeros((SUBLANES, d), F32)

    x = x_ref[...]
    p = jnp.dot(x.astype(BF16), win_ref[...], preferred_element_type=F32)
    ubuf[SUBLANES:SUBLANES + tm, :] = p[:, d:2 * d] * p[:, 2 * d:]
    cw = conv_ref[...]
    conv = (cw[2:3, :] * ubuf[SUBLANES:SUBLANES + tm, :]
            + cw[1:2, :] * ubuf[SUBLANES - 1:SUBLANES - 1 + tm, :]
            + cw[0:1, :] * ubuf[SUBLANES - 2:SUBLANES - 2 + tm, :])
    y = jnp.dot((p[:, :d] * conv).astype(BF16), wout_ref[...], preferred_element_type=F32)
    o_ref[...] = _layer_norm(ALPHA * x + y, g_ref[...], b_ref[...])
    ubuf[0:SUBLANES, :] = ubuf[tm:tm + SUBLANES, :]


def _a_mixer(x, w_in, conv_w, w_out, g, b, seq_len):
    t, d = x.shape
    tm = TM_MIX
    const = lambda i: (0, 0)
    return pl.pallas_call(
        functools.partial(_a_mixer_kernel, tiles_per_seq=seq_len // tm),
        grid=(t // tm,),
        in_specs=[
            pl.BlockSpec((tm, d), lambda i: (i, 0)),
            pl.BlockSpec((d, 3 * d), const),
            pl.BlockSpec((CONV_WIDTH, d), const),
            pl.BlockSpec((d, d), const),
            pl.BlockSpec((1, d), const),
            pl.BlockSpec((1, d), const),
        ],
        out_specs=pl.BlockSpec((tm, d), lambda i: (i, 0)),
        out_shape=jax.ShapeDtypeStruct((t, d), F32),
        scratch_shapes=[pltpu.VMEM((tm + SUBLANES, d), F32)],
        compiler_params=_params("arbitrary"),
        name="a_mixer",
    )(x, w_in, conv_w, w_out, g, b)


def _swiglu_partial(xb, wg_ref, wu_ref, wd_ref):
    gate = jnp.dot(xb, wg_ref[...], preferred_element_type=F32)
    up = jnp.dot(xb, wu_ref[...], preferred_element_type=F32)
    h = (gate * jax.nn.sigmoid(gate) * up).astype(BF16)
    return jnp.dot(h, wd_ref[...], preferred_element_type=F32)


def _dense_ffn_kernel(x_ref, wg_ref, wu_ref, wd_ref, g_ref, b_ref, o_ref, xb_ref, acc_ref):
    f = pl.program_id(1)

    @pl.when(f == 0)
    def _():
        xb_ref[...] = x_ref[...].astype(BF16)

    part = _swiglu_partial(xb_ref[...], wg_ref, wu_ref, wd_ref)

    @pl.when(f == 0)
    def _():
        acc_ref[...] = part

    @pl.when(f > 0)
    def _():
        acc_ref[...] += part

    @pl.when(f == pl.num_programs(1) - 1)
    def _():
        o_ref[...] = _layer_norm(ALPHA * x_ref[...] + acc_ref[...], g_ref[...], b_ref[...])


def _dense_ffn(x, w_gate, w_up, w_down, g, b, tf):
    t, d = x.shape
    d_ff = w_gate.shape[1]
    tm = TM_FFN
    const = lambda i, f: (0, 0)
    return pl.pallas_call(
        _dense_ffn_kernel,
        grid=(t // tm, d_ff // tf),
        in_specs=[
            pl.BlockSpec((tm, d), lambda i, f: (i, 0)),
            pl.BlockSpec((d, tf), lambda i, f: (0, f)),
            pl.BlockSpec((d, tf), lambda i, f: (0, f)),
            pl.BlockSpec((tf, d), lambda i, f: (f, 0)),
            pl.BlockSpec((1, d), const),
            pl.BlockSpec((1, d), const),
        ],
        out_specs=pl.BlockSpec((tm, d), lambda i, f: (i, 0)),
        out_shape=jax.ShapeDtypeStruct((t, d), F32),
        scratch_shapes=[pltpu.VMEM((tm, d), BF16), pltpu.VMEM((tm, d), F32)],
        compiler_params=_params("arbitrary", "arbitrary"),
        name="dense_ffn",
    )(x, w_gate, w_up, w_down, g, b)


def _expert_ffn_kernel(te_ref, nv_ref, x_ref, wg_ref, wu_ref, wd_ref, o_ref, xb_ref, acc_ref):
    del te_ref
    i = pl.program_id(0)
    f = pl.program_id(1)
    last = f == pl.num_programs(1) - 1
    valid = i < nv_ref[0]

    @pl.when(valid)
    def _():
        @pl.when(f == 0)
        def _():
            xb_ref[...] = x_ref[...].astype(BF16)

        part = _swiglu_partial(xb_ref[...], wg_ref, wu_ref, wd_ref)

        @pl.when(f == 0)
        def _():
            acc_ref[...] = part

        @pl.when(f > 0)
        def _():
            acc_ref[...] += part

        @pl.when(last)
        def _():
            o_ref[...] = acc_ref[...]

    @pl.when(jnp.logical_and(last, jnp.logical_not(valid)))
    def _():
        o_ref[...] = jnp.zeros(o_ref.shape, F32)


def _expert_ffn(xs, tile_expert, n_valid, w_gate, w_up, w_down, layer, tf):
    d = w_gate.shape[2]
    d_exp = w_gate.shape[3]
    tm = TM_EXP
    n_rows = xs.shape[0] // tm * tm
    nf = d_exp // tf

    def w_col(i, f, te, nv):
        return (layer, te[i], 0, jnp.where(i < nv[0], f, nf - 1))

    def w_row(i, f, te, nv):
        return (layer, te[i], jnp.where(i < nv[0], f, nf - 1), 0)

    return pl.pallas_call(
        _expert_ffn_kernel,
        grid_spec=pltpu.PrefetchScalarGridSpec(
            num_scalar_prefetch=2,
            grid=(n_rows // tm, nf),
            in_specs=[
                pl.BlockSpec((tm, d), lambda i, f, te, nv: (i, 0)),
                pl.BlockSpec((None, None, d, tf), w_col),
                pl.BlockSpec((None, None, d, tf), w_col),
                pl.BlockSpec((None, None, tf, d), w_row),
            ],
            out_specs=pl.BlockSpec((tm, d), lambda i, f, te, nv: (i, 0)),
            scratch_shapes=[pltpu.VMEM((tm, d), BF16), pltpu.VMEM((tm, d), F32)],
        ),
        out_shape=jax.ShapeDtypeStruct((n_rows, d), F32),
        compiler_params=_params("arbitrary", "arbitrary"),
        name="expert_ffn",
    )(tile_expert, n_valid, xs, w_gate, w_up, w_down)


def _split_bf16(v):
    hi = v.astype(BF16)
    lo = (v - hi.astype(F32)).astype(BF16)
    return hi, lo


def _router_kernel(x_ref, wr_ref, idx_ref, gate_ref):
    xh, xl = _split_bf16(x_ref[...])
    wh, wl = _split_bf16(wr_ref[...])
    logits = (lax.dot_general(wh, xh, _NT, preferred_element_type=F32)
              + lax.dot_general(wh, xl, _NT, preferred_element_type=F32)
              + lax.dot_general(wl, xh, _NT, preferred_element_type=F32))
    e = lax.broadcasted_iota(I32, logits.shape, 0)
    m1 = jnp.max(logits, axis=0, keepdims=True)
    i1 = jnp.min(jnp.where(logits == m1, e, N_EXPERTS), axis=0, keepdims=True)
    rest = jnp.where(e == i1, -jnp.inf, logits)
    m2 = jnp.max(rest, axis=0, keepdims=True)
    i2 = jnp.min(jnp.where(rest == m2, e, N_EXPERTS), axis=0, keepdims=True)
    r = jnp.exp(m2 - m1)
    idx_ref[...] = jnp.concatenate([i1, i2], axis=0)
    gate_ref[...] = jnp.concatenate([1.0 / (1.0 + r), r / (1.0 + r)], axis=0)


def _router(x, w_router_t):
    t, d = x.shape
    tm = TM_MIX
    return pl.pallas_call(
        _router_kernel,
        grid=(t // tm,),
        in_specs=[
            pl.BlockSpec((tm, d), lambda i: (i, 0)),
            pl.BlockSpec((N_EXPERTS, d), lambda i: (0, 0)),
        ],
        out_specs=[
            pl.BlockSpec((TOP_K, tm), lambda i: (0, i)),
            pl.BlockSpec((TOP_K, tm), lambda i: (0, i)),
        ],
        out_shape=[jax.ShapeDtypeStruct((TOP_K, t), I32), jax.ShapeDtypeStruct((TOP_K, t), F32)],
        compiler_params=_params("arbitrary"),
        name="router",
    )(x, w_router_t)


def _dispatch_kernel(ends_ref, nv_ref, pos_ref, x_ref, xs_ref, zero_ref, sem, zero_sem):
    tm = x_ref.shape[0]

    @pl.when(pl.program_id(0) == 0)
    def _():
        zr = zero_ref.shape[0]
        zero_ref[...] = jnp.zeros(zero_ref.shape, F32)

        def zero_tile(tile):
            copies = [pltpu.make_async_copy(zero_ref, xs_ref.at[pl.ds(tile * TM_EXP + q * zr, zr)], zero_sem)
                      for q in range(TM_EXP // zr)]
            for c in copies:
                c.start()
            for c in copies:
                c.wait()

        for e in range(N_EXPERTS):
            zero_tile(jnp.maximum(ends_ref[e] // TM_EXP - 1, 0))
        for u in range(N_EXPERTS):
            @pl.when(nv_ref[0] + u < xs_ref.shape[0] // TM_EXP)
            def _():
                zero_tile(nv_ref[0] + u)

    def issue(j, carry):
        for k in range(TOP_K):
            pltpu.make_async_copy(x_ref.at[pl.ds(j, 1)], xs_ref.at[pl.ds(pos_ref[0, 0, TOP_K * j + k], 1)], sem).start()
        return carry

    lax.fori_loop(0, tm, issue, 0, unroll=4)
    for k in range(TOP_K):
        pltpu.make_async_copy(x_ref, xs_ref.at[pl.ds(0, tm)], sem).wait()


def _dispatch(x, pos, ends, n_valid, n_rows):
    t, d = x.shape
    tm = TM_MIX
    return pl.pallas_call(
        _dispatch_kernel,
        grid_spec=pltpu.PrefetchScalarGridSpec(
            num_scalar_prefetch=2,
            grid=(t // tm,),
            in_specs=[
                pl.BlockSpec((1, 1, TOP_K * tm), lambda i, ends, nv: (i, 0, 0), memory_space=pltpu.SMEM),
                pl.BlockSpec((tm, d), lambda i, ends, nv: (i, 0)),
            ],
            out_specs=pl.BlockSpec(memory_space=pl.ANY),
            scratch_shapes=[pltpu.VMEM((tm, d), F32), pltpu.SemaphoreType.DMA(()), pltpu.SemaphoreType.DMA(())],
        ),
        out_shape=jax.ShapeDtypeStruct((n_rows, d), F32),
        compiler_params=_params("arbitrary"),
        name="moe_dispatch",
    )(ends, n_valid, pos.reshape(t // tm, 1, TOP_K * tm), x)


def _moe_combine_kernel(pos_ref, pos_next_ref, x_ref, gate_ref, g_ref, b_ref, ys_ref, o_ref, ybuf, sem):
    tm, d = x_ref.shape
    i = pl.program_id(0)
    slot = i % 2

    def gather(idx_ref, s):
        def issue(j, carry):
            for k in range(TOP_K):
                pltpu.make_async_copy(ys_ref.at[pl.ds(idx_ref[0, 0, TOP_K * j + k], 1)], ybuf.at[s, k, pl.ds(j, 1)],
                                      sem.at[s]).start()
            return carry

        lax.fori_loop(0, tm, issue, 0, unroll=4)

    @pl.when(i == 0)
    def _():
        gather(pos_ref, 0)

    @pl.when(i + 1 < pl.num_programs(0))
    def _():
        gather(pos_next_ref, 1 - slot)

    for k in range(TOP_K):
        pltpu.make_async_copy(ys_ref.at[pl.ds(0, tm)], ybuf.at[slot, k], sem.at[slot]).wait()
    gates = gate_ref[...]
    ff = jnp.zeros((tm, d), F32)
    for k in range(TOP_K):
        ff = ff + gates[:, k:k + 1] * ybuf[slot, k]
    o_ref[...] = _layer_norm(ALPHA * x_ref[...] + ff, g_ref[...], b_ref[...])


def _moe_combine(x, ys, pos, gates, g, b):
    t, d = x.shape
    tm = TM_MIX
    steps = t // tm
    const = lambda i: (0, 0)
    pos = pos.reshape(steps, 1, TOP_K * tm)
    return pl.pallas_call(
        _moe_combine_kernel,
        grid=(steps,),
        in_specs=[
            pl.BlockSpec((1, 1, TOP_K * tm), lambda i: (i, 0, 0), memory_space=pltpu.SMEM),
            pl.BlockSpec((1, 1, TOP_K * tm), lambda i: (jnp.minimum(i + 1, steps - 1), 0, 0), memory_space=pltpu.SMEM),
            pl.BlockSpec((tm, d), lambda i: (i, 0)),
            pl.BlockSpec((tm, TOP_K), lambda i: (i, 0)),
            pl.BlockSpec((1, d), const),
            pl.BlockSpec((1, d), const),
            pl.BlockSpec(memory_space=pl.ANY),
        ],
        out_specs=pl.BlockSpec((tm, d), lambda i: (i, 0)),
        out_shape=jax.ShapeDtypeStruct((t, d), F32),
        scratch_shapes=[pltpu.VMEM((2, TOP_K, tm, d), F32), pltpu.SemaphoreType.DMA((2,))],
        compiler_params=_params("arbitrary"),
        name="moe_combine",
    )(pos, pos, x, gates, g, b, ys)


def _moe_layer(x, w_router, w_gate, w_up, w_down, layer, g, b):
    t, d = x.shape
    n_assign = TOP_K * t
    idx, gates = _router(x, w_router.T)

    e_flat = idx.T.reshape(n_assign)
    onehot = (e_flat[:, None] == jnp.arange(N_EXPERTS, dtype=I32)[None, :]).astype(I32)
    csum = jnp.cumsum(onehot, axis=0)
    rank = jnp.sum(csum * onehot, axis=1) - 1
    counts = csum[-1]
    padded = ((counts + TM_EXP - 1) // TM_EXP) * TM_EXP
    ends = jnp.cumsum(padded)
    pos = (jnp.sum((ends - padded)[None, :] * onehot, axis=1) + rank).astype(I32)
    n_tiles = n_assign // TM_EXP + N_EXPERTS
    tile_ids = jnp.arange(n_tiles, dtype=I32)
    tile_expert = jnp.minimum(
        jnp.sum((tile_ids[:, None] >= (ends // TM_EXP)[None, :]).astype(I32), axis=1), N_EXPERTS - 1).astype(I32)
    n_valid = (ends[-1:] // TM_EXP).astype(I32)

    pos_k = pos.reshape(t, TOP_K)
    n_rows = n_tiles * TM_EXP
    xs = _sc_dispatch(x, pos_k[:, 0], pos_k[:, 1], _unused_rows(counts, padded, ends, n_rows), n_rows)
    ys = _expert_ffn(xs, tile_expert, n_valid, w_gate, w_up, w_down, layer, tf=512)
    y0, y1 = _sc_gather2(ys, pos_k[:, 0], pos_k[:, 1])
    return _moe_sum(x, y0, y1, gates.T, g, b)


SC_CORES = 2
SC_SUBCORES = 16
SC_ROWS = 64


def _sc_worker_base(per_worker):
    return (lax.axis_index("subcore") * SC_CORES + lax.axis_index("core")) * per_worker


def _unused_rows(counts, padded, ends, n_rows):
    lane = jnp.arange(SC_ROWS, dtype=I32)
    chunk0 = jnp.arange(TM_EXP // SC_ROWS, dtype=I32) * SC_ROWS
    spare = n_rows + lane
    pad = (ends - padded + counts)[:, None, None] + chunk0[None, :, None] + lane[None, None, :]
    pad = jnp.where(pad < ends[:, None, None], pad, spare[None, None, :])
    tail = ends[-1] + jnp.arange(N_EXPERTS * TM_EXP // SC_ROWS, dtype=I32)[:, None] * SC_ROWS + lane[None, :]
    tail = jnp.where(tail < n_rows, tail, spare[None, :])
    return jnp.concatenate([pad.reshape(-1), tail.reshape(-1)]).astype(I32)


def _sc_dispatch(x, pos0, pos1, unused, n_rows):
    t, d = x.shape
    workers = SC_CORES * SC_SUBCORES
    per_worker = t // workers
    zero_chunks = unused.shape[0] // SC_ROWS // workers
    mesh = plsc.VectorSubcoreMesh(core_axis_name="core", subcore_axis_name="subcore")

    @pl.kernel(out_type=jax.ShapeDtypeStruct((n_rows + SC_ROWS, d), F32), mesh=mesh,
               scratch_types=[pltpu.VMEM((SC_ROWS,), I32), pltpu.VMEM((SC_ROWS,), I32),
                              pltpu.VMEM((SC_ROWS, d), F32)])
    def run(x_hbm, p0_hbm, p1_hbm, unused_hbm, zeros_hbm, o_hbm, i0_v, i1_v, rows_v):
        pltpu.sync_copy(zeros_hbm, rows_v)
        zbase = _sc_worker_base(zero_chunks * SC_ROWS)

        @pl.loop(0, zero_chunks)
        def _(c):
            pltpu.sync_copy(unused_hbm.at[pl.ds(zbase + c * SC_ROWS, SC_ROWS)], i0_v)
            pltpu.sync_copy(rows_v, o_hbm.at[i0_v])

        base = _sc_worker_base(per_worker)

        @pl.loop(0, per_worker // SC_ROWS)
        def _(c):
            off = base + c * SC_ROWS
            pltpu.sync_copy(p0_hbm.at[pl.ds(off, SC_ROWS)], i0_v)
            pltpu.sync_copy(p1_hbm.at[pl.ds(off, SC_ROWS)], i1_v)
            pltpu.sync_copy(x_hbm.at[pl.ds(off, SC_ROWS)], rows_v)
            pltpu.sync_copy(rows_v, o_hbm.at[i0_v])
            pltpu.sync_copy(rows_v, o_hbm.at[i1_v])

    return run(x, pos0, pos1, unused, jnp.zeros((SC_ROWS, d), F32))


def _sc_gather2(ys, pos0, pos1):
    t = pos0.shape[0]
    d = ys.shape[1]
    per_worker = t // (SC_CORES * SC_SUBCORES)
    mesh = plsc.VectorSubcoreMesh(core_axis_name="core", subcore_axis_name="subcore")
    out = jax.ShapeDtypeStruct((t, d), F32)

    @pl.kernel(out_type=(out, out), mesh=mesh,
               scratch_types=[pltpu.VMEM((SC_ROWS,), I32), pltpu.VMEM((SC_ROWS, d), F32)])
    def run(y_hbm, p0_hbm, p1_hbm, o0_hbm, o1_hbm, i_v, rows_v):
        base = _sc_worker_base(per_worker)

        @pl.loop(0, per_worker // SC_ROWS)
        def _(c):
            off = base + c * SC_ROWS
            for p_hbm, o_hbm in ((p0_hbm, o0_hbm), (p1_hbm, o1_hbm)):
                pltpu.sync_copy(p_hbm.at[pl.ds(off, SC_ROWS)], i_v)
                pltpu.sync_copy(y_hbm.at[i_v], rows_v)
                pltpu.sync_copy(rows_v, o_hbm.at[pl.ds(off, SC_ROWS)])

    return run(ys, pos0, pos1)


def _moe_sum_kernel(x_ref, y0_ref, y1_ref, gate_ref, g_ref, b_ref, o_ref):
    gates = gate_ref[...]
    ff = gates[:, 0:1] * y0_ref[...] + gates[:, 1:2] * y1_ref[...]
    o_ref[...] = _layer_norm(ALPHA * x_ref[...] + ff, g_ref[...], b_ref[...])


def _moe_sum(x, y0, y1, gates, g, b):
    t, d = x.shape
    tm = TM_MIX
    const = lambda i: (0, 0)
    row = lambda i: (i, 0)
    return pl.pallas_call(
        _moe_sum_kernel,
        grid=(t // tm,),
        in_specs=[pl.BlockSpec((tm, d), row), pl.BlockSpec((tm, d), row), pl.BlockSpec((tm, d), row),
                  pl.BlockSpec((tm, TOP_K), row), pl.BlockSpec((1, d), const), pl.BlockSpec((1, d), const)],
        out_specs=pl.BlockSpec((tm, d), row),
        out_shape=jax.ShapeDtypeStruct((t, d), F32),
        compiler_params=_params("arbitrary"),
        name="moe_sum",
    )(x, y0, y1, gates, g, b)


QBLKS_PER_STEP = 2


def _attn_kernel(q_ref, kp_ref, kc_ref, vp_ref, vc_ref, o_ref, m_ref, l_ref, kcat_ref, vcat_ref):
    n_pairs = q_ref.shape[1] // LANES
    qi = lax.broadcasted_iota(I32, (QBLK, 2 * QBLK), 0)
    ki = lax.broadcasted_iota(I32, (QBLK, 2 * QBLK), 1)
    in_band = ki - QBLK <= qi
    lane = lax.broadcasted_iota(I32, (QBLK, LANES), 1)
    low_half = lane < HEAD_DIM
    kcat_ref[0:QBLK, :] = kp_ref[...]
    kcat_ref[QBLK:, :] = kc_ref[...]
    for pr in range(n_pairs):
        vcat_ref[0:QBLK, 2 * pr * LANES:(2 * pr + 1) * LANES] = vp_ref[:, pr * LANES:(pr + 1) * LANES]
        vcat_ref[QBLK:, 2 * pr * LANES:(2 * pr + 1) * LANES] = vc_ref[:, pr * LANES:(pr + 1) * LANES]
        vcat_ref[:, (2 * pr + 1) * LANES:(2 * pr + 2) * LANES] = jnp.ones((vcat_ref.shape[0], LANES), BF16)
    m_ref[...] = jnp.zeros(m_ref.shape, F32)
    l_ref[...] = jnp.ones(l_ref.shape, F32)
    for j in range(QBLKS_PER_STEP):
        rows = slice(j * QBLK, (j + 1) * QBLK)
        keys = slice(j * QBLK, (j + 2) * QBLK)
        first_ok = qi + jnp.where(pl.program_id(2) > 0, 0, QBLK) if j == 0 else qi
        mask = jnp.logical_and(jnp.logical_or(ki >= first_ok, ki >= QBLK), in_band)
        for pr in range(n_pairs):
            cols = slice(pr * LANES, (pr + 1) * LANES)
            q2 = q_ref[rows, cols]
            k2 = kcat_ref[keys, cols]
            v_ext = vcat_ref[keys, 2 * pr * LANES:(2 * pr + 2) * LANES]
            o_pair = None
            for a in range(2):
                own = low_half if a == 0 else jnp.logical_not(low_half)
                qm = jnp.where(own, q2, jnp.zeros_like(q2))
                s = lax.dot_general(qm, k2, _NT, preferred_element_type=F32)
                s = jnp.where(mask, s, NEG_BIG)
                m = jnp.max(s, axis=-1, keepdims=True)
                p = jnp.exp(s - m).astype(BF16)
                oe = jnp.dot(p, v_ext, preferred_element_type=F32)
                o_pair = oe[:, :LANES] if a == 0 else jnp.where(low_half, o_pair, oe[:, :LANES])
                h = 2 * pr + a
                m_ref[rows, h:h + 1] = m
                l_ref[rows, h:h + 1] = oe[:, LANES + h:LANES + h + 1]
            o_ref[rows, cols] = o_pair


def _group_attention(q, k, v):
    bsz, dil, length, width = q.shape
    step = QBLKS_PER_STEP * QBLK
    blk = (None, None, step, width)
    cur = lambda b, r, n: (b, r, n, 0)
    prev_blk = (None, None, QBLK, width)
    prev = lambda b, r, n: (b, r, jnp.maximum(QBLKS_PER_STEP * n - 1, 0), 0)
    stat_spec = pl.BlockSpec((None, None, step, LANES), cur)
    stat_shape = jax.ShapeDtypeStruct((bsz, dil, length, LANES), F32)
    return pl.pallas_call(
        _attn_kernel,
        grid=(bsz, dil, length // step),
        in_specs=[
            pl.BlockSpec(blk, cur),
            pl.BlockSpec(prev_blk, prev),
            pl.BlockSpec(blk, cur),
            pl.BlockSpec(prev_blk, prev),
            pl.BlockSpec(blk, cur),
        ],
        out_specs=[pl.BlockSpec(blk, cur), stat_spec, stat_spec],
        out_shape=[jax.ShapeDtypeStruct((bsz, dil, length, width), F32), stat_shape, stat_shape],
        scratch_shapes=[pltpu.VMEM((step + QBLK, width), BF16), pltpu.VMEM((step + QBLK, 2 * width), BF16)],
        compiler_params=_params("arbitrary", "arbitrary", "arbitrary"),
        name="group_attention",
    )(q, k, k, v, v)


def _to_token_order(src_ref, dst_ref):
    dil, n, w = src_ref.shape
    for r in range(dil):
        for c in range(w // LANES):
            dst_ref[c, pl.ds(r, n, stride=dil), :] = src_ref[r, :, c * LANES:(c + 1) * LANES]
    return jnp.concatenate([dst_ref[c] for c in range(w // LANES)], axis=1)


def _attn_out_kernel(x_ref, o0_ref, o1_ref, o2_ref, m0_ref, m1_ref, m2_ref, l0_ref, l1_ref, l2_ref,
                     wo_ref, g_ref, b_ref, out_ref, o_tok, stat_tok):
    width = o0_ref.shape[-1]

    def token_order(ref, scratch):
        if ref.shape[0] == 1:
            return ref[0]
        return _to_token_order(ref, scratch)

    ms = [token_order(r, stat_tok.at[i]) for i, r in enumerate((m0_ref, m1_ref, m2_ref))]
    ls = [token_order(r, stat_tok.at[N_GROUPS + i]) for i, r in enumerate((l0_ref, l1_ref, l2_ref))]
    top = jnp.maximum(jnp.maximum(ms[0], ms[1]), ms[2])
    es = [jnp.exp(m - top) for m in ms]
    den = es[0] * ls[0] + es[1] * ls[1] + es[2] * ls[2]
    head = lax.broadcasted_iota(I32, (LANES, width), 0)
    lane = lax.broadcasted_iota(I32, (LANES, width), 1)
    spread = (lane // HEAD_DIM == head).astype(BF16)
    mixed = jnp.zeros(out_ref.shape[:1] + (width,), F32)
    for e, o_ref in zip(es, (o0_ref, o1_ref, o2_ref)):
        hi, lo = _split_bf16(e / den)
        wide = (jnp.dot(hi, spread, preferred_element_type=F32)
                + jnp.dot(lo, spread, preferred_element_type=F32))
        mixed = mixed + wide * token_order(o_ref, o_tok)
    y = jnp.dot(mixed.astype(BF16), wo_ref[...], preferred_element_type=F32)
    out_ref[...] = _layer_norm(ALPHA * x_ref[...] + y, g_ref[...], b_ref[...])


def _attn_out(x, bsz, outs, stats, w_o, g, b):
    t, d = x.shape
    width = w_o.shape[0]
    tm = TM_RES
    tiles_per_seq = t // bsz // tm
    const = lambda bi, c: (0, 0)
    row = lambda bi, c: (bi * tiles_per_seq + c, 0)

    def res_spec(a):
        dil, w = a.shape[1], a.shape[3]
        return pl.BlockSpec((None, dil, tm // dil, w), lambda bi, c: (bi, 0, c, 0))

    return pl.pallas_call(
        _attn_out_kernel,
        grid=(bsz, tiles_per_seq),
        in_specs=([pl.BlockSpec((tm, d), row)] + [res_spec(a) for a in outs] + [res_spec(a) for a in stats]
                  + [pl.BlockSpec((width, d), const), pl.BlockSpec((1, d), const), pl.BlockSpec((1, d), const)]),
        out_specs=pl.BlockSpec((tm, d), row),
        out_shape=jax.ShapeDtypeStruct((t, d), F32),
        scratch_shapes=[pltpu.VMEM((width // LANES, tm, LANES), F32), pltpu.VMEM((2 * N_GROUPS, 1, tm, LANES), F32)],
        compiler_params=_params("arbitrary", "arbitrary"),
        name="attn_out",
    )(x, *outs, *stats, w_o, g, b)


def _proj_residue_kernel(x_ref, w_ref, *rest, dils, scale):
    outs, y_ref = rest[:-1], rest[-1]
    n_slabs, tm, _ = y_ref.shape
    width = n_slabs * LANES
    xb = x_ref[...].astype(BF16)
    for c, (o_ref, dil) in enumerate(zip(outs, dils)):
        y = jnp.dot(xb, w_ref[:, c * width:(c + 1) * width], preferred_element_type=F32)
        if scale != 1.0:
            y = y * scale
        if dil == 1:
            o_ref[0] = y.astype(o_ref.dtype)
        else:
            for s in range(n_slabs):
                y_ref[s] = y[:, s * LANES:(s + 1) * LANES]
            for r in range(dil):
                rows = [y_ref[s, pl.ds(r, tm // dil, stride=dil), :] for s in range(n_slabs)]
                o_ref[r] = jnp.concatenate(rows, axis=1).astype(o_ref.dtype)


def _proj_residue(x, bsz, w, dils, scale=1.0):
    t, d = x.shape
    seq = t // bsz
    width = w.shape[1] // len(dils)
    tm = TM_RES
    tiles_per_seq = seq // tm
    return pl.pallas_call(
        functools.partial(_proj_residue_kernel, dils=dils, scale=scale),
        grid=(bsz, tiles_per_seq),
        in_specs=[pl.BlockSpec((tm, d), lambda bi, c: (bi * tiles_per_seq + c, 0)),
                  pl.BlockSpec(w.shape, lambda bi, c: (0, 0))],
        out_specs=[pl.BlockSpec((None, dil, tm // dil, width), lambda bi, c: (bi, 0, c, 0)) for dil in dils],
        out_shape=[jax.ShapeDtypeStruct((bsz, dil, seq // dil, width), BF16) for dil in dils],
        scratch_shapes=[pltpu.VMEM((width // LANES, tm, LANES), F32)],
        compiler_params=_params("arbitrary", "arbitrary"),
        name="proj_residue",
    )(x, w)


def _attention_layer(x, bsz, w_q, w_o, kv_groups, g, b):
    dils = tuple(dil for _, dil in ATT_GROUPS)
    qs = _proj_residue(x, bsz, w_q, dils, scale=HEAD_DIM ** -0.5)
    outs, maxes, dens = [], [], []
    for q, (k_res, v_res) in zip(qs, kv_groups):
        o, m, l = _group_attention(q, k_res, v_res)
        outs.append(o)
        maxes.append(m)
        dens.append(l)
    return _attn_out(x, bsz, outs, maxes + dens, w_o, g, b)


def kernel(x, a_w_in, a_conv, a_w_out, kv_w, b_w_q, b_w_o, ffn_w_gate, ffn_w_up, ffn_w_down,
           moe_w_router, moe_w_gate, moe_w_up, moe_w_down, ln_g, ln_b):
    bsz, seq, d = x.shape
    h = x.reshape(bsz * seq, d)
    ln_g = ln_g.reshape(DEPTH, 2, 1, d)
    ln_b = ln_b.reshape(DEPTH, 2, 1, d)
    kv_groups = None
    for l in range(DEPTH):
        if l < N_A_LAYERS:
            h = _a_mixer(h, a_w_in[l].astype(BF16), a_conv[l], a_w_out[l].astype(BF16),
                         ln_g[l, 0], ln_b[l, 0], seq)
        else:
            if kv_groups is None:
                kv = _proj_residue(h, bsz, kv_w.astype(BF16), tuple(dil for _, dil in ATT_GROUPS for _ in range(2)))
                kv_groups = [(kv[2 * gi], kv[2 * gi + 1]) for gi in range(N_GROUPS)]
            j = l - N_A_LAYERS
            h = _attention_layer(h, bsz, b_w_q[j].astype(BF16), b_w_o[j].astype(BF16), kv_groups,
                                 ln_g[l, 0], ln_b[l, 0])
        i = l // 2
        if l % 2 == 0:
            h = _dense_ffn(h, ffn_w_gate[i].astype(BF16), ffn_w_up[i].astype(BF16),
                           ffn_w_down[i].astype(BF16), ln_g[l, 1], ln_b[l, 1], tf=1408)
        else:
            h = _moe_layer(h, moe_w_router[i], moe_w_gate.astype(BF16), moe_w_up.astype(BF16),
                           moe_w_down.astype(BF16), i, ln_g[l, 1], ln_b[l, 1])
    return h.reshape(bsz, seq, d)
```

```python
import functools

import jax
import jax.numpy as jnp
from jax import lax
from jax.experimental import pallas as pl
from jax.experimental.pallas import tpu as pltpu
from jax.experimental.pallas import tpu_sc as plsc

F32 = jnp.float32
U32 = jnp.uint32
BF16 = jnp.bfloat16
I32 = jnp.int32

DEPTH = 4
N_A_LAYERS = DEPTH // 2
CONV_WIDTH = 3
ATT_GROUPS = ((128, 1), (512, 4), (2048, 16))
N_GROUPS = len(ATT_GROUPS)
HEAD_DIM = 64
N_EXPERTS = 8
TOP_K = 2
ALPHA = (2.0 * DEPTH) ** 0.25
LN_EPS = 1e-5

QBLK = 128
assert all(w // d == QBLK for w, d in ATT_GROUPS)

LANES = 128
SUBLANES = 8
VMEM_LIMIT = 56 * 1024 * 1024
NEG_BIG = -1e30

TM_MIX = 512
TM_FFN = 1024
TM_EXP = 1024
TM_RES = 512
N_STREAMS = 1

_NT = (((1,), (1,)), ((), ()))


def _layer_norm(z, g, b):
    mu = jnp.mean(z, axis=-1, keepdims=True)
    zc = z - mu
    var = jnp.mean(zc * zc, axis=-1, keepdims=True)
    return zc * lax.rsqrt(var + LN_EPS) * g + b


def _params(*semantics):
    return pltpu.CompilerParams(dimension_semantics=semantics, vmem_limit_bytes=VMEM_LIMIT)


def _a_mixer_kernel(x_ref, win_ref, conv_ref, wout_ref, g_ref, b_ref, o_ref, ubuf, *, tiles_per_seq):
    tm, d = x_ref.shape

    @pl.when(pl.program_id(0) % tiles_per_seq == 0)
    def _():
        ubuf[0:SUBLANES, :] = jnp.zeros((SUBLANES, d), F32)

    x = x_ref[...]
    p = jnp.dot(x.astype(BF16), win_ref[...], preferred_element_type=F32)
    ubuf[SUBLANES:SUBLANES + tm, :] = p[:, d:2 * d] * p[:, 2 * d:]
    cw = conv_ref[...]
    conv = (cw[2:3, :] * ubuf[SUBLANES:SUBLANES + tm, :]
            + cw[1:2, :] * ubuf[SUBLANES - 1:SUBLANES - 1 + tm, :]
            + cw[0:1, :] * ubuf[SUBLANES - 2:SUBLANES - 2 + tm, :])
    y = jnp.dot((p[:, :d] * conv).astype(BF16), wout_ref[...], preferred_element_type=F32)
    o_ref[...] = _layer_norm(ALPHA * x + y, g_ref[...], b_ref[...])
    ubuf[0:SUBLANES, :] = ubuf[tm:tm + SUBLANES, :]


def _a_mixer(x, w_in, conv_w, w_out, g, b, seq_len):
    t, d = x.shape
    tm = TM_MIX
    const = lambda i: (0, 0)
    return pl.pallas_call(
        functools.partial(_a_mixer_kernel, tiles_per_seq=seq_len // tm),
        grid=(t // tm,),
        in_specs=[
            pl.BlockSpec((tm, d), lambda i: (i, 0)),
            pl.BlockSpec((d, 3 * d), const),
            pl.BlockSpec((CONV_WIDTH, d), const),
            pl.BlockSpec((d, d), const),
            pl.BlockSpec((1, d), const),
            pl.BlockSpec((1, d), const),
        ],
        out_specs=pl.BlockSpec((tm, d), lambda i: (i, 0)),
        out_shape=jax.ShapeDtypeStruct((t, d), F32),
        scratch_shapes=[pltpu.VMEM((tm + SUBLANES, d), F32)],
        compiler_params=_params("arbitrary"),
        name="a_mixer",
    )(x, w_in, conv_w, w_out, g, b)


def _swiglu_partial(xb, wg_ref, wu_ref, wd_ref):
    gate = jnp.dot(xb, wg_ref[...], preferred_element_type=F32)
    up = jnp.dot(xb, wu_ref[...], preferred_element_type=F32)
    h = (gate * jax.nn.sigmoid(gate) * up).astype(BF16)
    return jnp.dot(h, wd_ref[...], preferred_element_type=F32)


def _dense_ffn_kernel(x_ref, wg_ref, wu_ref, wd_ref, g_ref, b_ref, o_ref, xb_ref, acc_ref):
    f = pl.program_id(1)

    @pl.when(f == 0)
    def _():
        xb_ref[...] = x_ref[...].astype(BF16)

    part = _swiglu_partial(xb_ref[...], wg_ref, wu_ref, wd_ref)

    @pl.when(f == 0)
    def _():
        acc_ref[...] = part

    @pl.when(f > 0)
    def _():
        acc_ref[...] += part

    @pl.when(f == pl.num_programs(1) - 1)
    def _():
        o_ref[...] = _layer_norm(ALPHA * x_ref[...] + acc_ref[...], g_ref[...], b_ref[...])


def _dense_ffn(x, w_gate, w_up, w_down, g, b, tf):
    t, d = x.shape
    d_ff = w_gate.shape[1]
    tm = TM_FFN
    const = lambda i, f: (0, 0)
    return pl.pallas_call(
        _dense_ffn_kernel,
        grid=(t // tm, d_ff // tf),
        in_specs=[
            pl.BlockSpec((tm, d), lambda i, f: (i, 0)),
            pl.BlockSpec((d, tf), lambda i, f: (0, f)),
            pl.BlockSpec((d, tf), lambda i, f: (0, f)),
            pl.BlockSpec((tf, d), lambda i, f: (f, 0)),
            pl.BlockSpec((1, d), const),
            pl.BlockSpec((1, d), const),
        ],
        out_specs=pl.BlockSpec((tm, d), lambda i, f: (i, 0)),
        out_shape=jax.ShapeDtypeStruct((t, d), F32),
        scratch_shapes=[pltpu.VMEM((tm, d), BF16), pltpu.VMEM((tm, d), F32)],
        compiler_params=_params("arbitrary", "arbitrary"),
        name="dense_ffn",
    )(x, w_gate, w_up, w_down, g, b)


def _expert_ffn_kernel(te_ref, nv_ref, x_ref, wg_ref, wu_ref, wd_ref, o_ref, xb_ref, acc_ref):
    del te_ref
    i = pl.program_id(0)
    f = pl.program_id(1)
    last = f == pl.num_programs(1) - 1
    valid = i < nv_ref[0]

    @pl.when(valid)
    def _():
        @pl.when(f == 0)
        def _():
            xb_ref[...] = _unpack_bf16_pairs(x_ref[...]).astype(BF16)

        part = _swiglu_partial(xb_ref[...], wg_ref, wu_ref, wd_ref)

        @pl.when(f == 0)
        def _():
            acc_ref[...] = part

        @pl.when(f > 0)
        def _():
            acc_ref[...] += part

        @pl.when(last)
        def _():
            o_ref[...] = _pack_bf16_pairs(acc_ref[...])

    @pl.when(jnp.logical_and(last, jnp.logical_not(valid)))
    def _():
        o_ref[...] = jnp.zeros(o_ref.shape, U32)


def _expert_ffn(xs, tile_expert, n_valid, w_gate, w_up, w_down, layer, tf):
    d = w_gate.shape[2]
    d_exp = w_gate.shape[3]
    tm = TM_EXP
    n_rows = xs.shape[0] // tm * tm
    nf = d_exp // tf

    def w_col(i, f, te, nv):
        return (layer, te[i], 0, jnp.where(i < nv[0], f, nf - 1))

    def w_row(i, f, te, nv):
        return (layer, te[i], jnp.where(i < nv[0], f, nf - 1), 0)

    return pl.pallas_call(
        _expert_ffn_kernel,
        grid_spec=pltpu.PrefetchScalarGridSpec(
            num_scalar_prefetch=2,
            grid=(n_rows // tm, nf),
            in_specs=[
                pl.BlockSpec((tm, d // 2), lambda i, f, te, nv: (i, 0)),
                pl.BlockSpec((None, None, d, tf), w_col),
                pl.BlockSpec((None, None, d, tf), w_col),
                pl.BlockSpec((None, None, tf, d), w_row),
            ],
            out_specs=pl.BlockSpec((tm, d // 2), lambda i, f, te, nv: (i, 0)),
            scratch_shapes=[pltpu.VMEM((tm, d), BF16), pltpu.VMEM((tm, d), F32)],
        ),
        out_shape=jax.ShapeDtypeStruct((n_rows, d // 2), U32),
        compiler_params=_params("arbitrary", "arbitrary"),
        name="expert_ffn",
    )(tile_expert, n_valid, xs, w_gate, w_up, w_down)


def _split_bf16(v):
    hi = v.astype(BF16)
    lo = (v - hi.astype(F32)).astype(BF16)
    return hi, lo


def _pack_bf16_pairs(v):
    w = v.shape[1] // 2
    lo = lax.bitcast_convert_type(v[:, :w].astype(BF16).astype(F32), U32)
    hi = lax.bitcast_convert_type(v[:, w:].astype(BF16).astype(F32), U32)
    return (lo >> 16) | hi


def _unpack_bf16_pairs(p):
    lo = lax.bitcast_convert_type(p << 16, F32)
    hi = lax.bitcast_convert_type(p & jnp.uint32(0xFFFF0000), F32)
    return jnp.concatenate([lo, hi], axis=1)


def _router_kernel(x_ref, wr_ref, idx_ref, gate_ref, xp_ref):
    xp_ref[...] = _pack_bf16_pairs(x_ref[...])
    xh, xl = _split_bf16(x_ref[...])
    wh, wl = _split_bf16(wr_ref[...])
    logits = (lax.dot_general(wh, xh, _NT, preferred_element_type=F32)
              + lax.dot_general(wh, xl, _NT, preferred_element_type=F32)
              + lax.dot_general(wl, xh, _NT, preferred_element_type=F32))
    e = lax.broadcasted_iota(I32, logits.shape, 0)
    m1 = jnp.max(logits, axis=0, keepdims=True)
    i1 = jnp.min(jnp.where(logits == m1, e, N_EXPERTS), axis=0, keepdims=True)
    rest = jnp.where(e == i1, -jnp.inf, logits)
    m2 = jnp.max(rest, axis=0, keepdims=True)
    i2 = jnp.min(jnp.where(rest == m2, e, N_EXPERTS), axis=0, keepdims=True)
    r = jnp.exp(m2 - m1)
    idx_ref[...] = jnp.concatenate([i1, i2], axis=0)
    gate_ref[...] = jnp.concatenate([1.0 / (1.0 + r), r / (1.0 + r)], axis=0)


def _router(x, w_router_t):
    t, d = x.shape
    tm = TM_MIX
    return pl.pallas_call(
        _router_kernel,
        grid=(t // tm,),
        in_specs=[
            pl.BlockSpec((tm, d), lambda i: (i, 0)),
            pl.BlockSpec((N_EXPERTS, d), lambda i: (0, 0)),
        ],
        out_specs=[
            pl.BlockSpec((TOP_K, tm), lambda i: (0, i)),
            pl.BlockSpec((TOP_K, tm), lambda i: (0, i)),
            pl.BlockSpec((tm, d // 2), lambda i: (i, 0)),
        ],
        out_shape=[jax.ShapeDtypeStruct((TOP_K, t), I32), jax.ShapeDtypeStruct((TOP_K, t), F32),
                   jax.ShapeDtypeStruct((t, d // 2), U32)],
        compiler_params=_params("arbitrary"),
        name="router",
    )(x, w_router_t)


def _dispatch_kernel(ends_ref, nv_ref, pos_ref, x_ref, xs_ref, zero_ref, sem, zero_sem):
    tm = x_ref.shape[0]

    @pl.when(pl.program_id(0) == 0)
    def _():
        zr = zero_ref.shape[0]
        zero_ref[...] = jnp.zeros(zero_ref.shape, F32)

        def zero_tile(tile):
            copies = [pltpu.make_async_copy(zero_ref, xs_ref.at[pl.ds(tile * TM_EXP + q * zr, zr)], zero_sem)
                      for q in range(TM_EXP // zr)]
            for c in copies:
                c.start()
            for c in copies:
                c.wait()

        for e in range(N_EXPERTS):
            zero_tile(jnp.maximum(ends_ref[e] // TM_EXP - 1, 0))
        for u in range(N_EXPERTS):
            @pl.when(nv_ref[0] + u < xs_ref.shape[0] // TM_EXP)
            def _():
                zero_tile(nv_ref[0] + u)

    def issue(j, carry):
        for k in range(TOP_K):
            pltpu.make_async_copy(x_ref.at[pl.ds(j, 1)], xs_ref.at[pl.ds(pos_ref[0, 0, TOP_K * j + k], 1)], sem).start()
        return carry

    lax.fori_loop(0, tm, issue, 0, unroll=4)
    for k in range(TOP_K):
        pltpu.make_async_copy(x_ref, xs_ref.at[pl.ds(0, tm)], sem).wait()


def _dispatch(x, pos, ends, n_valid, n_rows):
    t, d = x.shape
    tm = TM_MIX
    return pl.pallas_call(
        _dispatch_kernel,
        grid_spec=pltpu.PrefetchScalarGridSpec(
            num_scalar_prefetch=2,
            grid=(t // tm,),
            in_specs=[
                pl.BlockSpec((1, 1, TOP_K * tm), lambda i, ends, nv: (i, 0, 0), memory_space=pltpu.SMEM),
                pl.BlockSpec((tm, d), lambda i, ends, nv: (i, 0)),
            ],
            out_specs=pl.BlockSpec(memory_space=pl.ANY),
            scratch_shapes=[pltpu.VMEM((tm, d), F32), pltpu.SemaphoreType.DMA(()), pltpu.SemaphoreType.DMA(())],
        ),
        out_shape=jax.ShapeDtypeStruct((n_rows, d), F32),
        compiler_params=_params("arbitrary"),
        name="moe_dispatch",
    )(ends, n_valid, pos.reshape(t // tm, 1, TOP_K * tm), x)


def _moe_combine_kernel(pos_ref, pos_next_ref, x_ref, gate_ref, g_ref, b_ref, ys_ref, o_ref, ybuf, sem):
    tm, d = x_ref.shape
    i = pl.program_id(0)
    slot = i % 2

    def gather(idx_ref, s):
        def issue(j, carry):
            for k in range(TOP_K):
                pltpu.make_async_copy(ys_ref.at[pl.ds(idx_ref[0, 0, TOP_K * j + k], 1)], ybuf.at[s, k, pl.ds(j, 1)],
                                      sem.at[s]).start()
            return carry

        lax.fori_loop(0, tm, issue, 0, unroll=4)

    @pl.when(i == 0)
    def _():
        gather(pos_ref, 0)

    @pl.when(i + 1 < pl.num_programs(0))
    def _():
        gather(pos_next_ref, 1 - slot)

    for k in range(TOP_K):
        pltpu.make_async_copy(ys_ref.at[pl.ds(0, tm)], ybuf.at[slot, k], sem.at[slot]).wait()
    gates = gate_ref[...]
    ff = jnp.zeros((tm, d), F32)
    for k in range(TOP_K):
        ff = ff + gates[:, k:k + 1] * ybuf[slot, k]
    o_ref[...] = _layer_norm(ALPHA * x_ref[...] + ff, g_ref[...], b_ref[...])


def _moe_combine(x, ys, pos, gates, g, b):
    t, d = x.shape
    tm = TM_MIX
    steps = t // tm
    const = lambda i: (0, 0)
    pos = pos.reshape(steps, 1, TOP_K * tm)
    return pl.pallas_call(
        _moe_combine_kernel,
        grid=(steps,),
        in_specs=[
            pl.BlockSpec((1, 1, TOP_K * tm), lambda i: (i, 0, 0), memory_space=pltpu.SMEM),
            pl.BlockSpec((1, 1, TOP_K * tm), lambda i: (jnp.minimum(i + 1, steps - 1), 0, 0), memory_space=pltpu.SMEM),
            pl.BlockSpec((tm, d), lambda i: (i, 0)),
            pl.BlockSpec((tm, TOP_K), lambda i: (i, 0)),
            pl.BlockSpec((1, d), const),
            pl.BlockSpec((1, d), const),
            pl.BlockSpec(memory_space=pl.ANY),
        ],
        out_specs=pl.BlockSpec((tm, d), lambda i: (i, 0)),
        out_shape=jax.ShapeDtypeStruct((t, d), F32),
        scratch_shapes=[pltpu.VMEM((2, TOP_K, tm, d), F32), pltpu.SemaphoreType.DMA((2,))],
        compiler_params=_params("arbitrary"),
        name="moe_combine",
    )(pos, pos, x, gates, g, b, ys)


def _moe_route(x, w_router):
    t, d = x.shape
    n_assign = TOP_K * t
    idx, gates, x_packed = _router(x, w_router.T)

    e_flat = idx.T.reshape(n_assign)
    onehot = (e_flat[:, None] == jnp.arange(N_EXPERTS, dtype=I32)[None, :]).astype(I32)
    csum = jnp.cumsum(onehot, axis=0)
    rank = jnp.sum(csum * onehot, axis=1) - 1
    counts = csum[-1]
    padded = ((counts + TM_EXP - 1) // TM_EXP) * TM_EXP
    ends = jnp.cumsum(padded)
    pos = (jnp.sum((ends - padded)[None, :] * onehot, axis=1) + rank).astype(I32)
    n_tiles = n_assign // TM_EXP + N_EXPERTS
    tile_ids = jnp.arange(n_tiles, dtype=I32)
    tile_expert = jnp.minimum(
        jnp.sum((tile_ids[:, None] >= (ends // TM_EXP)[None, :]).astype(I32), axis=1), N_EXPERTS - 1).astype(I32)
    n_valid = (ends[-1:] // TM_EXP).astype(I32)

    pos_k = pos.reshape(t, TOP_K)
    n_rows = n_tiles * TM_EXP
    return dict(x_packed=x_packed, pos0=pos_k[:, 0], pos1=pos_k[:, 1], gates=gates.T, tile_expert=tile_expert, n_valid=n_valid,
                unused=_unused_rows(counts, padded, ends, n_rows), n_rows=n_rows)


def _moe_layer(streams, w_router, w_gate, w_up, w_down, layer, g, b):
    routes = [_moe_route(x, w_router) for x in streams]
    xs = [_sc_dispatch(r["x_packed"], r["pos0"], r["pos1"], r["unused"], r["n_rows"]) for r in routes]
    outs = []
    for x, r, x_sorted in zip(streams, routes, xs):
        ys = _expert_ffn(x_sorted, r["tile_expert"], r["n_valid"], w_gate, w_up, w_down, layer, tf=512)
        outs.append((x, r, _sc_gather2(ys, r["pos0"], r["pos1"])))
    return [_moe_sum(x, y0, y1, r["gates"], g, b) for x, r, (y0, y1) in outs]


SC_CORES = 2
SC_SUBCORES = 16
SC_ROWS = 128


def _sc_worker_base(per_worker):
    return (lax.axis_index("subcore") * SC_CORES + lax.axis_index("core")) * per_worker


def _unused_rows(counts, padded, ends, n_rows):
    lane = jnp.arange(SC_ROWS, dtype=I32)
    chunk0 = jnp.arange(TM_EXP // SC_ROWS, dtype=I32) * SC_ROWS
    spare = n_rows + lane
    pad = (ends - padded + counts)[:, None, None] + chunk0[None, :, None] + lane[None, None, :]
    pad = jnp.where(pad < ends[:, None, None], pad, spare[None, None, :])
    tail = ends[-1] + jnp.arange(N_EXPERTS * TM_EXP // SC_ROWS, dtype=I32)[:, None] * SC_ROWS + lane[None, :]
    tail = jnp.where(tail < n_rows, tail, spare[None, :])
    return jnp.concatenate([pad.reshape(-1), tail.reshape(-1)]).astype(I32)


def _sc_dispatch(x, pos0, pos1, unused, n_rows):
    t, d = x.shape
    workers = SC_CORES * SC_SUBCORES
    per_worker = t // workers
    zero_chunks = unused.shape[0] // SC_ROWS // workers
    mesh = plsc.VectorSubcoreMesh(core_axis_name="core", subcore_axis_name="subcore")

    @pl.kernel(out_type=jax.ShapeDtypeStruct((n_rows + SC_ROWS, d), x.dtype), mesh=mesh,
               scratch_types=[pltpu.VMEM((SC_ROWS,), I32), pltpu.VMEM((SC_ROWS,), I32),
                              pltpu.VMEM((SC_ROWS, d), x.dtype)])
    def run(x_hbm, p0_hbm, p1_hbm, unused_hbm, zeros_hbm, o_hbm, i0_v, i1_v, rows_v):
        pltpu.sync_copy(zeros_hbm, rows_v)
        zbase = _sc_worker_base(zero_chunks * SC_ROWS)

        @pl.loop(0, zero_chunks)
        def _(c):
            pltpu.sync_copy(unused_hbm.at[pl.ds(zbase + c * SC_ROWS, SC_ROWS)], i0_v)
            pltpu.sync_copy(rows_v, o_hbm.at[i0_v])

        base = _sc_worker_base(per_worker)

        @pl.loop(0, per_worker // SC_ROWS)
        def _(c):
            off = base + c * SC_ROWS
            pltpu.sync_copy(p0_hbm.at[pl.ds(off, SC_ROWS)], i0_v)
            pltpu.sync_copy(p1_hbm.at[pl.ds(off, SC_ROWS)], i1_v)
            pltpu.sync_copy(x_hbm.at[pl.ds(off, SC_ROWS)], rows_v)
            pltpu.sync_copy(rows_v, o_hbm.at[i0_v])
            pltpu.sync_copy(rows_v, o_hbm.at[i1_v])

    return run(x, pos0, pos1, unused, jnp.zeros((SC_ROWS, d), x.dtype))


def _sc_gather2(ys, pos0, pos1):
    t = pos0.shape[0]
    d = ys.shape[1]
    per_worker = t // (SC_CORES * SC_SUBCORES)
    mesh = plsc.VectorSubcoreMesh(core_axis_name="core", subcore_axis_name="subcore")
    out = jax.ShapeDtypeStruct((t, d), ys.dtype)

    @pl.kernel(out_type=(out, out), mesh=mesh,
               scratch_types=[pltpu.VMEM((SC_ROWS,), I32), pltpu.VMEM((SC_ROWS, d), ys.dtype)])
    def run(y_hbm, p0_hbm, p1_hbm, o0_hbm, o1_hbm, i_v, rows_v):
        base = _sc_worker_base(per_worker)

        @pl.loop(0, per_worker // SC_ROWS)
        def _(c):
            off = base + c * SC_ROWS
            for p_hbm, o_hbm in ((p0_hbm, o0_hbm), (p1_hbm, o1_hbm)):
                pltpu.sync_copy(p_hbm.at[pl.ds(off, SC_ROWS)], i_v)
                pltpu.sync_copy(y_hbm.at[i_v], rows_v)
                pltpu.sync_copy(rows_v, o_hbm.at[pl.ds(off, SC_ROWS)])

    return run(ys, pos0, pos1)


def _moe_sum_kernel(x_ref, y0_ref, y1_ref, gate_ref, g_ref, b_ref, o_ref):
    gates = gate_ref[...]
    ff = gates[:, 0:1] * _unpack_bf16_pairs(y0_ref[...]) + gates[:, 1:2] * _unpack_bf16_pairs(y1_ref[...])
    o_ref[...] = _layer_norm(ALPHA * x_ref[...] + ff, g_ref[...], b_ref[...])


def _moe_sum(x, y0, y1, gates, g, b):
    t, d = x.shape
    tm = TM_MIX
    const = lambda i: (0, 0)
    row = lambda i: (i, 0)
    return pl.pallas_call(
        _moe_sum_kernel,
        grid=(t // tm,),
        in_specs=[pl.BlockSpec((tm, d), row), pl.BlockSpec((tm, d // 2), row), pl.BlockSpec((tm, d // 2), row),
                  pl.BlockSpec((tm, TOP_K), row), pl.BlockSpec((1, d), const), pl.BlockSpec((1, d), const)],
        out_specs=pl.BlockSpec((tm, d), row),
        out_shape=jax.ShapeDtypeStruct((t, d), F32),
        compiler_params=_params("arbitrary"),
        name="moe_sum",
    )(x, y0, y1, gates, g, b)


QBLKS_PER_STEP = 2


def _attn_kernel(q_ref, kp_ref, kc_ref, vp_ref, vc_ref, o_ref, m_ref, l_ref, kcat_ref, vcat_ref):
    n_pairs = q_ref.shape[1] // LANES
    qi = lax.broadcasted_iota(I32, (QBLK, 2 * QBLK), 0)
    ki = lax.broadcasted_iota(I32, (QBLK, 2 * QBLK), 1)
    in_band = ki - QBLK <= qi
    lane = lax.broadcasted_iota(I32, (QBLK, LANES), 1)
    low_half = lane < HEAD_DIM
    kcat_ref[0:QBLK, :] = kp_ref[...]
    kcat_ref[QBLK:, :] = kc_ref[...]
    for pr in range(n_pairs):
        vcat_ref[0:QBLK, 2 * pr * LANES:(2 * pr + 1) * LANES] = vp_ref[:, pr * LANES:(pr + 1) * LANES]
        vcat_ref[QBLK:, 2 * pr * LANES:(2 * pr + 1) * LANES] = vc_ref[:, pr * LANES:(pr + 1) * LANES]
        vcat_ref[:, (2 * pr + 1) * LANES:(2 * pr + 2) * LANES] = jnp.ones((vcat_ref.shape[0], LANES), BF16)
    m_ref[...] = jnp.zeros(m_ref.shape, F32)
    l_ref[...] = jnp.ones(l_ref.shape, F32)
    for j in range(QBLKS_PER_STEP):
        rows = slice(j * QBLK, (j + 1) * QBLK)
        keys = slice(j * QBLK, (j + 2) * QBLK)
        first_ok = qi + jnp.where(pl.program_id(2) > 0, 0, QBLK) if j == 0 else qi
        mask = jnp.logical_and(jnp.logical_or(ki >= first_ok, ki >= QBLK), in_band)
        for pr in range(n_pairs):
            cols = slice(pr * LANES, (pr + 1) * LANES)
            q2 = q_ref[rows, cols]
            k2 = kcat_ref[keys, cols]
            v_ext = vcat_ref[keys, 2 * pr * LANES:(2 * pr + 2) * LANES]
            o_pair = None
            for a in range(2):
                own = low_half if a == 0 else jnp.logical_not(low_half)
                qm = jnp.where(own, q2, jnp.zeros_like(q2))
                s = lax.dot_general(qm, k2, _NT, preferred_element_type=F32)
                s = jnp.where(mask, s, NEG_BIG)
                m = jnp.max(s, axis=-1, keepdims=True)
                p = jnp.exp(s - m).astype(BF16)
                oe = jnp.dot(p, v_ext, preferred_element_type=F32)
                o_pair = oe[:, :LANES] if a == 0 else jnp.where(low_half, o_pair, oe[:, :LANES])
                h = 2 * pr + a
                m_ref[rows, h:h + 1] = m
                l_ref[rows, h:h + 1] = oe[:, LANES + h:LANES + h + 1]
            o_ref[rows, cols] = o_pair


def _group_attention(q, k, v):
    bsz, dil, length, width = q.shape
    step = QBLKS_PER_STEP * QBLK
    blk = (None, None, step, width)
    cur = lambda b, r, n: (b, r, n, 0)
    prev_blk = (None, None, QBLK, width)
    prev = lambda b, r, n: (b, r, jnp.maximum(QBLKS_PER_STEP * n - 1, 0), 0)
    stat_spec = pl.BlockSpec((None, None, step, LANES), cur)
    stat_shape = jax.ShapeDtypeStruct((bsz, dil, length, LANES), F32)
    return pl.pallas_call(
        _attn_kernel,
        grid=(bsz, dil, length // step),
        in_specs=[
            pl.BlockSpec(blk, cur),
            pl.BlockSpec(prev_blk, prev),
            pl.BlockSpec(blk, cur),
            pl.BlockSpec(prev_blk, prev),
            pl.BlockSpec(blk, cur),
        ],
        out_specs=[pl.BlockSpec(blk, cur), stat_spec, stat_spec],
        out_shape=[jax.ShapeDtypeStruct((bsz, dil, length, width), F32), stat_shape, stat_shape],
        scratch_shapes=[pltpu.VMEM((step + QBLK, width), BF16), pltpu.VMEM((step + QBLK, 2 * width), BF16)],
        compiler_params=_params("arbitrary", "arbitrary", "arbitrary"),
        name="group_attention",
    )(q, k, k, v, v)


def _to_token_order(src_ref, dst_ref):
    dil, n, w = src_ref.shape
    for r in range(dil):
        for c in range(w // LANES):
            dst_ref[c, pl.ds(r, n, stride=dil), :] = src_ref[r, :, c * LANES:(c + 1) * LANES]
    return jnp.concatenate([dst_ref[c] for c in range(w // LANES)], axis=1)


def _attn_out_kernel(x_ref, o0_ref, o1_ref, o2_ref, m0_ref, m1_ref, m2_ref, l0_ref, l1_ref, l2_ref,
                     wo_ref, g_ref, b_ref, out_ref, o_tok, stat_tok):
    width = o0_ref.shape[-1]

    def token_order(ref, scratch):
        if ref.shape[0] == 1:
            return ref[0]
        return _to_token_order(ref, scratch)

    ms = [token_order(r, stat_tok.at[i]) for i, r in enumerate((m0_ref, m1_ref, m2_ref))]
    ls = [token_order(r, stat_tok.at[N_GROUPS + i]) for i, r in enumerate((l0_ref, l1_ref, l2_ref))]
    top = jnp.maximum(jnp.maximum(ms[0], ms[1]), ms[2])
    es = [jnp.exp(m - top) for m in ms]
    den = es[0] * ls[0] + es[1] * ls[1] + es[2] * ls[2]
    head = lax.broadcasted_iota(I32, (LANES, width), 0)
    lane = lax.broadcasted_iota(I32, (LANES, width), 1)
    spread = (lane // HEAD_DIM == head).astype(BF16)
    mixed = jnp.zeros(out_ref.shape[:1] + (width,), F32)
    for e, o_ref in zip(es, (o0_ref, o1_ref, o2_ref)):
        hi, lo = _split_bf16(e / den)
        wide = (jnp.dot(hi, spread, preferred_element_type=F32)
                + jnp.dot(lo, spread, preferred_element_type=F32))
        mixed = mixed + wide * token_order(o_ref, o_tok)
    y = jnp.dot(mixed.astype(BF16), wo_ref[...], preferred_element_type=F32)
    out_ref[...] = _layer_norm(ALPHA * x_ref[...] + y, g_ref[...], b_ref[...])


def _attn_out(x, bsz, outs, stats, w_o, g, b):
    t, d = x.shape
    width = w_o.shape[0]
    tm = TM_RES
    tiles_per_seq = t // bsz // tm
    const = lambda bi, c: (0, 0)
    row = lambda bi, c: (bi * tiles_per_seq + c, 0)

    def res_spec(a):
        dil, w = a.shape[1], a.shape[3]
        return pl.BlockSpec((None, dil, tm // dil, w), lambda bi, c: (bi, 0, c, 0))

    return pl.pallas_call(
        _attn_out_kernel,
        grid=(bsz, tiles_per_seq),
        in_specs=([pl.BlockSpec((tm, d), row)] + [res_spec(a) for a in outs] + [res_spec(a) for a in stats]
                  + [pl.BlockSpec((width, d), const), pl.BlockSpec((1, d), const), pl.BlockSpec((1, d), const)]),
        out_specs=pl.BlockSpec((tm, d), row),
        out_shape=jax.ShapeDtypeStruct((t, d), F32),
        scratch_shapes=[pltpu.VMEM((width // LANES, tm, LANES), F32), pltpu.VMEM((2 * N_GROUPS, 1, tm, LANES), F32)],
        compiler_params=_params("arbitrary", "arbitrary"),
        name="attn_out",
    )(x, *outs, *stats, w_o, g, b)


def _proj_residue_kernel(x_ref, w_ref, *rest, dils, scale):
    outs, y_ref = rest[:-1], rest[-1]
    n_slabs, tm, _ = y_ref.shape
    width = n_slabs * LANES
    xb = x_ref[...].astype(BF16)
    for c, (o_ref, dil) in enumerate(zip(outs, dils)):
        y = jnp.dot(xb, w_ref[:, c * width:(c + 1) * width], preferred_element_type=F32)
        if scale != 1.0:
            y = y * scale
        if dil == 1:
            o_ref[0] = y.astype(o_ref.dtype)
        else:
            for s in range(n_slabs):
                y_ref[s] = y[:, s * LANES:(s + 1) * LANES]
            for r in range(dil):
                rows = [y_ref[s, pl.ds(r, tm // dil, stride=dil), :] for s in range(n_slabs)]
                o_ref[r] = jnp.concatenate(rows, axis=1).astype(o_ref.dtype)


def _proj_residue(x, bsz, w, dils, scale=1.0):
    t, d = x.shape
    seq = t // bsz
    width = w.shape[1] // len(dils)
    tm = TM_RES
    tiles_per_seq = seq // tm
    return pl.pallas_call(
        functools.partial(_proj_residue_kernel, dils=dils, scale=scale),
        grid=(bsz, tiles_per_seq),
        in_specs=[pl.BlockSpec((tm, d), lambda bi, c: (bi * tiles_per_seq + c, 0)),
                  pl.BlockSpec(w.shape, lambda bi, c: (0, 0))],
        out_specs=[pl.BlockSpec((None, dil, tm // dil, width), lambda bi, c: (bi, 0, c, 0)) for dil in dils],
        out_shape=[jax.ShapeDtypeStruct((bsz, dil, seq // dil, width), BF16) for dil in dils],
        scratch_shapes=[pltpu.VMEM((width // LANES, tm, LANES), F32)],
        compiler_params=_params("arbitrary", "arbitrary"),
        name="proj_residue",
    )(x, w)


def _attention_layer(x, bsz, w_q, w_o, kv_groups, g, b):
    dils = tuple(dil for _, dil in ATT_GROUPS)
    qs = _proj_residue(x, bsz, w_q, dils, scale=HEAD_DIM ** -0.5)
    outs, maxes, dens = [], [], []
    for q, (k_res, v_res) in zip(qs, kv_groups):
        o, m, l = _group_attention(q, k_res, v_res)
        outs.append(o)
        maxes.append(m)
        dens.append(l)
    return _attn_out(x, bsz, outs, maxes + dens, w_o, g, b)


def kernel(x, a_w_in, a_conv, a_w_out, kv_w, b_w_q, b_w_o, ffn_w_gate, ffn_w_up, ffn_w_down,
           moe_w_router, moe_w_gate, moe_w_up, moe_w_down, ln_g, ln_b):
    bsz, seq, d = x.shape
    sb = bsz // N_STREAMS
    hs = [x[s * sb:(s + 1) * sb].reshape(sb * seq, d) for s in range(N_STREAMS)]
    ln_g = ln_g.reshape(DEPTH, 2, 1, d)
    ln_b = ln_b.reshape(DEPTH, 2, 1, d)
    kv_dils = tuple(dil for _, dil in ATT_GROUPS for _ in range(2))
    kv_groups = None
    for l in range(DEPTH):
        if l < N_A_LAYERS:
            w_in, w_out = a_w_in[l].astype(BF16), a_w_out[l].astype(BF16)
            hs = [_a_mixer(h, w_in, a_conv[l], w_out, ln_g[l, 0], ln_b[l, 0], seq) for h in hs]
        else:
            if kv_groups is None:
                kv_wb = kv_w.astype(BF16)
                kvs = [_proj_residue(h, sb, kv_wb, kv_dils) for h in hs]
                kv_groups = [[(kv[2 * gi], kv[2 * gi + 1]) for gi in range(N_GROUPS)] for kv in kvs]
            j = l - N_A_LAYERS
            w_q, w_o = b_w_q[j].astype(BF16), b_w_o[j].astype(BF16)
            hs = [_attention_layer(h, sb, w_q, w_o, kvg, ln_g[l, 0], ln_b[l, 0]) for h, kvg in zip(hs, kv_groups)]
        i = l // 2
        if l % 2 == 0:
            w_gate, w_up, w_down = (w[i].astype(BF16) for w in (ffn_w_gate, ffn_w_up, ffn_w_down))
            hs = [_dense_ffn(h, w_gate, w_up, w_down, ln_g[l, 1], ln_b[l, 1], tf=1408) for h in hs]
        else:
            hs = _moe_layer(hs, moe_w_router[i], moe_w_gate.astype(BF16), moe_w_up.astype(BF16),
                            moe_w_down.astype(BF16), i, ln_g[l, 1], ln_b[l, 1])
    return jnp.concatenate(hs, axis=0).reshape(bsz, seq, d)
```

```python
import functools

import jax
import jax.numpy as jnp
from jax import lax
from jax.experimental import pallas as pl
from jax.experimental.pallas import tpu as pltpu
from jax.experimental.pallas import tpu_sc as plsc

F32 = jnp.float32
U32 = jnp.uint32
BF16 = jnp.bfloat16
I32 = jnp.int32

DEPTH = 4
N_A_LAYERS = DEPTH // 2
CONV_WIDTH = 3
ATT_GROUPS = ((128, 1), (512, 4), (2048, 16))
N_GROUPS = len(ATT_GROUPS)
HEAD_DIM = 64
N_EXPERTS = 8
TOP_K = 2
ALPHA = (2.0 * DEPTH) ** 0.25
LN_EPS = 1e-5

QBLK = 128
assert all(w // d == QBLK for w, d in ATT_GROUPS)

LANES = 128
SUBLANES = 8
VMEM_LIMIT = 56 * 1024 * 1024
NEG_BIG = -1e30

TM_MIX = 512
TM_EXP = 512
TM_RES = 512
N_STREAMS = 1

_NT = (((1,), (1,)), ((), ()))


def _layer_norm(z, g, b):
    mu = jnp.mean(z, axis=-1, keepdims=True)
    zc = z - mu
    var = jnp.mean(zc * zc, axis=-1, keepdims=True)
    return zc * lax.rsqrt(var + LN_EPS) * g + b


def _params(*semantics):
    return pltpu.CompilerParams(dimension_semantics=semantics, vmem_limit_bytes=VMEM_LIMIT)


def _a_mixer_kernel(x_ref, win_ref, conv_ref, wout_ref, g_ref, b_ref, o_ref, ubuf, *, tiles_per_seq):
    tm, d = x_ref.shape

    @pl.when(pl.program_id(0) % tiles_per_seq == 0)
    def _():
        ubuf[0:SUBLANES, :] = jnp.zeros((SUBLANES, d), F32)

    x = x_ref[...]
    p = jnp.dot(x.astype(BF16), win_ref[...], preferred_element_type=F32)
    ubuf[SUBLANES:SUBLANES + tm, :] = p[:, d:2 * d] * p[:, 2 * d:]
    cw = conv_ref[...]
    conv = (cw[2:3, :] * ubuf[SUBLANES:SUBLANES + tm, :]
            + cw[1:2, :] * ubuf[SUBLANES - 1:SUBLANES - 1 + tm, :]
            + cw[0:1, :] * ubuf[SUBLANES - 2:SUBLANES - 2 + tm, :])
    y = jnp.dot((p[:, :d] * conv).astype(BF16), wout_ref[...], preferred_element_type=F32)
    o_ref[...] = _layer_norm(ALPHA * x + y, g_ref[...], b_ref[...])
    ubuf[0:SUBLANES, :] = ubuf[tm:tm + SUBLANES, :]


def _a_mixer(x, w_in, conv_w, w_out, g, b, seq_len):
    t, d = x.shape
    tm = TM_MIX
    const = lambda i: (0, 0)
    return pl.pallas_call(
        functools.partial(_a_mixer_kernel, tiles_per_seq=seq_len // tm),
        grid=(t // tm,),
        in_specs=[
            pl.BlockSpec((tm, d), lambda i: (i, 0)),
            pl.BlockSpec((d, 3 * d), const),
            pl.BlockSpec((CONV_WIDTH, d), const),
            pl.BlockSpec((d, d), const),
            pl.BlockSpec((1, d), const),
            pl.BlockSpec((1, d), const),
        ],
        out_specs=pl.BlockSpec((tm, d), lambda i: (i, 0)),
        out_shape=jax.ShapeDtypeStruct((t, d), F32),
        scratch_shapes=[pltpu.VMEM((tm + SUBLANES, d), F32)],
        compiler_params=_params("arbitrary"),
        name="a_mixer",
    )(x, w_in, conv_w, w_out, g, b)


def _swiglu_partial(xb, wg_ref, wu_ref, wd_ref):
    gate = jnp.dot(xb, wg_ref[...], preferred_element_type=F32)
    up = jnp.dot(xb, wu_ref[...], preferred_element_type=F32)
    h = (gate * jax.nn.sigmoid(gate) * up).astype(BF16)
    return jnp.dot(h, wd_ref[...], preferred_element_type=F32)


def _dense_ffn_kernel(x_ref, wg_ref, wu_ref, wd_ref, g_ref, b_ref, o_ref):
    x = x_ref[...]
    y = _swiglu_partial(x.astype(BF16), wg_ref, wu_ref, wd_ref)
    o_ref[...] = _layer_norm(ALPHA * x + y, g_ref[...], b_ref[...])


def _dense_ffn(x, w_gate, w_up, w_down, g, b):
    t, d = x.shape
    d_ff = w_gate.shape[1]
    tm = TM_MIX
    const = lambda i: (0, 0)
    resident = dict(pipeline_mode=pl.Buffered(1))
    return pl.pallas_call(
        _dense_ffn_kernel,
        grid=(t // tm,),
        in_specs=[
            pl.BlockSpec((tm, d), lambda i: (i, 0)),
            pl.BlockSpec((d, d_ff), const, **resident),
            pl.BlockSpec((d, d_ff), const, **resident),
            pl.BlockSpec((d_ff, d), const, **resident),
            pl.BlockSpec((1, d), const),
            pl.BlockSpec((1, d), const),
        ],
        out_specs=pl.BlockSpec((tm, d), lambda i: (i, 0)),
        out_shape=jax.ShapeDtypeStruct((t, d), F32),
        compiler_params=_params("arbitrary"),
        name="dense_ffn",
    )(x, w_gate, w_up, w_down, g, b)


EXPERT_COL_SPLITS = 2


def _expert_ffn_kernel(te_ref, nv_ref, x_ref, wg_ref, wu_ref, wd_ref, o_ref):
    del te_ref
    valid = pl.program_id(0) < nv_ref[0]

    @pl.when(valid)
    def _():
        xb = _unpack_bf16_pairs(x_ref[...]).astype(BF16)
        piece = wg_ref.shape[1] // EXPERT_COL_SPLITS
        y = None
        for c in range(EXPERT_COL_SPLITS):
            cols = slice(c * piece, (c + 1) * piece)
            gate = jnp.dot(xb, wg_ref[:, cols], preferred_element_type=F32)
            up = jnp.dot(xb, wu_ref[:, cols], preferred_element_type=F32)
            h = (gate * jax.nn.sigmoid(gate) * up).astype(BF16)
            part = jnp.dot(h, wd_ref[cols, :], preferred_element_type=F32)
            y = part if y is None else y + part
        o_ref[...] = _pack_bf16_pairs(y)

    @pl.when(jnp.logical_not(valid))
    def _():
        o_ref[...] = jnp.zeros(o_ref.shape, U32)


def _expert_ffn(xs, tile_expert, n_valid, w_gate, w_up, w_down, layer):
    d = w_gate.shape[2]
    d_exp = w_gate.shape[3]
    tm = TM_EXP
    n_rows = xs.shape[0] // tm * tm
    weights_of_tile = lambda i, te, nv: (layer, te[i], 0, 0)
    resident = dict(pipeline_mode=pl.Buffered(1))
    return pl.pallas_call(
        _expert_ffn_kernel,
        grid_spec=pltpu.PrefetchScalarGridSpec(
            num_scalar_prefetch=2,
            grid=(n_rows // tm,),
            in_specs=[
                pl.BlockSpec((tm, d // 2), lambda i, te, nv: (i, 0)),
                pl.BlockSpec((None, None, d, d_exp), weights_of_tile, **resident),
                pl.BlockSpec((None, None, d, d_exp), weights_of_tile, **resident),
                pl.BlockSpec((None, None, d_exp, d), weights_of_tile, **resident),
            ],
            out_specs=pl.BlockSpec((tm, d // 2), lambda i, te, nv: (i, 0)),
        ),
        out_shape=jax.ShapeDtypeStruct((n_rows, d // 2), U32),
        compiler_params=_params("arbitrary"),
        name="expert_ffn",
    )(tile_expert, n_valid, xs, w_gate, w_up, w_down)


def _split_bf16(v):
    hi = v.astype(BF16)
    lo = (v - hi.astype(F32)).astype(BF16)
    return hi, lo


def _pack_bf16_pairs(v):
    w = v.shape[1] // 2
    lo = lax.bitcast_convert_type(v[:, :w].astype(BF16).astype(F32), U32)
    hi = lax.bitcast_convert_type(v[:, w:].astype(BF16).astype(F32), U32)
    return (lo >> 16) | hi


def _unpack_bf16_pairs(p):
    lo = lax.bitcast_convert_type(p << 16, F32)
    hi = lax.bitcast_convert_type(p & jnp.uint32(0xFFFF0000), F32)
    return jnp.concatenate([lo, hi], axis=1)


def _router_kernel(x_ref, wr_ref, idx_ref, gate_ref, xp_ref):
    xp_ref[...] = _pack_bf16_pairs(x_ref[...])
    xh, xl = _split_bf16(x_ref[...])
    wh, wl = _split_bf16(wr_ref[...])
    logits = (lax.dot_general(wh, xh, _NT, preferred_element_type=F32)
              + lax.dot_general(wh, xl, _NT, preferred_element_type=F32)
              + lax.dot_general(wl, xh, _NT, preferred_element_type=F32))
    e = lax.broadcasted_iota(I32, logits.shape, 0)
    m1 = jnp.max(logits, axis=0, keepdims=True)
    i1 = jnp.min(jnp.where(logits == m1, e, N_EXPERTS), axis=0, keepdims=True)
    rest = jnp.where(e == i1, -jnp.inf, logits)
    m2 = jnp.max(rest, axis=0, keepdims=True)
    i2 = jnp.min(jnp.where(rest == m2, e, N_EXPERTS), axis=0, keepdims=True)
    r = jnp.exp(m2 - m1)
    idx_ref[...] = jnp.concatenate([i1, i2], axis=0)
    gate_ref[...] = jnp.concatenate([1.0 / (1.0 + r), r / (1.0 + r)], axis=0)


def _router(x, w_router_t):
    t, d = x.shape
    tm = TM_MIX
    return pl.pallas_call(
        _router_kernel,
        grid=(t // tm,),
        in_specs=[
            pl.BlockSpec((tm, d), lambda i: (i, 0)),
            pl.BlockSpec((N_EXPERTS, d), lambda i: (0, 0)),
        ],
        out_specs=[
            pl.BlockSpec((TOP_K, tm), lambda i: (0, i)),
            pl.BlockSpec((TOP_K, tm), lambda i: (0, i)),
            pl.BlockSpec((tm, d // 2), lambda i: (i, 0)),
        ],
        out_shape=[jax.ShapeDtypeStruct((TOP_K, t), I32), jax.ShapeDtypeStruct((TOP_K, t), F32),
                   jax.ShapeDtypeStruct((t, d // 2), U32)],
        compiler_params=_params("arbitrary"),
        name="router",
    )(x, w_router_t)


def _dispatch_kernel(ends_ref, nv_ref, pos_ref, x_ref, xs_ref, zero_ref, sem, zero_sem):
    tm = x_ref.shape[0]

    @pl.when(pl.program_id(0) == 0)
    def _():
        zr = zero_ref.shape[0]
        zero_ref[...] = jnp.zeros(zero_ref.shape, F32)

        def zero_tile(tile):
            copies = [pltpu.make_async_copy(zero_ref, xs_ref.at[pl.ds(tile * TM_EXP + q * zr, zr)], zero_sem)
                      for q in range(TM_EXP // zr)]
            for c in copies:
                c.start()
            for c in copies:
                c.wait()

        for e in range(N_EXPERTS):
            zero_tile(jnp.maximum(ends_ref[e] // TM_EXP - 1, 0))
        for u in range(N_EXPERTS):
            @pl.when(nv_ref[0] + u < xs_ref.shape[0] // TM_EXP)
            def _():
                zero_tile(nv_ref[0] + u)

    def issue(j, carry):
        for k in range(TOP_K):
            pltpu.make_async_copy(x_ref.at[pl.ds(j, 1)], xs_ref.at[pl.ds(pos_ref[0, 0, TOP_K * j + k], 1)], sem).start()
        return carry

    lax.fori_loop(0, tm, issue, 0, unroll=4)
    for k in range(TOP_K):
        pltpu.make_async_copy(x_ref, xs_ref.at[pl.ds(0, tm)], sem).wait()


def _dispatch(x, pos, ends, n_valid, n_rows):
    t, d = x.shape
    tm = TM_MIX
    return pl.pallas_call(
        _dispatch_kernel,
        grid_spec=pltpu.PrefetchScalarGridSpec(
            num_scalar_prefetch=2,
            grid=(t // tm,),
            in_specs=[
                pl.BlockSpec((1, 1, TOP_K * tm), lambda i, ends, nv: (i, 0, 0), memory_space=pltpu.SMEM),
                pl.BlockSpec((tm, d), lambda i, ends, nv: (i, 0)),
            ],
            out_specs=pl.BlockSpec(memory_space=pl.ANY),
            scratch_shapes=[pltpu.VMEM((tm, d), F32), pltpu.SemaphoreType.DMA(()), pltpu.SemaphoreType.DMA(())],
        ),
        out_shape=jax.ShapeDtypeStruct((n_rows, d), F32),
        compiler_params=_params("arbitrary"),
        name="moe_dispatch",
    )(ends, n_valid, pos.reshape(t // tm, 1, TOP_K * tm), x)


def _moe_combine_kernel(pos_ref, pos_next_ref, x_ref, gate_ref, g_ref, b_ref, ys_ref, o_ref, ybuf, sem):
    tm, d = x_ref.shape
    i = pl.program_id(0)
    slot = i % 2

    def gather(idx_ref, s):
        def issue(j, carry):
            for k in range(TOP_K):
                pltpu.make_async_copy(ys_ref.at[pl.ds(idx_ref[0, 0, TOP_K * j + k], 1)], ybuf.at[s, k, pl.ds(j, 1)],
                                      sem.at[s]).start()
            return carry

        lax.fori_loop(0, tm, issue, 0, unroll=4)

    @pl.when(i == 0)
    def _():
        gather(pos_ref, 0)

    @pl.when(i + 1 < pl.num_programs(0))
    def _():
        gather(pos_next_ref, 1 - slot)

    for k in range(TOP_K):
        pltpu.make_async_copy(ys_ref.at[pl.ds(0, tm)], ybuf.at[slot, k], sem.at[slot]).wait()
    gates = gate_ref[...]
    ff = jnp.zeros((tm, d), F32)
    for k in range(TOP_K):
        ff = ff + gates[:, k:k + 1] * ybuf[slot, k]
    o_ref[...] = _layer_norm(ALPHA * x_ref[...] + ff, g_ref[...], b_ref[...])


def _moe_combine(x, ys, pos, gates, g, b):
    t, d = x.shape
    tm = TM_MIX
    steps = t // tm
    const = lambda i: (0, 0)
    pos = pos.reshape(steps, 1, TOP_K * tm)
    return pl.pallas_call(
        _moe_combine_kernel,
        grid=(steps,),
        in_specs=[
            pl.BlockSpec((1, 1, TOP_K * tm), lambda i: (i, 0, 0), memory_space=pltpu.SMEM),
            pl.BlockSpec((1, 1, TOP_K * tm), lambda i: (jnp.minimum(i + 1, steps - 1), 0, 0), memory_space=pltpu.SMEM),
            pl.BlockSpec((tm, d), lambda i: (i, 0)),
            pl.BlockSpec((tm, TOP_K), lambda i: (i, 0)),
            pl.BlockSpec((1, d), const),
            pl.BlockSpec((1, d), const),
            pl.BlockSpec(memory_space=pl.ANY),
        ],
        out_specs=pl.BlockSpec((tm, d), lambda i: (i, 0)),
        out_shape=jax.ShapeDtypeStruct((t, d), F32),
        scratch_shapes=[pltpu.VMEM((2, TOP_K, tm, d), F32), pltpu.SemaphoreType.DMA((2,))],
        compiler_params=_params("arbitrary"),
        name="moe_combine",
    )(pos, pos, x, gates, g, b, ys)


def _moe_route(x, w_router):
    t, d = x.shape
    n_assign = TOP_K * t
    idx, gates, x_packed = _router(x, w_router.T)

    e_flat = idx.T.reshape(n_assign)
    onehot = (e_flat[:, None] == jnp.arange(N_EXPERTS, dtype=I32)[None, :]).astype(I32)
    csum = jnp.cumsum(onehot, axis=0)
    rank = jnp.sum(csum * onehot, axis=1) - 1
    counts = csum[-1]
    padded = ((counts + TM_EXP - 1) // TM_EXP) * TM_EXP
    ends = jnp.cumsum(padded)
    pos = (jnp.sum((ends - padded)[None, :] * onehot, axis=1) + rank).astype(I32)
    n_tiles = n_assign // TM_EXP + N_EXPERTS
    tile_ids = jnp.arange(n_tiles, dtype=I32)
    tile_expert = jnp.minimum(
        jnp.sum((tile_ids[:, None] >= (ends // TM_EXP)[None, :]).astype(I32), axis=1), N_EXPERTS - 1).astype(I32)
    n_valid = (ends[-1:] // TM_EXP).astype(I32)

    pos_k = pos.reshape(t, TOP_K)
    n_rows = n_tiles * TM_EXP
    return dict(x_packed=x_packed, pos0=pos_k[:, 0], pos1=pos_k[:, 1], gates=gates.T, tile_expert=tile_expert, n_valid=n_valid,
                unused=_unused_rows(counts, padded, ends, n_rows), n_rows=n_rows)


def _moe_layer(streams, w_router, w_gate, w_up, w_down, layer, g, b):
    routes = [_moe_route(x, w_router) for x in streams]
    xs = [_sc_dispatch(r["x_packed"], r["pos0"], r["pos1"], r["unused"], r["n_rows"]) for r in routes]
    outs = []
    for x, r, x_sorted in zip(streams, routes, xs):
        ys = _expert_ffn(x_sorted, r["tile_expert"], r["n_valid"], w_gate, w_up, w_down, layer)
        outs.append((x, r, _sc_gather2(ys, r["pos0"], r["pos1"])))
    return [_moe_sum(x, y0, y1, r["gates"], g, b) for x, r, (y0, y1) in outs]


SC_CORES = 2
SC_SUBCORES = 16
SC_ROWS = 128


def _sc_worker_base(per_worker):
    return (lax.axis_index("subcore") * SC_CORES + lax.axis_index("core")) * per_worker


def _unused_rows(counts, padded, ends, n_rows):
    lane = jnp.arange(SC_ROWS, dtype=I32)
    chunk0 = jnp.arange(TM_EXP // SC_ROWS, dtype=I32) * SC_ROWS
    spare = n_rows + lane
    pad = (ends - padded + counts)[:, None, None] + chunk0[None, :, None] + lane[None, None, :]
    pad = jnp.where(pad < ends[:, None, None], pad, spare[None, None, :])
    tail = ends[-1] + jnp.arange(N_EXPERTS * TM_EXP // SC_ROWS, dtype=I32)[:, None] * SC_ROWS + lane[None, :]
    tail = jnp.where(tail < n_rows, tail, spare[None, :])
    return jnp.concatenate([pad.reshape(-1), tail.reshape(-1)]).astype(I32)


def _sc_dispatch(x, pos0, pos1, unused, n_rows):
    t, d = x.shape
    workers = SC_CORES * SC_SUBCORES
    per_worker = t // workers
    zero_chunks = unused.shape[0] // SC_ROWS // workers
    mesh = plsc.VectorSubcoreMesh(core_axis_name="core", subcore_axis_name="subcore")

    @pl.kernel(out_type=jax.ShapeDtypeStruct((n_rows + SC_ROWS, d), x.dtype), mesh=mesh,
               scratch_types=[pltpu.VMEM((SC_ROWS,), I32), pltpu.VMEM((SC_ROWS,), I32),
                              pltpu.VMEM((SC_ROWS, d), x.dtype)])
    def run(x_hbm, p0_hbm, p1_hbm, unused_hbm, zeros_hbm, o_hbm, i0_v, i1_v, rows_v):
        pltpu.sync_copy(zeros_hbm, rows_v)
        zbase = _sc_worker_base(zero_chunks * SC_ROWS)

        @pl.loop(0, zero_chunks)
        def _(c):
            pltpu.sync_copy(unused_hbm.at[pl.ds(zbase + c * SC_ROWS, SC_ROWS)], i0_v)
            pltpu.sync_copy(rows_v, o_hbm.at[i0_v])

        base = _sc_worker_base(per_worker)

        @pl.loop(0, per_worker // SC_ROWS)
        def _(c):
            off = base + c * SC_ROWS
            pltpu.sync_copy(p0_hbm.at[pl.ds(off, SC_ROWS)], i0_v)
            pltpu.sync_copy(p1_hbm.at[pl.ds(off, SC_ROWS)], i1_v)
            pltpu.sync_copy(x_hbm.at[pl.ds(off, SC_ROWS)], rows_v)
            pltpu.sync_copy(rows_v, o_hbm.at[i0_v])
            pltpu.sync_copy(rows_v, o_hbm.at[i1_v])

    return run(x, pos0, pos1, unused, jnp.zeros((SC_ROWS, d), x.dtype))


def _sc_gather2(ys, pos0, pos1):
    t = pos0.shape[0]
    d = ys.shape[1]
    per_worker = t // (SC_CORES * SC_SUBCORES)
    mesh = plsc.VectorSubcoreMesh(core_axis_name="core", subcore_axis_name="subcore")
    out = jax.ShapeDtypeStruct((t, d), ys.dtype)

    @pl.kernel(out_type=(out, out), mesh=mesh,
               scratch_types=[pltpu.VMEM((SC_ROWS,), I32), pltpu.VMEM((SC_ROWS, d), ys.dtype)])
    def run(y_hbm, p0_hbm, p1_hbm, o0_hbm, o1_hbm, i_v, rows_v):
        base = _sc_worker_base(per_worker)

        @pl.loop(0, per_worker // SC_ROWS)
        def _(c):
            off = base + c * SC_ROWS
            for p_hbm, o_hbm in ((p0_hbm, o0_hbm), (p1_hbm, o1_hbm)):
                pltpu.sync_copy(p_hbm.at[pl.ds(off, SC_ROWS)], i_v)
                pltpu.sync_copy(y_hbm.at[i_v], rows_v)
                pltpu.sync_copy(rows_v, o_hbm.at[pl.ds(off, SC_ROWS)])

    return run(ys, pos0, pos1)


def _moe_sum_kernel(x_ref, y0_ref, y1_ref, gate_ref, g_ref, b_ref, o_ref):
    gates = gate_ref[...]
    ff = gates[:, 0:1] * _unpack_bf16_pairs(y0_ref[...]) + gates[:, 1:2] * _unpack_bf16_pairs(y1_ref[...])
    o_ref[...] = _layer_norm(ALPHA * x_ref[...] + ff, g_ref[...], b_ref[...])


def _moe_sum(x, y0, y1, gates, g, b):
    t, d = x.shape
    tm = TM_MIX
    const = lambda i: (0, 0)
    row = lambda i: (i, 0)
    return pl.pallas_call(
        _moe_sum_kernel,
        grid=(t // tm,),
        in_specs=[pl.BlockSpec((tm, d), row), pl.BlockSpec((tm, d // 2), row), pl.BlockSpec((tm, d // 2), row),
                  pl.BlockSpec((tm, TOP_K), row), pl.BlockSpec((1, d), const), pl.BlockSpec((1, d), const)],
        out_specs=pl.BlockSpec((tm, d), row),
        out_shape=jax.ShapeDtypeStruct((t, d), F32),
        compiler_params=_params("arbitrary"),
        name="moe_sum",
    )(x, y0, y1, gates, g, b)


QBLKS_PER_STEP = 2


def _attn_kernel(q_ref, kp_ref, kc_ref, vp_ref, vc_ref, o_ref, m_ref, l_ref, kcat_ref, vcat_ref):
    n_pairs = q_ref.shape[1] // LANES
    qi = lax.broadcasted_iota(I32, (QBLK, 2 * QBLK), 0)
    ki = lax.broadcasted_iota(I32, (QBLK, 2 * QBLK), 1)
    in_band = ki - QBLK <= qi
    lane = lax.broadcasted_iota(I32, (QBLK, LANES), 1)
    low_half = lane < HEAD_DIM
    kcat_ref[0:QBLK, :] = kp_ref[...]
    kcat_ref[QBLK:, :] = kc_ref[...]
    for pr in range(n_pairs):
        vcat_ref[0:QBLK, 2 * pr * LANES:(2 * pr + 1) * LANES] = vp_ref[:, pr * LANES:(pr + 1) * LANES]
        vcat_ref[QBLK:, 2 * pr * LANES:(2 * pr + 1) * LANES] = vc_ref[:, pr * LANES:(pr + 1) * LANES]
        vcat_ref[:, (2 * pr + 1) * LANES:(2 * pr + 2) * LANES] = jnp.ones((vcat_ref.shape[0], LANES), BF16)
    m_ref[...] = jnp.zeros(m_ref.shape, F32)
    l_ref[...] = jnp.ones(l_ref.shape, F32)
    for j in range(QBLKS_PER_STEP):
        rows = slice(j * QBLK, (j + 1) * QBLK)
        keys = slice(j * QBLK, (j + 2) * QBLK)
        first_ok = qi + jnp.where(pl.program_id(2) > 0, 0, QBLK) if j == 0 else qi
        mask = jnp.logical_and(jnp.logical_or(ki >= first_ok, ki >= QBLK), in_band)
        for pr in range(n_pairs):
            cols = slice(pr * LANES, (pr + 1) * LANES)
            q2 = q_ref[rows, cols]
            k2 = kcat_ref[keys, cols]
            v_ext = vcat_ref[keys, 2 * pr * LANES:(2 * pr + 2) * LANES]
            o_pair = None
            for a in range(2):
                own = low_half if a == 0 else jnp.logical_not(low_half)
                qm = jnp.where(own, q2, jnp.zeros_like(q2))
                s = lax.dot_general(qm, k2, _NT, preferred_element_type=F32)
                s = jnp.where(mask, s, NEG_BIG)
                m = jnp.max(s, axis=-1, keepdims=True)
                p = jnp.exp(s - m).astype(BF16)
                oe = jnp.dot(p, v_ext, preferred_element_type=F32)
                o_pair = oe[:, :LANES] if a == 0 else jnp.where(low_half, o_pair, oe[:, :LANES])
                h = 2 * pr + a
                m_ref[rows, h:h + 1] = m
                l_ref[rows, h:h + 1] = oe[:, LANES + h:LANES + h + 1]
            o_ref[rows, cols] = o_pair


def _group_attention(q, k, v):
    bsz, dil, length, width = q.shape
    step = QBLKS_PER_STEP * QBLK
    blk = (None, None, step, width)
    cur = lambda b, r, n: (b, r, n, 0)
    prev_blk = (None, None, QBLK, width)
    prev = lambda b, r, n: (b, r, jnp.maximum(QBLKS_PER_STEP * n - 1, 0), 0)
    stat_spec = pl.BlockSpec((None, None, step, LANES), cur)
    stat_shape = jax.ShapeDtypeStruct((bsz, dil, length, LANES), F32)
    return pl.pallas_call(
        _attn_kernel,
        grid=(bsz, dil, length // step),
        in_specs=[
            pl.BlockSpec(blk, cur),
            pl.BlockSpec(prev_blk, prev),
            pl.BlockSpec(blk, cur),
            pl.BlockSpec(prev_blk, prev),
            pl.BlockSpec(blk, cur),
        ],
        out_specs=[pl.BlockSpec(blk, cur), stat_spec, stat_spec],
        out_shape=[jax.ShapeDtypeStruct((bsz, dil, length, width), F32), stat_shape, stat_shape],
        scratch_shapes=[pltpu.VMEM((step + QBLK, width), BF16), pltpu.VMEM((step + QBLK, 2 * width), BF16)],
        compiler_params=_params("arbitrary", "arbitrary", "arbitrary"),
        name="group_attention",
    )(q, k, k, v, v)


def _to_token_order(src_ref, dst_ref):
    dil, n, w = src_ref.shape
    for r in range(dil):
        for c in range(w // LANES):
            dst_ref[c, pl.ds(r, n, stride=dil), :] = src_ref[r, :, c * LANES:(c + 1) * LANES]
    return jnp.concatenate([dst_ref[c] for c in range(w // LANES)], axis=1)


def _attn_out_kernel(x_ref, o0_ref, o1_ref, o2_ref, m0_ref, m1_ref, m2_ref, l0_ref, l1_ref, l2_ref,
                     wo_ref, g_ref, b_ref, out_ref, o_tok, stat_tok):
    width = o0_ref.shape[-1]

    def token_order(ref, scratch):
        if ref.shape[0] == 1:
            return ref[0]
        return _to_token_order(ref, scratch)

    ms = [token_order(r, stat_tok.at[i]) for i, r in enumerate((m0_ref, m1_ref, m2_ref))]
    ls = [token_order(r, stat_tok.at[N_GROUPS + i]) for i, r in enumerate((l0_ref, l1_ref, l2_ref))]
    top = jnp.maximum(jnp.maximum(ms[0], ms[1]), ms[2])
    es = [jnp.exp(m - top) for m in ms]
    den = es[0] * ls[0] + es[1] * ls[1] + es[2] * ls[2]
    head = lax.broadcasted_iota(I32, (LANES, width), 0)
    lane = lax.broadcasted_iota(I32, (LANES, width), 1)
    spread = (lane // HEAD_DIM == head).astype(BF16)
    mixed = jnp.zeros(out_ref.shape[:1] + (width,), F32)
    for e, o_ref in zip(es, (o0_ref, o1_ref, o2_ref)):
        hi, lo = _split_bf16(e / den)
        wide = (jnp.dot(hi, spread, preferred_element_type=F32)
                + jnp.dot(lo, spread, preferred_element_type=F32))
        mixed = mixed + wide * token_order(o_ref, o_tok)
    y = jnp.dot(mixed.astype(BF16), wo_ref[...], preferred_element_type=F32)
    out_ref[...] = _layer_norm(ALPHA * x_ref[...] + y, g_ref[...], b_ref[...])


def _attn_out(x, bsz, outs, stats, w_o, g, b):
    t, d = x.shape
    width = w_o.shape[0]
    tm = TM_RES
    tiles_per_seq = t // bsz // tm
    const = lambda bi, c: (0, 0)
    row = lambda bi, c: (bi * tiles_per_seq + c, 0)

    def res_spec(a):
        dil, w = a.shape[1], a.shape[3]
        return pl.BlockSpec((None, dil, tm // dil, w), lambda bi, c: (bi, 0, c, 0))

    return pl.pallas_call(
        _attn_out_kernel,
        grid=(bsz, tiles_per_seq),
        in_specs=([pl.BlockSpec((tm, d), row)] + [res_spec(a) for a in outs] + [res_spec(a) for a in stats]
                  + [pl.BlockSpec((width, d), const), pl.BlockSpec((1, d), const), pl.BlockSpec((1, d), const)]),
        out_specs=pl.BlockSpec((tm, d), row),
        out_shape=jax.ShapeDtypeStruct((t, d), F32),
        scratch_shapes=[pltpu.VMEM((width // LANES, tm, LANES), F32), pltpu.VMEM((2 * N_GROUPS, 1, tm, LANES), F32)],
        compiler_params=_params("arbitrary", "arbitrary"),
        name="attn_out",
    )(x, *outs, *stats, w_o, g, b)


def _proj_residue_kernel(x_ref, w_ref, *rest, dils, scale):
    outs, y_ref = rest[:-1], rest[-1]
    n_slabs, tm, _ = y_ref.shape
    width = n_slabs * LANES
    xb = x_ref[...].astype(BF16)
    for c, (o_ref, dil) in enumerate(zip(outs, dils)):
        y = jnp.dot(xb, w_ref[:, c * width:(c + 1) * width], preferred_element_type=F32)
        if scale != 1.0:
            y = y * scale
        if dil == 1:
            o_ref[0] = y.astype(o_ref.dtype)
        else:
            for s in range(n_slabs):
                y_ref[s] = y[:, s * LANES:(s + 1) * LANES]
            for r in range(dil):
                rows = [y_ref[s, pl.ds(r, tm // dil, stride=dil), :] for s in range(n_slabs)]
                o_ref[r] = jnp.concatenate(rows, axis=1).astype(o_ref.dtype)


def _proj_residue(x, bsz, w, dils, scale=1.0):
    t, d = x.shape
    seq = t // bsz
    width = w.shape[1] // len(dils)
    tm = TM_RES
    tiles_per_seq = seq // tm
    return pl.pallas_call(
        functools.partial(_proj_residue_kernel, dils=dils, scale=scale),
        grid=(bsz, tiles_per_seq),
        in_specs=[pl.BlockSpec((tm, d), lambda bi, c: (bi * tiles_per_seq + c, 0)),
                  pl.BlockSpec(w.shape, lambda bi, c: (0, 0))],
        out_specs=[pl.BlockSpec((None, dil, tm // dil, width), lambda bi, c: (bi, 0, c, 0)) for dil in dils],
        out_shape=[jax.ShapeDtypeStruct((bsz, dil, seq // dil, width), BF16) for dil in dils],
        scratch_shapes=[pltpu.VMEM((width // LANES, tm, LANES), F32)],
        compiler_params=_params("arbitrary", "arbitrary"),
        name="proj_residue",
    )(x, w)


def _attention_layer(x, bsz, w_q, w_o, kv_groups, g, b):
    dils = tuple(dil for _, dil in ATT_GROUPS)
    qs = _proj_residue(x, bsz, w_q, dils, scale=HEAD_DIM ** -0.5)
    outs, maxes, dens = [], [], []
    for q, (k_res, v_res) in zip(qs, kv_groups):
        o, m, l = _group_attention(q, k_res, v_res)
        outs.append(o)
        maxes.append(m)
        dens.append(l)
    return _attn_out(x, bsz, outs, maxes + dens, w_o, g, b)


def kernel(x, a_w_in, a_conv, a_w_out, kv_w, b_w_q, b_w_o, ffn_w_gate, ffn_w_up, ffn_w_down,
           moe_w_router, moe_w_gate, moe_w_up, moe_w_down, ln_g, ln_b):
    bsz, seq, d = x.shape
    sb = bsz // N_STREAMS
    hs = [x[s * sb:(s + 1) * sb].reshape(sb * seq, d) for s in range(N_STREAMS)]
    ln_g = ln_g.reshape(DEPTH, 2, 1, d)
    ln_b = ln_b.reshape(DEPTH, 2, 1, d)
    kv_dils = tuple(dil for _, dil in ATT_GROUPS for _ in range(2))
    kv_groups = None
    for l in range(DEPTH):
        if l < N_A_LAYERS:
            w_in, w_out = a_w_in[l].astype(BF16), a_w_out[l].astype(BF16)
            hs = [_a_mixer(h, w_in, a_conv[l], w_out, ln_g[l, 0], ln_b[l, 0], seq) for h in hs]
        else:
            if kv_groups is None:
                kv_wb = kv_w.astype(BF16)
                kvs = [_proj_residue(h, sb, kv_wb, kv_dils) for h in hs]
                kv_groups = [[(kv[2 * gi], kv[2 * gi + 1]) for gi in range(N_GROUPS)] for kv in kvs]
            j = l - N_A_LAYERS
            w_q, w_o = b_w_q[j].astype(BF16), b_w_o[j].astype(BF16)
            hs = [_attention_layer(h, sb, w_q, w_o, kvg, ln_g[l, 0], ln_b[l, 0]) for h, kvg in zip(hs, kv_groups)]
        i = l // 2
        if l % 2 == 0:
            w_gate, w_up, w_down = (w[i].astype(BF16) for w in (ffn_w_gate, ffn_w_up, ffn_w_down))
            hs = [_dense_ffn(h, w_gate, w_up, w_down, ln_g[l, 1], ln_b[l, 1]) for h in hs]
        else:
            hs = _moe_layer(hs, moe_w_router[i], moe_w_gate.astype(BF16), moe_w_up.astype(BF16),
                            moe_w_down.astype(BF16), i, ln_g[l, 1], ln_b[l, 1])
    return jnp.concatenate(hs, axis=0).reshape(bsz, seq, d)
```

```python
import functools

import jax
import jax.numpy as jnp
from jax import lax
from jax.experimental import pallas as pl
from jax.experimental.pallas import tpu as pltpu
from jax.experimental.pallas import tpu_sc as plsc

F32 = jnp.float32
U32 = jnp.uint32
BF16 = jnp.bfloat16
I32 = jnp.int32

DEPTH = 4
N_A_LAYERS = DEPTH // 2
CONV_WIDTH = 3
ATT_GROUPS = ((128, 1), (512, 4), (2048, 16))
N_GROUPS = len(ATT_GROUPS)
HEAD_DIM = 64
N_EXPERTS = 8
TOP_K = 2
ALPHA = (2.0 * DEPTH) ** 0.25
LN_EPS = 1e-5

QBLK = 128
assert all(w // d == QBLK for w, d in ATT_GROUPS)

LANES = 128
SUBLANES = 8
VMEM_LIMIT = 56 * 1024 * 1024
NEG_BIG = -1e30

TM_MIX = 512
TM_EXP = 512
TM_RES = 512
TM_WIDE = 1024
N_STREAMS = 1

_NT = (((1,), (1,)), ((), ()))


def _layer_norm(z, g, b):
    mu = jnp.mean(z, axis=-1, keepdims=True)
    zc = z - mu
    var = jnp.mean(zc * zc, axis=-1, keepdims=True)
    return zc * lax.rsqrt(var + LN_EPS) * g + b


def _params(*semantics):
    return pltpu.CompilerParams(dimension_semantics=semantics, vmem_limit_bytes=VMEM_LIMIT)


def _a_mixer_kernel(x_ref, win_ref, conv_ref, wout_ref, g_ref, b_ref, o_ref, ubuf, *, tiles_per_seq):
    tm, d = x_ref.shape

    @pl.when(pl.program_id(0) % tiles_per_seq == 0)
    def _():
        ubuf[0:SUBLANES, :] = jnp.zeros((SUBLANES, d), F32)

    x = x_ref[...]
    p = jnp.dot(x.astype(BF16), win_ref[...], preferred_element_type=F32)
    ubuf[SUBLANES:SUBLANES + tm, :] = p[:, d:2 * d] * p[:, 2 * d:]
    cw = conv_ref[...]
    conv = (cw[2:3, :] * ubuf[SUBLANES:SUBLANES + tm, :]
            + cw[1:2, :] * ubuf[SUBLANES - 1:SUBLANES - 1 + tm, :]
            + cw[0:1, :] * ubuf[SUBLANES - 2:SUBLANES - 2 + tm, :])
    y = jnp.dot((p[:, :d] * conv).astype(BF16), wout_ref[...], preferred_element_type=F32)
    o_ref[...] = _layer_norm(ALPHA * x + y, g_ref[...], b_ref[...])
    ubuf[0:SUBLANES, :] = ubuf[tm:tm + SUBLANES, :]


def _a_mixer(x, w_in, conv_w, w_out, g, b, seq_len):
    t, d = x.shape
    tm = TM_MIX
    const = lambda i: (0, 0)
    return pl.pallas_call(
        functools.partial(_a_mixer_kernel, tiles_per_seq=seq_len // tm),
        grid=(t // tm,),
        in_specs=[
            pl.BlockSpec((tm, d), lambda i: (i, 0)),
            pl.BlockSpec((d, 3 * d), const),
            pl.BlockSpec((CONV_WIDTH, d), const),
            pl.BlockSpec((d, d), const),
            pl.BlockSpec((1, d), const),
            pl.BlockSpec((1, d), const),
        ],
        out_specs=pl.BlockSpec((tm, d), lambda i: (i, 0)),
        out_shape=jax.ShapeDtypeStruct((t, d), F32),
        scratch_shapes=[pltpu.VMEM((tm + SUBLANES, d), F32)],
        compiler_params=_params("arbitrary"),
        name="a_mixer",
    )(x, w_in, conv_w, w_out, g, b)


def _swiglu_partial(xb, wg_ref, wu_ref, wd_ref):
    gate = jnp.dot(xb, wg_ref[...], preferred_element_type=F32)
    up = jnp.dot(xb, wu_ref[...], preferred_element_type=F32)
    h = (gate * jax.nn.sigmoid(gate) * up).astype(BF16)
    return jnp.dot(h, wd_ref[...], preferred_element_type=F32)


def _dense_ffn_kernel(x_ref, wg_ref, wu_ref, wd_ref, g_ref, b_ref, o_ref):
    x = x_ref[...]
    y = _swiglu_partial(x.astype(BF16), wg_ref, wu_ref, wd_ref)
    o_ref[...] = _layer_norm(ALPHA * x + y, g_ref[...], b_ref[...])


def _dense_ffn(x, w_gate, w_up, w_down, g, b):
    t, d = x.shape
    d_ff = w_gate.shape[1]
    tm = TM_MIX
    const = lambda i: (0, 0)
    resident = dict(pipeline_mode=pl.Buffered(1))
    return pl.pallas_call(
        _dense_ffn_kernel,
        grid=(t // tm,),
        in_specs=[
            pl.BlockSpec((tm, d), lambda i: (i, 0)),
            pl.BlockSpec((d, d_ff), const, **resident),
            pl.BlockSpec((d, d_ff), const, **resident),
            pl.BlockSpec((d_ff, d), const, **resident),
            pl.BlockSpec((1, d), const),
            pl.BlockSpec((1, d), const),
        ],
        out_specs=pl.BlockSpec((tm, d), lambda i: (i, 0)),
        out_shape=jax.ShapeDtypeStruct((t, d), F32),
        compiler_params=_params("arbitrary"),
        name="dense_ffn",
    )(x, w_gate, w_up, w_down, g, b)


EXPERT_COL_SPLITS = 2


def _expert_ffn_kernel(te_ref, nv_ref, x_ref, wg_ref, wu_ref, wd_ref, o_ref):
    del te_ref
    valid = pl.program_id(0) < nv_ref[0]

    @pl.when(valid)
    def _():
        xb = _unpack_bf16_pairs(x_ref[...]).astype(BF16)
        piece = wg_ref.shape[1] // EXPERT_COL_SPLITS
        y = None
        for c in range(EXPERT_COL_SPLITS):
            cols = slice(c * piece, (c + 1) * piece)
            gate = jnp.dot(xb, wg_ref[:, cols], preferred_element_type=F32)
            up = jnp.dot(xb, wu_ref[:, cols], preferred_element_type=F32)
            h = (gate * jax.nn.sigmoid(gate) * up).astype(BF16)
            part = jnp.dot(h, wd_ref[cols, :], preferred_element_type=F32)
            y = part if y is None else y + part
        o_ref[...] = _pack_bf16_pairs(y)

    @pl.when(jnp.logical_not(valid))
    def _():
        o_ref[...] = jnp.zeros(o_ref.shape, U32)


def _expert_ffn(xs, tile_expert, n_valid, w_gate, w_up, w_down, layer):
    d = w_gate.shape[2]
    d_exp = w_gate.shape[3]
    tm = TM_EXP
    n_rows = xs.shape[0] // tm * tm
    weights_of_tile = lambda i, te, nv: (layer, te[i], 0, 0)
    resident = dict(pipeline_mode=pl.Buffered(1))
    return pl.pallas_call(
        _expert_ffn_kernel,
        grid_spec=pltpu.PrefetchScalarGridSpec(
            num_scalar_prefetch=2,
            grid=(n_rows // tm,),
            in_specs=[
                pl.BlockSpec((tm, d // 2), lambda i, te, nv: (i, 0)),
                pl.BlockSpec((None, None, d, d_exp), weights_of_tile, **resident),
                pl.BlockSpec((None, None, d, d_exp), weights_of_tile, **resident),
                pl.BlockSpec((None, None, d_exp, d), weights_of_tile, **resident),
            ],
            out_specs=pl.BlockSpec((tm, d // 2), lambda i, te, nv: (i, 0)),
        ),
        out_shape=jax.ShapeDtypeStruct((n_rows, d // 2), U32),
        compiler_params=_params("arbitrary"),
        name="expert_ffn",
    )(tile_expert, n_valid, xs, w_gate, w_up, w_down)


def _split_bf16(v):
    hi = v.astype(BF16)
    lo = (v - hi.astype(F32)).astype(BF16)
    return hi, lo


def _pack_bf16_pairs(v):
    w = v.shape[1] // 2
    lo = lax.bitcast_convert_type(v[:, :w].astype(BF16).astype(F32), U32)
    hi = lax.bitcast_convert_type(v[:, w:].astype(BF16).astype(F32), U32)
    return (lo >> 16) | hi


def _unpack_bf16_pairs(p):
    lo = lax.bitcast_convert_type(p << 16, F32)
    hi = lax.bitcast_convert_type(p & jnp.uint32(0xFFFF0000), F32)
    return jnp.concatenate([lo, hi], axis=1)


def _router_kernel(x_ref, wr_ref, idx_ref, gate_ref, xp_ref):
    xp_ref[...] = _pack_bf16_pairs(x_ref[...])
    xh, xl = _split_bf16(x_ref[...])
    wh, wl = _split_bf16(wr_ref[...])
    logits = (lax.dot_general(wh, xh, _NT, preferred_element_type=F32)
              + lax.dot_general(wh, xl, _NT, preferred_element_type=F32)
              + lax.dot_general(wl, xh, _NT, preferred_element_type=F32))
    e = lax.broadcasted_iota(I32, logits.shape, 0)
    m1 = jnp.max(logits, axis=0, keepdims=True)
    i1 = jnp.min(jnp.where(logits == m1, e, N_EXPERTS), axis=0, keepdims=True)
    rest = jnp.where(e == i1, -jnp.inf, logits)
    m2 = jnp.max(rest, axis=0, keepdims=True)
    i2 = jnp.min(jnp.where(rest == m2, e, N_EXPERTS), axis=0, keepdims=True)
    r = jnp.exp(m2 - m1)
    idx_ref[...] = jnp.concatenate([i1, i2], axis=0)
    gate_ref[...] = jnp.concatenate([1.0 / (1.0 + r), r / (1.0 + r)], axis=0)


def _router(x, w_router_t):
    t, d = x.shape
    tm = TM_WIDE
    return pl.pallas_call(
        _router_kernel,
        grid=(t // tm,),
        in_specs=[
            pl.BlockSpec((tm, d), lambda i: (i, 0)),
            pl.BlockSpec((N_EXPERTS, d), lambda i: (0, 0)),
        ],
        out_specs=[
            pl.BlockSpec((TOP_K, tm), lambda i: (0, i)),
            pl.BlockSpec((TOP_K, tm), lambda i: (0, i)),
            pl.BlockSpec((tm, d // 2), lambda i: (i, 0)),
        ],
        out_shape=[jax.ShapeDtypeStruct((TOP_K, t), I32), jax.ShapeDtypeStruct((TOP_K, t), F32),
                   jax.ShapeDtypeStruct((t, d // 2), U32)],
        compiler_params=_params("arbitrary"),
        name="router",
    )(x, w_router_t)


def _dispatch_kernel(ends_ref, nv_ref, pos_ref, x_ref, xs_ref, zero_ref, sem, zero_sem):
    tm = x_ref.shape[0]

    @pl.when(pl.program_id(0) == 0)
    def _():
        zr = zero_ref.shape[0]
        zero_ref[...] = jnp.zeros(zero_ref.shape, F32)

        def zero_tile(tile):
            copies = [pltpu.make_async_copy(zero_ref, xs_ref.at[pl.ds(tile * TM_EXP + q * zr, zr)], zero_sem)
                      for q in range(TM_EXP // zr)]
            for c in copies:
                c.start()
            for c in copies:
                c.wait()

        for e in range(N_EXPERTS):
            zero_tile(jnp.maximum(ends_ref[e] // TM_EXP - 1, 0))
        for u in range(N_EXPERTS):
            @pl.when(nv_ref[0] + u < xs_ref.shape[0] // TM_EXP)
            def _():
                zero_tile(nv_ref[0] + u)

    def issue(j, carry):
        for k in range(TOP_K):
            pltpu.make_async_copy(x_ref.at[pl.ds(j, 1)], xs_ref.at[pl.ds(pos_ref[0, 0, TOP_K * j + k], 1)], sem).start()
        return carry

    lax.fori_loop(0, tm, issue, 0, unroll=4)
    for k in range(TOP_K):
        pltpu.make_async_copy(x_ref, xs_ref.at[pl.ds(0, tm)], sem).wait()


def _dispatch(x, pos, ends, n_valid, n_rows):
    t, d = x.shape
    tm = TM_MIX
    return pl.pallas_call(
        _dispatch_kernel,
        grid_spec=pltpu.PrefetchScalarGridSpec(
            num_scalar_prefetch=2,
            grid=(t // tm,),
            in_specs=[
                pl.BlockSpec((1, 1, TOP_K * tm), lambda i, ends, nv: (i, 0, 0), memory_space=pltpu.SMEM),
                pl.BlockSpec((tm, d), lambda i, ends, nv: (i, 0)),
            ],
            out_specs=pl.BlockSpec(memory_space=pl.ANY),
            scratch_shapes=[pltpu.VMEM((tm, d), F32), pltpu.SemaphoreType.DMA(()), pltpu.SemaphoreType.DMA(())],
        ),
        out_shape=jax.ShapeDtypeStruct((n_rows, d), F32),
        compiler_params=_params("arbitrary"),
        name="moe_dispatch",
    )(ends, n_valid, pos.reshape(t // tm, 1, TOP_K * tm), x)


def _moe_combine_kernel(pos_ref, pos_next_ref, x_ref, gate_ref, g_ref, b_ref, ys_ref, o_ref, ybuf, sem):
    tm, d = x_ref.shape
    i = pl.program_id(0)
    slot = i % 2

    def gather(idx_ref, s):
        def issue(j, carry):
            for k in range(TOP_K):
                pltpu.make_async_copy(ys_ref.at[pl.ds(idx_ref[0, 0, TOP_K * j + k], 1)], ybuf.at[s, k, pl.ds(j, 1)],
                                      sem.at[s]).start()
            return carry

        lax.fori_loop(0, tm, issue, 0, unroll=4)

    @pl.when(i == 0)
    def _():
        gather(pos_ref, 0)

    @pl.when(i + 1 < pl.num_programs(0))
    def _():
        gather(pos_next_ref, 1 - slot)

    for k in range(TOP_K):
        pltpu.make_async_copy(ys_ref.at[pl.ds(0, tm)], ybuf.at[slot, k], sem.at[slot]).wait()
    gates = gate_ref[...]
    ff = jnp.zeros((tm, d), F32)
    for k in range(TOP_K):
        ff = ff + gates[:, k:k + 1] * ybuf[slot, k]
    o_ref[...] = _layer_norm(ALPHA * x_ref[...] + ff, g_ref[...], b_ref[...])


def _moe_combine(x, ys, pos, gates, g, b):
    t, d = x.shape
    tm = TM_MIX
    steps = t // tm
    const = lambda i: (0, 0)
    pos = pos.reshape(steps, 1, TOP_K * tm)
    return pl.pallas_call(
        _moe_combine_kernel,
        grid=(steps,),
        in_specs=[
            pl.BlockSpec((1, 1, TOP_K * tm), lambda i: (i, 0, 0), memory_space=pltpu.SMEM),
            pl.BlockSpec((1, 1, TOP_K * tm), lambda i: (jnp.minimum(i + 1, steps - 1), 0, 0), memory_space=pltpu.SMEM),
            pl.BlockSpec((tm, d), lambda i: (i, 0)),
            pl.BlockSpec((tm, TOP_K), lambda i: (i, 0)),
            pl.BlockSpec((1, d), const),
            pl.BlockSpec((1, d), const),
            pl.BlockSpec(memory_space=pl.ANY),
        ],
        out_specs=pl.BlockSpec((tm, d), lambda i: (i, 0)),
        out_shape=jax.ShapeDtypeStruct((t, d), F32),
        scratch_shapes=[pltpu.VMEM((2, TOP_K, tm, d), F32), pltpu.SemaphoreType.DMA((2,))],
        compiler_params=_params("arbitrary"),
        name="moe_combine",
    )(pos, pos, x, gates, g, b, ys)


def _moe_route(x, w_router):
    t, d = x.shape
    n_assign = TOP_K * t
    idx, gates, x_packed = _router(x, w_router.T)

    e_flat = idx.T.reshape(n_assign)
    onehot = (e_flat[:, None] == jnp.arange(N_EXPERTS, dtype=I32)[None, :]).astype(I32)
    csum = jnp.cumsum(onehot, axis=0)
    rank = jnp.sum(csum * onehot, axis=1) - 1
    counts = csum[-1]
    padded = ((counts + TM_EXP - 1) // TM_EXP) * TM_EXP
    ends = jnp.cumsum(padded)
    pos = (jnp.sum((ends - padded)[None, :] * onehot, axis=1) + rank).astype(I32)
    n_tiles = n_assign // TM_EXP + N_EXPERTS
    tile_ids = jnp.arange(n_tiles, dtype=I32)
    tile_expert = jnp.minimum(
        jnp.sum((tile_ids[:, None] >= (ends // TM_EXP)[None, :]).astype(I32), axis=1), N_EXPERTS - 1).astype(I32)
    n_valid = (ends[-1:] // TM_EXP).astype(I32)

    pos_k = pos.reshape(t, TOP_K)
    n_rows = n_tiles * TM_EXP
    return dict(x_packed=x_packed, pos0=pos_k[:, 0], pos1=pos_k[:, 1], gates=gates.T, tile_expert=tile_expert, n_valid=n_valid,
                unused=_unused_rows(counts, padded, ends, n_rows), n_rows=n_rows)


def _moe_layer(streams, w_router, w_gate, w_up, w_down, layer, g, b):
    routes = [_moe_route(x, w_router) for x in streams]
    xs = [_sc_dispatch(r["x_packed"], r["pos0"], r["pos1"], r["unused"], r["n_rows"]) for r in routes]
    outs = []
    for x, r, x_sorted in zip(streams, routes, xs):
        ys = _expert_ffn(x_sorted, r["tile_expert"], r["n_valid"], w_gate, w_up, w_down, layer)
        outs.append((x, r, _sc_gather2(ys, r["pos0"], r["pos1"])))
    return [_moe_sum(x, y0, y1, r["gates"], g, b) for x, r, (y0, y1) in outs]


SC_CORES = 2
SC_SUBCORES = 16
SC_ROWS = 128


def _sc_worker_base(per_worker):
    return (lax.axis_index("subcore") * SC_CORES + lax.axis_index("core")) * per_worker


def _unused_rows(counts, padded, ends, n_rows):
    lane = jnp.arange(SC_ROWS, dtype=I32)
    chunk0 = jnp.arange(TM_EXP // SC_ROWS, dtype=I32) * SC_ROWS
    spare = n_rows + lane
    pad = (ends - padded + counts)[:, None, None] + chunk0[None, :, None] + lane[None, None, :]
    pad = jnp.where(pad < ends[:, None, None], pad, spare[None, None, :])
    tail = ends[-1] + jnp.arange(N_EXPERTS * TM_EXP // SC_ROWS, dtype=I32)[:, None] * SC_ROWS + lane[None, :]
    tail = jnp.where(tail < n_rows, tail, spare[None, :])
    return jnp.concatenate([pad.reshape(-1), tail.reshape(-1)]).astype(I32)


def _sc_dispatch(x, pos0, pos1, unused, n_rows):
    t, d = x.shape
    workers = SC_CORES * SC_SUBCORES
    per_worker = t // workers
    zero_chunks = unused.shape[0] // SC_ROWS // workers
    mesh = plsc.VectorSubcoreMesh(core_axis_name="core", subcore_axis_name="subcore")

    @pl.kernel(out_type=jax.ShapeDtypeStruct((n_rows + SC_ROWS, d), x.dtype), mesh=mesh,
               scratch_types=[pltpu.VMEM((SC_ROWS,), I32), pltpu.VMEM((SC_ROWS,), I32),
                              pltpu.VMEM((SC_ROWS, d), x.dtype)])
    def run(x_hbm, p0_hbm, p1_hbm, unused_hbm, zeros_hbm, o_hbm, i0_v, i1_v, rows_v):
        pltpu.sync_copy(zeros_hbm, rows_v)
        zbase = _sc_worker_base(zero_chunks * SC_ROWS)

        @pl.loop(0, zero_chunks)
        def _(c):
            pltpu.sync_copy(unused_hbm.at[pl.ds(zbase + c * SC_ROWS, SC_ROWS)], i0_v)
            pltpu.sync_copy(rows_v, o_hbm.at[i0_v])

        base = _sc_worker_base(per_worker)

        @pl.loop(0, per_worker // SC_ROWS)
        def _(c):
            off = base + c * SC_ROWS
            pltpu.sync_copy(p0_hbm.at[pl.ds(off, SC_ROWS)], i0_v)
            pltpu.sync_copy(p1_hbm.at[pl.ds(off, SC_ROWS)], i1_v)
            pltpu.sync_copy(x_hbm.at[pl.ds(off, SC_ROWS)], rows_v)
            pltpu.sync_copy(rows_v, o_hbm.at[i0_v])
            pltpu.sync_copy(rows_v, o_hbm.at[i1_v])

    return run(x, pos0, pos1, unused, jnp.zeros((SC_ROWS, d), x.dtype))


def _sc_gather2(ys, pos0, pos1):
    t = pos0.shape[0]
    d = ys.shape[1]
    per_worker = t // (SC_CORES * SC_SUBCORES)
    mesh = plsc.VectorSubcoreMesh(core_axis_name="core", subcore_axis_name="subcore")
    out = jax.ShapeDtypeStruct((t, d), ys.dtype)

    @pl.kernel(out_type=(out, out), mesh=mesh,
               scratch_types=[pltpu.VMEM((SC_ROWS,), I32), pltpu.VMEM((SC_ROWS, d), ys.dtype)])
    def run(y_hbm, p0_hbm, p1_hbm, o0_hbm, o1_hbm, i_v, rows_v):
        base = _sc_worker_base(per_worker)

        @pl.loop(0, per_worker // SC_ROWS)
        def _(c):
            off = base + c * SC_ROWS
            for p_hbm, o_hbm in ((p0_hbm, o0_hbm), (p1_hbm, o1_hbm)):
                pltpu.sync_copy(p_hbm.at[pl.ds(off, SC_ROWS)], i_v)
                pltpu.sync_copy(y_hbm.at[i_v], rows_v)
                pltpu.sync_copy(rows_v, o_hbm.at[pl.ds(off, SC_ROWS)])

    return run(ys, pos0, pos1)


def _moe_sum_kernel(x_ref, y0_ref, y1_ref, gate_ref, g_ref, b_ref, o_ref):
    gates = gate_ref[...]
    ff = gates[:, 0:1] * _unpack_bf16_pairs(y0_ref[...]) + gates[:, 1:2] * _unpack_bf16_pairs(y1_ref[...])
    o_ref[...] = _layer_norm(ALPHA * x_ref[...] + ff, g_ref[...], b_ref[...])


def _moe_sum(x, y0, y1, gates, g, b):
    t, d = x.shape
    tm = TM_WIDE
    const = lambda i: (0, 0)
    row = lambda i: (i, 0)
    return pl.pallas_call(
        _moe_sum_kernel,
        grid=(t // tm,),
        in_specs=[pl.BlockSpec((tm, d), row), pl.BlockSpec((tm, d // 2), row), pl.BlockSpec((tm, d // 2), row),
                  pl.BlockSpec((tm, TOP_K), row), pl.BlockSpec((1, d), const), pl.BlockSpec((1, d), const)],
        out_specs=pl.BlockSpec((tm, d), row),
        out_shape=jax.ShapeDtypeStruct((t, d), F32),
        compiler_params=_params("arbitrary"),
        name="moe_sum",
    )(x, y0, y1, gates, g, b)


MAX_QBLKS_PER_STEP = 8


def _attn_kernel(q_ref, kp_ref, kc_ref, vp_ref, vc_ref, o_ref, m_ref, l_ref, kcat_ref, vcat_ref):
    n_pairs = q_ref.shape[-1] // LANES
    qi = lax.broadcasted_iota(I32, (QBLK, 2 * QBLK), 0)
    ki = lax.broadcasted_iota(I32, (QBLK, 2 * QBLK), 1)
    in_band = ki - QBLK <= qi
    lane = lax.broadcasted_iota(I32, (QBLK, LANES), 1)
    low_half = lane < HEAD_DIM
    for pr in range(n_pairs):
        vcat_ref[:, (2 * pr + 1) * LANES:(2 * pr + 2) * LANES] = jnp.ones((vcat_ref.shape[0], LANES), BF16)
    m_ref[...] = jnp.zeros(m_ref.shape, F32)
    l_ref[...] = jnp.ones(l_ref.shape, F32)

    def residue_class(r, carry):
        kcat_ref[0:QBLK, :] = kp_ref[r]
        kcat_ref[QBLK:, :] = kc_ref[r]
        for pr in range(n_pairs):
            vcat_ref[0:QBLK, 2 * pr * LANES:(2 * pr + 1) * LANES] = vp_ref[r, :, pr * LANES:(pr + 1) * LANES]
            vcat_ref[QBLK:, 2 * pr * LANES:(2 * pr + 1) * LANES] = vc_ref[r, :, pr * LANES:(pr + 1) * LANES]
        lax.fori_loop(0, q_ref.shape[1] // QBLK,
                      functools.partial(block, q_ref.at[r], o_ref.at[r], m_ref.at[r], l_ref.at[r]), 0, unroll=2)
        return carry

    def block(q_r, o_r, m_r, l_r, j, carry):
        row0 = pl.multiple_of(j * QBLK, QBLK)
        rows = pl.ds(row0, QBLK)
        keys = pl.ds(row0, 2 * QBLK)
        has_prev = jnp.logical_or(j > 0, pl.program_id(2) > 0)
        first_ok = qi + jnp.where(has_prev, 0, QBLK)
        mask = jnp.logical_and(jnp.logical_or(ki >= first_ok, ki >= QBLK), in_band)
        for pr in range(n_pairs):
            cols = slice(pr * LANES, (pr + 1) * LANES)
            q2 = q_r[rows, cols]
            k2 = kcat_ref[keys, cols]
            v_ext = vcat_ref[keys, 2 * pr * LANES:(2 * pr + 2) * LANES]
            o_pair = None
            for a in range(2):
                own = low_half if a == 0 else jnp.logical_not(low_half)
                qm = jnp.where(own, q2, jnp.zeros_like(q2))
                s = lax.dot_general(qm, k2, _NT, preferred_element_type=F32)
                s = jnp.where(mask, s, NEG_BIG)
                m = jnp.max(s, axis=-1, keepdims=True)
                p = jnp.exp(s - m).astype(BF16)
                oe = jnp.dot(p, v_ext, preferred_element_type=F32)
                o_pair = oe[:, :LANES] if a == 0 else jnp.where(low_half, o_pair, oe[:, :LANES])
                h = 2 * pr + a
                m_r[rows, h:h + 1] = m
                l_r[rows, h:h + 1] = oe[:, LANES + h:LANES + h + 1]
            o_r[rows, cols] = o_pair.astype(o_r.dtype)
        return carry

    lax.fori_loop(0, q_ref.shape[0], residue_class, 0)


def _group_attention(q, k, v):
    bsz, dil, length, width = q.shape
    blocks_per_step = min(MAX_QBLKS_PER_STEP, length // QBLK)
    classes_per_step = min(dil, MAX_QBLKS_PER_STEP // blocks_per_step)
    step = blocks_per_step * QBLK
    blk = (None, classes_per_step, step, width)
    cur = lambda b, r, n: (b, r, n, 0)
    prev_blk = (None, classes_per_step, QBLK, width)
    prev = lambda b, r, n: (b, r, jnp.maximum(blocks_per_step * n - 1, 0), 0)
    stat_spec = pl.BlockSpec((None, classes_per_step, step, LANES), cur)
    stat_shape = jax.ShapeDtypeStruct((bsz, dil, length, LANES), F32)
    return pl.pallas_call(
        _attn_kernel,
        grid=(bsz, dil // classes_per_step, length // step),
        in_specs=[
            pl.BlockSpec(blk, cur),
            pl.BlockSpec(prev_blk, prev),
            pl.BlockSpec(blk, cur),
            pl.BlockSpec(prev_blk, prev),
            pl.BlockSpec(blk, cur),
        ],
        out_specs=[pl.BlockSpec(blk, cur), stat_spec, stat_spec],
        out_shape=[jax.ShapeDtypeStruct((bsz, dil, length, width), BF16), stat_shape, stat_shape],
        scratch_shapes=[pltpu.VMEM((step + QBLK, width), BF16), pltpu.VMEM((step + QBLK, 2 * width), BF16)],
        compiler_params=_params("arbitrary", "arbitrary", "arbitrary"),
        name="group_attention",
    )(q, k, k, v, v)


def _to_token_order(src_ref, dst_ref):
    dil, n, w = src_ref.shape
    for r in range(dil):
        for c in range(w // LANES):
            dst_ref[c, pl.ds(r, n, stride=dil), :] = src_ref[r, :, c * LANES:(c + 1) * LANES].astype(dst_ref.dtype)
    return jnp.concatenate([dst_ref[c] for c in range(w // LANES)], axis=1)


def _attn_out_kernel(x_ref, o0_ref, o1_ref, o2_ref, m0_ref, m1_ref, m2_ref, l0_ref, l1_ref, l2_ref,
                     wo_ref, g_ref, b_ref, out_ref, o_tok, stat_tok):
    width = o0_ref.shape[-1]

    def token_order(ref, scratch):
        if ref.shape[0] == 1:
            return ref[0].astype(F32)
        return _to_token_order(ref, scratch)

    ms = [token_order(r, stat_tok.at[i]) for i, r in enumerate((m0_ref, m1_ref, m2_ref))]
    ls = [token_order(r, stat_tok.at[N_GROUPS + i]) for i, r in enumerate((l0_ref, l1_ref, l2_ref))]
    top = jnp.maximum(jnp.maximum(ms[0], ms[1]), ms[2])
    es = [jnp.exp(m - top) for m in ms]
    den = es[0] * ls[0] + es[1] * ls[1] + es[2] * ls[2]
    head = lax.broadcasted_iota(I32, (LANES, width), 0)
    lane = lax.broadcasted_iota(I32, (LANES, width), 1)
    spread = (lane // HEAD_DIM == head).astype(BF16)
    mixed = jnp.zeros(out_ref.shape[:1] + (width,), F32)
    for e, o_ref in zip(es, (o0_ref, o1_ref, o2_ref)):
        hi, lo = _split_bf16(e / den)
        wide = (jnp.dot(hi, spread, preferred_element_type=F32)
                + jnp.dot(lo, spread, preferred_element_type=F32))
        mixed = mixed + wide * token_order(o_ref, o_tok)
    y = jnp.dot(mixed.astype(BF16), wo_ref[...], preferred_element_type=F32)
    out_ref[...] = _layer_norm(ALPHA * x_ref[...] + y, g_ref[...], b_ref[...])


def _attn_out(x, bsz, outs, stats, w_o, g, b):
    t, d = x.shape
    width = w_o.shape[0]
    tm = TM_RES
    tiles_per_seq = t // bsz // tm
    const = lambda bi, c: (0, 0)
    row = lambda bi, c: (bi * tiles_per_seq + c, 0)

    def res_spec(a):
        dil, w = a.shape[1], a.shape[3]
        return pl.BlockSpec((None, dil, tm // dil, w), lambda bi, c: (bi, 0, c, 0))

    return pl.pallas_call(
        _attn_out_kernel,
        grid=(bsz, tiles_per_seq),
        in_specs=([pl.BlockSpec((tm, d), row)] + [res_spec(a) for a in outs] + [res_spec(a) for a in stats]
                  + [pl.BlockSpec((width, d), const), pl.BlockSpec((1, d), const), pl.BlockSpec((1, d), const)]),
        out_specs=pl.BlockSpec((tm, d), row),
        out_shape=jax.ShapeDtypeStruct((t, d), F32),
        scratch_shapes=[pltpu.VMEM((width // LANES, tm, LANES), F32), pltpu.VMEM((2 * N_GROUPS, 1, tm, LANES), F32)],
        compiler_params=_params("arbitrary", "arbitrary"),
        name="attn_out",
    )(x, *outs, *stats, w_o, g, b)


def _proj_residue_kernel(x_ref, w_ref, *rest, dils, scale):
    outs, y_ref = rest[:-1], rest[-1]
    n_slabs, tm, _ = y_ref.shape
    width = n_slabs * LANES
    xb = x_ref[...].astype(BF16)
    for c, (o_ref, dil) in enumerate(zip(outs, dils)):
        y = jnp.dot(xb, w_ref[:, c * width:(c + 1) * width], preferred_element_type=F32)
        if scale != 1.0:
            y = y * scale
        if dil == 1:
            o_ref[0] = y.astype(o_ref.dtype)
        else:
            for s in range(n_slabs):
                y_ref[s] = y[:, s * LANES:(s + 1) * LANES]
            for r in range(dil):
                rows = [y_ref[s, pl.ds(r, tm // dil, stride=dil), :] for s in range(n_slabs)]
                o_ref[r] = jnp.concatenate(rows, axis=1).astype(o_ref.dtype)


def _proj_residue(x, bsz, w, dils, scale=1.0):
    t, d = x.shape
    seq = t // bsz
    width = w.shape[1] // len(dils)
    tm = TM_WIDE
    tiles_per_seq = seq // tm
    return pl.pallas_call(
        functools.partial(_proj_residue_kernel, dils=dils, scale=scale),
        grid=(bsz, tiles_per_seq),
        in_specs=[pl.BlockSpec((tm, d), lambda bi, c: (bi * tiles_per_seq + c, 0)),
                  pl.BlockSpec(w.shape, lambda bi, c: (0, 0))],
        out_specs=[pl.BlockSpec((None, dil, tm // dil, width), lambda bi, c: (bi, 0, c, 0)) for dil in dils],
        out_shape=[jax.ShapeDtypeStruct((bsz, dil, seq // dil, width), BF16) for dil in dils],
        scratch_shapes=[pltpu.VMEM((width // LANES, tm, LANES), F32)],
        compiler_params=_params("arbitrary", "arbitrary"),
        name="proj_residue",
    )(x, w)


def _attention_layer(x, bsz, w_q, w_o, kv_groups, g, b):
    dils = tuple(dil for _, dil in ATT_GROUPS)
    qs = _proj_residue(x, bsz, w_q, dils, scale=HEAD_DIM ** -0.5)
    outs, maxes, dens = [], [], []
    for q, (k_res, v_res) in zip(qs, kv_groups):
        o, m, l = _group_attention(q, k_res, v_res)
        outs.append(o)
        maxes.append(m)
        dens.append(l)
    return _attn_out(x, bsz, outs, maxes + dens, w_o, g, b)


def kernel(x, a_w_in, a_conv, a_w_out, kv_w, b_w_q, b_w_o, ffn_w_gate, ffn_w_up, ffn_w_down,
           moe_w_router, moe_w_gate, moe_w_up, moe_w_down, ln_g, ln_b):
    bsz, seq, d = x.shape
    sb = bsz // N_STREAMS
    hs = [x[s * sb:(s + 1) * sb].reshape(sb * seq, d) for s in range(N_STREAMS)]
    ln_g = ln_g.reshape(DEPTH, 2, 1, d)
    ln_b = ln_b.reshape(DEPTH, 2, 1, d)
    kv_dils = tuple(dil for _, dil in ATT_GROUPS for _ in range(2))
    kv_groups = None
    for l in range(DEPTH):
        if l < N_A_LAYERS:
            w_in, w_out = a_w_in[l].astype(BF16), a_w_out[l].astype(BF16)
            hs = [_a_mixer(h, w_in, a_conv[l], w_out, ln_g[l, 0], ln_b[l, 0], seq) for h in hs]
        else:
            if kv_groups is None:
                kv_wb = kv_w.astype(BF16)
                kvs = [_proj_residue(h, sb, kv_wb, kv_dils) for h in hs]
                kv_groups = [[(kv[2 * gi], kv[2 * gi + 1]) for gi in range(N_GROUPS)] for kv in kvs]
            j = l - N_A_LAYERS
            w_q, w_o = b_w_q[j].astype(BF16), b_w_o[j].astype(BF16)
            hs = [_attention_layer(h, sb, w_q, w_o, kvg, ln_g[l, 0], ln_b[l, 0]) for h, kvg in zip(hs, kv_groups)]
        i = l // 2
        if l % 2 == 0:
            w_gate, w_up, w_down = (w[i].astype(BF16) for w in (ffn_w_gate, ffn_w_up, ffn_w_down))
            hs = [_dense_ffn(h, w_gate, w_up, w_down, ln_g[l, 1], ln_b[l, 1]) for h in hs]
        else:
            hs = _moe_layer(hs, moe_w_router[i], moe_w_gate.astype(BF16), moe_w_up.astype(BF16),
                            moe_w_down.astype(BF16), i, ln_g[l, 1], ln_b[l, 1])
    return jnp.concatenate(hs, axis=0).reshape(bsz, seq, d)
```

```python
import functools

import jax
import jax.numpy as jnp
from jax import lax
from jax.experimental import pallas as pl
from jax.experimental.pallas import tpu as pltpu
from jax.experimental.pallas import tpu_sc as plsc

F32 = jnp.float32
U32 = jnp.uint32
BF16 = jnp.bfloat16
I32 = jnp.int32

DEPTH = 4
N_A_LAYERS = DEPTH // 2
CONV_WIDTH = 3
ATT_GROUPS = ((128, 1), (512, 4), (2048, 16))
N_GROUPS = len(ATT_GROUPS)
HEAD_DIM = 64
N_EXPERTS = 8
TOP_K = 2
ALPHA = (2.0 * DEPTH) ** 0.25
LN_EPS = 1e-5

QBLK = 128
assert all(w // d == QBLK for w, d in ATT_GROUPS)

LANES = 128
SUBLANES = 8
VMEM_LIMIT = 56 * 1024 * 1024
NEG_BIG = -1e30

TM_MIX = 512
TM_EXP = 512
TM_RES = 512
TM_WIDE = 1024
N_STREAMS = 1

_NT = (((1,), (1,)), ((), ()))


def _layer_norm(z, g, b):
    mu = jnp.mean(z, axis=-1, keepdims=True)
    zc = z - mu
    var = jnp.mean(zc * zc, axis=-1, keepdims=True)
    return zc * lax.rsqrt(var + LN_EPS) * g + b


def _params(*semantics):
    return pltpu.CompilerParams(dimension_semantics=semantics, vmem_limit_bytes=VMEM_LIMIT)


def _a_mixer_kernel(x_ref, win_ref, conv_ref, wout_ref, g_ref, b_ref, o_ref, ubuf, *, tiles_per_seq):
    tm, d = x_ref.shape

    @pl.when(pl.program_id(0) % tiles_per_seq == 0)
    def _():
        ubuf[0:SUBLANES, :] = jnp.zeros((SUBLANES, d), F32)

    x = x_ref[...]
    p = jnp.dot(x.astype(BF16), win_ref[...], preferred_element_type=F32)
    ubuf[SUBLANES:SUBLANES + tm, :] = p[:, d:2 * d] * p[:, 2 * d:]
    cw = conv_ref[...]
    conv = (cw[2:3, :] * ubuf[SUBLANES:SUBLANES + tm, :]
            + cw[1:2, :] * ubuf[SUBLANES - 1:SUBLANES - 1 + tm, :]
            + cw[0:1, :] * ubuf[SUBLANES - 2:SUBLANES - 2 + tm, :])
    y = jnp.dot((p[:, :d] * conv).astype(BF16), wout_ref[...], preferred_element_type=F32)
    o_ref[...] = _layer_norm(ALPHA * x + y, g_ref[...], b_ref[...])
    ubuf[0:SUBLANES, :] = ubuf[tm:tm + SUBLANES, :]


def _a_mixer(x, w_in, conv_w, w_out, g, b, seq_len):
    t, d = x.shape
    tm = TM_MIX
    const = lambda i: (0, 0)
    return pl.pallas_call(
        functools.partial(_a_mixer_kernel, tiles_per_seq=seq_len // tm),
        grid=(t // tm,),
        in_specs=[
            pl.BlockSpec((tm, d), lambda i: (i, 0)),
            pl.BlockSpec((d, 3 * d), const),
            pl.BlockSpec((CONV_WIDTH, d), const),
            pl.BlockSpec((d, d), const),
            pl.BlockSpec((1, d), const),
            pl.BlockSpec((1, d), const),
        ],
        out_specs=pl.BlockSpec((tm, d), lambda i: (i, 0)),
        out_shape=jax.ShapeDtypeStruct((t, d), F32),
        scratch_shapes=[pltpu.VMEM((tm + SUBLANES, d), F32)],
        compiler_params=_params("arbitrary"),
        name="a_mixer",
    )(x, w_in, conv_w, w_out, g, b)


def _swiglu_partial(xb, wg_ref, wu_ref, wd_ref):
    gate = jnp.dot(xb, wg_ref[...], preferred_element_type=F32)
    up = jnp.dot(xb, wu_ref[...], preferred_element_type=F32)
    h = (gate * jax.nn.sigmoid(gate) * up).astype(BF16)
    return jnp.dot(h, wd_ref[...], preferred_element_type=F32)


def _dense_ffn_kernel(x_ref, wg_ref, wu_ref, wd_ref, g_ref, b_ref, o_ref):
    x = x_ref[...]
    y = _swiglu_partial(x.astype(BF16), wg_ref, wu_ref, wd_ref)
    o_ref[...] = _layer_norm(ALPHA * x + y, g_ref[...], b_ref[...])


def _dense_ffn(x, w_gate, w_up, w_down, g, b):
    t, d = x.shape
    d_ff = w_gate.shape[1]
    tm = TM_MIX
    const = lambda i: (0, 0)
    resident = dict(pipeline_mode=pl.Buffered(1))
    return pl.pallas_call(
        _dense_ffn_kernel,
        grid=(t // tm,),
        in_specs=[
            pl.BlockSpec((tm, d), lambda i: (i, 0)),
            pl.BlockSpec((d, d_ff), const, **resident),
            pl.BlockSpec((d, d_ff), const, **resident),
            pl.BlockSpec((d_ff, d), const, **resident),
            pl.BlockSpec((1, d), const),
            pl.BlockSpec((1, d), const),
        ],
        out_specs=pl.BlockSpec((tm, d), lambda i: (i, 0)),
        out_shape=jax.ShapeDtypeStruct((t, d), F32),
        compiler_params=_params("arbitrary"),
        name="dense_ffn",
    )(x, w_gate, w_up, w_down, g, b)


EXPERT_COL_SPLITS = 2


def _expert_ffn_kernel(te_ref, nv_ref, x_ref, wg_ref, wu_ref, wd_ref, o_ref):
    del te_ref
    valid = pl.program_id(0) < nv_ref[0]

    @pl.when(valid)
    def _():
        xb = _unpack_bf16_pairs(x_ref[...]).astype(BF16)
        piece = wg_ref.shape[1] // EXPERT_COL_SPLITS
        y = None
        for c in range(EXPERT_COL_SPLITS):
            cols = slice(c * piece, (c + 1) * piece)
            gate = jnp.dot(xb, wg_ref[:, cols], preferred_element_type=F32)
            up = jnp.dot(xb, wu_ref[:, cols], preferred_element_type=F32)
            h = (gate * jax.nn.sigmoid(gate) * up).astype(BF16)
            part = jnp.dot(h, wd_ref[cols, :], preferred_element_type=F32)
            y = part if y is None else y + part
        o_ref[...] = _pack_bf16_pairs(y)

    @pl.when(jnp.logical_not(valid))
    def _():
        o_ref[...] = jnp.zeros(o_ref.shape, U32)


def _expert_ffn(xs, tile_expert, n_valid, w_gate, w_up, w_down, layer):
    d = w_gate.shape[2]
    d_exp = w_gate.shape[3]
    tm = TM_EXP
    n_rows = xs.shape[0] // tm * tm
    weights_of_tile = lambda i, te, nv: (layer, te[i], 0, 0)
    resident = dict(pipeline_mode=pl.Buffered(1))
    return pl.pallas_call(
        _expert_ffn_kernel,
        grid_spec=pltpu.PrefetchScalarGridSpec(
            num_scalar_prefetch=2,
            grid=(n_rows // tm,),
            in_specs=[
                pl.BlockSpec((tm, d // 2), lambda i, te, nv: (i, 0)),
                pl.BlockSpec((None, None, d, d_exp), weights_of_tile, **resident),
                pl.BlockSpec((None, None, d, d_exp), weights_of_tile, **resident),
                pl.BlockSpec((None, None, d_exp, d), weights_of_tile, **resident),
            ],
            out_specs=pl.BlockSpec((tm, d // 2), lambda i, te, nv: (i, 0)),
        ),
        out_shape=jax.ShapeDtypeStruct((n_rows, d // 2), U32),
        compiler_params=_params("arbitrary"),
        name="expert_ffn",
    )(tile_expert, n_valid, xs, w_gate, w_up, w_down)


def _split_bf16(v):
    hi = v.astype(BF16)
    lo = (v - hi.astype(F32)).astype(BF16)
    return hi, lo


def _pack_bf16_pairs(v):
    w = v.shape[1] // 2
    lo = lax.bitcast_convert_type(v[:, :w].astype(BF16).astype(F32), U32)
    hi = lax.bitcast_convert_type(v[:, w:].astype(BF16).astype(F32), U32)
    return (lo >> 16) | hi


def _unpack_bf16_pairs(p):
    lo = lax.bitcast_convert_type(p << 16, F32)
    hi = lax.bitcast_convert_type(p & jnp.uint32(0xFFFF0000), F32)
    return jnp.concatenate([lo, hi], axis=1)


def _router_kernel(x_ref, wr_ref, idx_ref, gate_ref, xp_ref, rank_ref, count_ref, run_ref):
    xp_ref[...] = _pack_bf16_pairs(x_ref[...])
    xh, xl = _split_bf16(x_ref[...])
    wh, wl = _split_bf16(wr_ref[...])
    logits = (lax.dot_general(wh, xh, _NT, preferred_element_type=F32)
              + lax.dot_general(wh, xl, _NT, preferred_element_type=F32)
              + lax.dot_general(wl, xh, _NT, preferred_element_type=F32))
    e = lax.broadcasted_iota(I32, logits.shape, 0)
    m1 = jnp.max(logits, axis=0, keepdims=True)
    i1 = jnp.min(jnp.where(logits == m1, e, N_EXPERTS), axis=0, keepdims=True)
    rest = jnp.where(e == i1, -jnp.inf, logits)
    m2 = jnp.max(rest, axis=0, keepdims=True)
    i2 = jnp.min(jnp.where(rest == m2, e, N_EXPERTS), axis=0, keepdims=True)
    r = jnp.exp(m2 - m1)
    idx_ref[...] = jnp.concatenate([i1, i2], axis=0)
    gate_ref[...] = jnp.concatenate([1.0 / (1.0 + r), r / (1.0 + r)], axis=0)

    @pl.when(pl.program_id(0) == 0)
    def _():
        run_ref[...] = jnp.zeros(run_ref.shape, F32)

    tm = logits.shape[1]
    uses = jnp.where(jnp.logical_or(e == i1, e == i2), 1.0, 0.0)
    earlier = lax.broadcasted_iota(I32, (tm, tm), 0) < lax.broadcasted_iota(I32, (tm, tm), 1)
    prefix = jnp.dot(uses.astype(BF16), jnp.where(earlier, 1.0, 0.0).astype(BF16), preferred_element_type=F32)
    prefix = prefix + run_ref[:, 0:1]
    ranks = [jnp.sum(jnp.where(e == i, prefix, 0.0), axis=0, keepdims=True) for i in (i1, i2)]
    rank_ref[...] = jnp.concatenate(ranks, axis=0).astype(I32)
    run_ref[...] = run_ref[...] + jnp.sum(uses, axis=1, keepdims=True)
    count_ref[...] = run_ref[...]


def _router(x, w_router_t):
    t, d = x.shape
    tm = TM_WIDE
    return pl.pallas_call(
        _router_kernel,
        grid=(t // tm,),
        in_specs=[
            pl.BlockSpec((tm, d), lambda i: (i, 0)),
            pl.BlockSpec((N_EXPERTS, d), lambda i: (0, 0)),
        ],
        out_specs=[
            pl.BlockSpec((TOP_K, tm), lambda i: (0, i)),
            pl.BlockSpec((TOP_K, tm), lambda i: (0, i)),
            pl.BlockSpec((tm, d // 2), lambda i: (i, 0)),
            pl.BlockSpec((TOP_K, tm), lambda i: (0, i)),
            pl.BlockSpec((N_EXPERTS, LANES), lambda i: (0, 0)),
        ],
        out_shape=[jax.ShapeDtypeStruct((TOP_K, t), I32), jax.ShapeDtypeStruct((TOP_K, t), F32),
                   jax.ShapeDtypeStruct((t, d // 2), U32), jax.ShapeDtypeStruct((TOP_K, t), I32),
                   jax.ShapeDtypeStruct((N_EXPERTS, LANES), F32)],
        scratch_shapes=[pltpu.VMEM((N_EXPERTS, LANES), F32)],
        compiler_params=_params("arbitrary"),
        name="router",
    )(x, w_router_t)


def _dispatch_kernel(ends_ref, nv_ref, pos_ref, x_ref, xs_ref, zero_ref, sem, zero_sem):
    tm = x_ref.shape[0]

    @pl.when(pl.program_id(0) == 0)
    def _():
        zr = zero_ref.shape[0]
        zero_ref[...] = jnp.zeros(zero_ref.shape, F32)

        def zero_tile(tile):
            copies = [pltpu.make_async_copy(zero_ref, xs_ref.at[pl.ds(tile * TM_EXP + q * zr, zr)], zero_sem)
                      for q in range(TM_EXP // zr)]
            for c in copies:
                c.start()
            for c in copies:
                c.wait()

        for e in range(N_EXPERTS):
            zero_tile(jnp.maximum(ends_ref[e] // TM_EXP - 1, 0))
        for u in range(N_EXPERTS):
            @pl.when(nv_ref[0] + u < xs_ref.shape[0] // TM_EXP)
            def _():
                zero_tile(nv_ref[0] + u)

    def issue(j, carry):
        for k in range(TOP_K):
            pltpu.make_async_copy(x_ref.at[pl.ds(j, 1)], xs_ref.at[pl.ds(pos_ref[0, 0, TOP_K * j + k], 1)], sem).start()
        return carry

    lax.fori_loop(0, tm, issue, 0, unroll=4)
    for k in range(TOP_K):
        pltpu.make_async_copy(x_ref, xs_ref.at[pl.ds(0, tm)], sem).wait()


def _dispatch(x, pos, ends, n_valid, n_rows):
    t, d = x.shape
    tm = TM_MIX
    return pl.pallas_call(
        _dispatch_kernel,
        grid_spec=pltpu.PrefetchScalarGridSpec(
            num_scalar_prefetch=2,
            grid=(t // tm,),
            in_specs=[
                pl.BlockSpec((1, 1, TOP_K * tm), lambda i, ends, nv: (i, 0, 0), memory_space=pltpu.SMEM),
                pl.BlockSpec((tm, d), lambda i, ends, nv: (i, 0)),
            ],
            out_specs=pl.BlockSpec(memory_space=pl.ANY),
            scratch_shapes=[pltpu.VMEM((tm, d), F32), pltpu.SemaphoreType.DMA(()), pltpu.SemaphoreType.DMA(())],
        ),
        out_shape=jax.ShapeDtypeStruct((n_rows, d), F32),
        compiler_params=_params("arbitrary"),
        name="moe_dispatch",
    )(ends, n_valid, pos.reshape(t // tm, 1, TOP_K * tm), x)


def _moe_combine_kernel(pos_ref, pos_next_ref, x_ref, gate_ref, g_ref, b_ref, ys_ref, o_ref, ybuf, sem):
    tm, d = x_ref.shape
    i = pl.program_id(0)
    slot = i % 2

    def gather(idx_ref, s):
        def issue(j, carry):
            for k in range(TOP_K):
                pltpu.make_async_copy(ys_ref.at[pl.ds(idx_ref[0, 0, TOP_K * j + k], 1)], ybuf.at[s, k, pl.ds(j, 1)],
                                      sem.at[s]).start()
            return carry

        lax.fori_loop(0, tm, issue, 0, unroll=4)

    @pl.when(i == 0)
    def _():
        gather(pos_ref, 0)

    @pl.when(i + 1 < pl.num_programs(0))
    def _():
        gather(pos_next_ref, 1 - slot)

    for k in range(TOP_K):
        pltpu.make_async_copy(ys_ref.at[pl.ds(0, tm)], ybuf.at[slot, k], sem.at[slot]).wait()
    gates = gate_ref[...]
    ff = jnp.zeros((tm, d), F32)
    for k in range(TOP_K):
        ff = ff + gates[:, k:k + 1] * ybuf[slot, k]
    o_ref[...] = _layer_norm(ALPHA * x_ref[...] + ff, g_ref[...], b_ref[...])


def _moe_combine(x, ys, pos, gates, g, b):
    t, d = x.shape
    tm = TM_MIX
    steps = t // tm
    const = lambda i: (0, 0)
    pos = pos.reshape(steps, 1, TOP_K * tm)
    return pl.pallas_call(
        _moe_combine_kernel,
        grid=(steps,),
        in_specs=[
            pl.BlockSpec((1, 1, TOP_K * tm), lambda i: (i, 0, 0), memory_space=pltpu.SMEM),
            pl.BlockSpec((1, 1, TOP_K * tm), lambda i: (jnp.minimum(i + 1, steps - 1), 0, 0), memory_space=pltpu.SMEM),
            pl.BlockSpec((tm, d), lambda i: (i, 0)),
            pl.BlockSpec((tm, TOP_K), lambda i: (i, 0)),
            pl.BlockSpec((1, d), const),
            pl.BlockSpec((1, d), const),
            pl.BlockSpec(memory_space=pl.ANY),
        ],
        out_specs=pl.BlockSpec((tm, d), lambda i: (i, 0)),
        out_shape=jax.ShapeDtypeStruct((t, d), F32),
        scratch_shapes=[pltpu.VMEM((2, TOP_K, tm, d), F32), pltpu.SemaphoreType.DMA((2,))],
        compiler_params=_params("arbitrary"),
        name="moe_combine",
    )(pos, pos, x, gates, g, b, ys)


def _moe_route(x, w_router):
    t, d = x.shape
    idx, gates, x_packed, rank, count = _router(x, w_router.T)

    counts = count[:, 0].astype(I32)
    padded = ((counts + TM_EXP - 1) // TM_EXP) * TM_EXP
    ends = jnp.cumsum(padded)
    experts = jnp.arange(N_EXPERTS, dtype=I32)[:, None, None]
    pos = rank + jnp.sum(jnp.where(idx[None] == experts, (ends - padded)[:, None, None], 0), axis=0)
    n_tiles = TOP_K * t // TM_EXP + N_EXPERTS
    tile_ids = jnp.arange(n_tiles, dtype=I32)
    tile_expert = jnp.minimum(
        jnp.sum((tile_ids[:, None] >= (ends // TM_EXP)[None, :]).astype(I32), axis=1), N_EXPERTS - 1).astype(I32)
    n_valid = (ends[-1:] // TM_EXP).astype(I32)

    n_rows = n_tiles * TM_EXP
    return dict(x_packed=x_packed, pos0=pos[0], pos1=pos[1], gates=gates.T, tile_expert=tile_expert, n_valid=n_valid,
                unused=_unused_rows(counts, padded, ends, n_rows), n_rows=n_rows)


def _moe_layer(streams, w_router, w_gate, w_up, w_down, layer, g, b):
    routes = [_moe_route(x, w_router) for x in streams]
    xs = [_sc_dispatch(r["x_packed"], r["pos0"], r["pos1"], r["unused"], r["n_rows"]) for r in routes]
    outs = []
    for x, r, x_sorted in zip(streams, routes, xs):
        ys = _expert_ffn(x_sorted, r["tile_expert"], r["n_valid"], w_gate, w_up, w_down, layer)
        outs.append((x, r, _sc_gather2(ys, r["pos0"], r["pos1"])))
    return [_moe_sum(x, y0, y1, r["gates"], g, b) for x, r, (y0, y1) in outs]


SC_CORES = 2
SC_SUBCORES = 16
SC_ROWS = 128


def _sc_worker_base(per_worker):
    return (lax.axis_index("subcore") * SC_CORES + lax.axis_index("core")) * per_worker


def _unused_rows(counts, padded, ends, n_rows):
    lane = jnp.arange(SC_ROWS, dtype=I32)
    chunk0 = jnp.arange(TM_EXP // SC_ROWS, dtype=I32) * SC_ROWS
    spare = n_rows + lane
    pad = (ends - padded + counts)[:, None, None] + chunk0[None, :, None] + lane[None, None, :]
    pad = jnp.where(pad < ends[:, None, None], pad, spare[None, None, :])
    tail = ends[-1] + jnp.arange(N_EXPERTS * TM_EXP // SC_ROWS, dtype=I32)[:, None] * SC_ROWS + lane[None, :]
    tail = jnp.where(tail < n_rows, tail, spare[None, :])
    return jnp.concatenate([pad.reshape(-1), tail.reshape(-1)]).astype(I32)


def _sc_dispatch(x, pos0, pos1, unused, n_rows):
    t, d = x.shape
    workers = SC_CORES * SC_SUBCORES
    per_worker = t // workers
    zero_chunks = unused.shape[0] // SC_ROWS // workers
    mesh = plsc.VectorSubcoreMesh(core_axis_name="core", subcore_axis_name="subcore")

    @pl.kernel(out_type=jax.ShapeDtypeStruct((n_rows + SC_ROWS, d), x.dtype), mesh=mesh,
               scratch_types=[pltpu.VMEM((SC_ROWS,), I32), pltpu.VMEM((SC_ROWS,), I32),
                              pltpu.VMEM((SC_ROWS, d), x.dtype)])
    def run(x_hbm, p0_hbm, p1_hbm, unused_hbm, zeros_hbm, o_hbm, i0_v, i1_v, rows_v):
        pltpu.sync_copy(zeros_hbm, rows_v)
        zbase = _sc_worker_base(zero_chunks * SC_ROWS)

        @pl.loop(0, zero_chunks)
        def _(c):
            pltpu.sync_copy(unused_hbm.at[pl.ds(zbase + c * SC_ROWS, SC_ROWS)], i0_v)
            pltpu.sync_copy(rows_v, o_hbm.at[i0_v])

        base = _sc_worker_base(per_worker)

        @pl.loop(0, per_worker // SC_ROWS)
        def _(c):
            off = base + c * SC_ROWS
            pltpu.sync_copy(p0_hbm.at[pl.ds(off, SC_ROWS)], i0_v)
            pltpu.sync_copy(p1_hbm.at[pl.ds(off, SC_ROWS)], i1_v)
            pltpu.sync_copy(x_hbm.at[pl.ds(off, SC_ROWS)], rows_v)
            pltpu.sync_copy(rows_v, o_hbm.at[i0_v])
            pltpu.sync_copy(rows_v, o_hbm.at[i1_v])

    return run(x, pos0, pos1, unused, jnp.zeros((SC_ROWS, d), x.dtype))


def _sc_gather2(ys, pos0, pos1):
    t = pos0.shape[0]
    d = ys.shape[1]
    per_worker = t // (SC_CORES * SC_SUBCORES)
    mesh = plsc.VectorSubcoreMesh(core_axis_name="core", subcore_axis_name="subcore")
    out = jax.ShapeDtypeStruct((t, d), ys.dtype)

    @pl.kernel(out_type=(out, out), mesh=mesh,
               scratch_types=[pltpu.VMEM((SC_ROWS,), I32), pltpu.VMEM((SC_ROWS, d), ys.dtype)])
    def run(y_hbm, p0_hbm, p1_hbm, o0_hbm, o1_hbm, i_v, rows_v):
        base = _sc_worker_base(per_worker)

        @pl.loop(0, per_worker // SC_ROWS)
        def _(c):
            off = base + c * SC_ROWS
            for p_hbm, o_hbm in ((p0_hbm, o0_hbm), (p1_hbm, o1_hbm)):
                pltpu.sync_copy(p_hbm.at[pl.ds(off, SC_ROWS)], i_v)
                pltpu.sync_copy(y_hbm.at[i_v], rows_v)
                pltpu.sync_copy(rows_v, o_hbm.at[pl.ds(off, SC_ROWS)])

    return run(ys, pos0, pos1)


def _moe_sum_kernel(x_ref, y0_ref, y1_ref, gate_ref, g_ref, b_ref, o_ref):
    gates = gate_ref[...]
    ff = gates[:, 0:1] * _unpack_bf16_pairs(y0_ref[...]) + gates[:, 1:2] * _unpack_bf16_pairs(y1_ref[...])
    o_ref[...] = _layer_norm(ALPHA * x_ref[...] + ff, g_ref[...], b_ref[...])


def _moe_sum(x, y0, y1, gates, g, b):
    t, d = x.shape
    tm = TM_WIDE
    const = lambda i: (0, 0)
    row = lambda i: (i, 0)
    return pl.pallas_call(
        _moe_sum_kernel,
        grid=(t // tm,),
        in_specs=[pl.BlockSpec((tm, d), row), pl.BlockSpec((tm, d // 2), row), pl.BlockSpec((tm, d // 2), row),
                  pl.BlockSpec((tm, TOP_K), row), pl.BlockSpec((1, d), const), pl.BlockSpec((1, d), const)],
        out_specs=pl.BlockSpec((tm, d), row),
        out_shape=jax.ShapeDtypeStruct((t, d), F32),
        compiler_params=_params("arbitrary"),
        name="moe_sum",
    )(x, y0, y1, gates, g, b)


MAX_QBLKS_PER_STEP = 8


def _attn_kernel(q_ref, kp_ref, kc_ref, vp_ref, vc_ref, o_ref, m_ref, l_ref, kcat_ref, vcat_ref):
    n_pairs = q_ref.shape[-1] // LANES
    qi = lax.broadcasted_iota(I32, (QBLK, 2 * QBLK), 0)
    ki = lax.broadcasted_iota(I32, (QBLK, 2 * QBLK), 1)
    in_band = ki - QBLK <= qi
    lane = lax.broadcasted_iota(I32, (QBLK, LANES), 1)
    low_half = lane < HEAD_DIM
    for pr in range(n_pairs):
        vcat_ref[:, (2 * pr + 1) * LANES:(2 * pr + 2) * LANES] = jnp.ones((vcat_ref.shape[0], LANES), BF16)
    m_ref[...] = jnp.zeros(m_ref.shape, F32)
    l_ref[...] = jnp.ones(l_ref.shape, F32)

    def residue_class(r, carry):
        kcat_ref[0:QBLK, :] = kp_ref[r]
        kcat_ref[QBLK:, :] = kc_ref[r]
        for pr in range(n_pairs):
            vcat_ref[0:QBLK, 2 * pr * LANES:(2 * pr + 1) * LANES] = vp_ref[r, :, pr * LANES:(pr + 1) * LANES]
            vcat_ref[QBLK:, 2 * pr * LANES:(2 * pr + 1) * LANES] = vc_ref[r, :, pr * LANES:(pr + 1) * LANES]
        lax.fori_loop(0, q_ref.shape[1] // QBLK,
                      functools.partial(block, q_ref.at[r], o_ref.at[r], m_ref.at[r], l_ref.at[r]), 0, unroll=2)
        return carry

    def block(q_r, o_r, m_r, l_r, j, carry):
        row0 = pl.multiple_of(j * QBLK, QBLK)
        rows = pl.ds(row0, QBLK)
        keys = pl.ds(row0, 2 * QBLK)
        has_prev = jnp.logical_or(j > 0, pl.program_id(2) > 0)
        first_ok = qi + jnp.where(has_prev, 0, QBLK)
        mask = jnp.logical_and(jnp.logical_or(ki >= first_ok, ki >= QBLK), in_band)
        for pr in range(n_pairs):
            cols = slice(pr * LANES, (pr + 1) * LANES)
            q2 = q_r[rows, cols]
            k2 = kcat_ref[keys, cols]
            v_ext = vcat_ref[keys, 2 * pr * LANES:(2 * pr + 2) * LANES]
            o_pair = None
            for a in range(2):
                own = low_half if a == 0 else jnp.logical_not(low_half)
                qm = jnp.where(own, q2, jnp.zeros_like(q2))
                s = lax.dot_general(qm, k2, _NT, preferred_element_type=F32)
                s = jnp.where(mask, s, NEG_BIG)
                m = jnp.max(s, axis=-1, keepdims=True)
                p = jnp.exp(s - m).astype(BF16)
                oe = jnp.dot(p, v_ext, preferred_element_type=F32)
                o_pair = oe[:, :LANES] if a == 0 else jnp.where(low_half, o_pair, oe[:, :LANES])
                h = 2 * pr + a
                m_r[rows, h:h + 1] = m
                l_r[rows, h:h + 1] = oe[:, LANES + h:LANES + h + 1]
            o_r[rows, cols] = o_pair.astype(o_r.dtype)
        return carry

    lax.fori_loop(0, q_ref.shape[0], residue_class, 0)


def _group_attention(q, k, v):
    bsz, dil, length, width = q.shape
    blocks_per_step = min(MAX_QBLKS_PER_STEP, length // QBLK)
    classes_per_step = min(dil, MAX_QBLKS_PER_STEP // blocks_per_step)
    step = blocks_per_step * QBLK
    blk = (None, classes_per_step, step, width)
    cur = lambda b, r, n: (b, r, n, 0)
    prev_blk = (None, classes_per_step, QBLK, width)
    prev = lambda b, r, n: (b, r, jnp.maximum(blocks_per_step * n - 1, 0), 0)
    stat_spec = pl.BlockSpec((None, classes_per_step, step, LANES), cur)
    stat_shape = jax.ShapeDtypeStruct((bsz, dil, length, LANES), F32)
    return pl.pallas_call(
        _attn_kernel,
        grid=(bsz, dil // classes_per_step, length // step),
        in_specs=[
            pl.BlockSpec(blk, cur),
            pl.BlockSpec(prev_blk, prev),
            pl.BlockSpec(blk, cur),
            pl.BlockSpec(prev_blk, prev),
            pl.BlockSpec(blk, cur),
        ],
        out_specs=[pl.BlockSpec(blk, cur), stat_spec, stat_spec],
        out_shape=[jax.ShapeDtypeStruct((bsz, dil, length, width), BF16), stat_shape, stat_shape],
        scratch_shapes=[pltpu.VMEM((step + QBLK, width), BF16), pltpu.VMEM((step + QBLK, 2 * width), BF16)],
        compiler_params=_params("arbitrary", "arbitrary", "arbitrary"),
        name="group_attention",
    )(q, k, k, v, v)


def _to_token_order(src_ref, dst_ref):
    dil, n, w = src_ref.shape
    for r in range(dil):
        for c in range(w // LANES):
            dst_ref[c, pl.ds(r, n, stride=dil), :] = src_ref[r, :, c * LANES:(c + 1) * LANES].astype(dst_ref.dtype)
    return jnp.concatenate([dst_ref[c] for c in range(w // LANES)], axis=1)


def _attn_out_kernel(x_ref, o0_ref, o1_ref, o2_ref, m0_ref, m1_ref, m2_ref, l0_ref, l1_ref, l2_ref,
                     wo_ref, g_ref, b_ref, out_ref, o_tok, stat_tok):
    width = o0_ref.shape[-1]

    def token_order(ref, scratch):
        if ref.shape[0] == 1:
            return ref[0].astype(F32)
        return _to_token_order(ref, scratch)

    ms = [token_order(r, stat_tok.at[i]) for i, r in enumerate((m0_ref, m1_ref, m2_ref))]
    ls = [token_order(r, stat_tok.at[N_GROUPS + i]) for i, r in enumerate((l0_ref, l1_ref, l2_ref))]
    top = jnp.maximum(jnp.maximum(ms[0], ms[1]), ms[2])
    es = [jnp.exp(m - top) for m in ms]
    den = es[0] * ls[0] + es[1] * ls[1] + es[2] * ls[2]
    head = lax.broadcasted_iota(I32, (LANES, width), 0)
    lane = lax.broadcasted_iota(I32, (LANES, width), 1)
    spread = (lane // HEAD_DIM == head).astype(BF16)
    mixed = jnp.zeros(out_ref.shape[:1] + (width,), F32)
    for e, o_ref in zip(es, (o0_ref, o1_ref, o2_ref)):
        hi, lo = _split_bf16(e / den)
        wide = (jnp.dot(hi, spread, preferred_element_type=F32)
                + jnp.dot(lo, spread, preferred_element_type=F32))
        mixed = mixed + wide * token_order(o_ref, o_tok)
    y = jnp.dot(mixed.astype(BF16), wo_ref[...], preferred_element_type=F32)
    out_ref[...] = _layer_norm(ALPHA * x_ref[...] + y, g_ref[...], b_ref[...])


def _attn_out(x, bsz, outs, stats, w_o, g, b):
    t, d = x.shape
    width = w_o.shape[0]
    tm = TM_RES
    tiles_per_seq = t // bsz // tm
    const = lambda bi, c: (0, 0)
    row = lambda bi, c: (bi * tiles_per_seq + c, 0)

    def res_spec(a):
        dil, w = a.shape[1], a.shape[3]
        return pl.BlockSpec((None, dil, tm // dil, w), lambda bi, c: (bi, 0, c, 0))

    return pl.pallas_call(
        _attn_out_kernel,
        grid=(bsz, tiles_per_seq),
        in_specs=([pl.BlockSpec((tm, d), row)] + [res_spec(a) for a in outs] + [res_spec(a) for a in stats]
                  + [pl.BlockSpec((width, d), const), pl.BlockSpec((1, d), const), pl.BlockSpec((1, d), const)]),
        out_specs=pl.BlockSpec((tm, d), row),
        out_shape=jax.ShapeDtypeStruct((t, d), F32),
        scratch_shapes=[pltpu.VMEM((width // LANES, tm, LANES), F32), pltpu.VMEM((2 * N_GROUPS, 1, tm, LANES), F32)],
        compiler_params=_params("arbitrary", "arbitrary"),
        name="attn_out",
    )(x, *outs, *stats, w_o, g, b)


def _proj_residue_kernel(x_ref, w_ref, *rest, dils, scale):
    outs, y_ref = rest[:-1], rest[-1]
    _, n_slabs, tm, _ = y_ref.shape
    width = n_slabs * LANES
    xb = x_ref[...].astype(BF16)
    for c, (o_ref, dil) in enumerate(zip(outs, dils)):
        y = jnp.dot(xb, w_ref[:, c * width:(c + 1) * width], preferred_element_type=F32)
        if scale != 1.0:
            y = y * scale
        if dil == 1:
            o_ref[0] = y.astype(o_ref.dtype)
        else:
            for s in range(n_slabs):
                y_ref[c, s] = y[:, s * LANES:(s + 1) * LANES]
            for r in range(dil):
                rows = [y_ref[c, s, pl.ds(r, tm // dil, stride=dil), :] for s in range(n_slabs)]
                o_ref[r] = jnp.concatenate(rows, axis=1).astype(o_ref.dtype)


def _proj_residue(x, bsz, w, dils, scale=1.0):
    t, d = x.shape
    seq = t // bsz
    width = w.shape[1] // len(dils)
    tm = TM_WIDE
    tiles_per_seq = seq // tm
    return pl.pallas_call(
        functools.partial(_proj_residue_kernel, dils=dils, scale=scale),
        grid=(bsz, tiles_per_seq),
        in_specs=[pl.BlockSpec((tm, d), lambda bi, c: (bi * tiles_per_seq + c, 0)),
                  pl.BlockSpec(w.shape, lambda bi, c: (0, 0))],
        out_specs=[pl.BlockSpec((None, dil, tm // dil, width), lambda bi, c: (bi, 0, c, 0)) for dil in dils],
        out_shape=[jax.ShapeDtypeStruct((bsz, dil, seq // dil, width), BF16) for dil in dils],
        scratch_shapes=[pltpu.VMEM((len(dils), width // LANES, tm, LANES), F32)],
        compiler_params=_params("arbitrary", "arbitrary"),
        name="proj_residue",
    )(x, w)


def _attention_layer(x, bsz, w_q, w_o, kv_groups, g, b):
    dils = tuple(dil for _, dil in ATT_GROUPS)
    qs = _proj_residue(x, bsz, w_q, dils, scale=HEAD_DIM ** -0.5)
    outs, maxes, dens = [], [], []
    for q, (k_res, v_res) in zip(qs, kv_groups):
        o, m, l = _group_attention(q, k_res, v_res)
        outs.append(o)
        maxes.append(m)
        dens.append(l)
    return _attn_out(x, bsz, outs, maxes + dens, w_o, g, b)


def kernel(x, a_w_in, a_conv, a_w_out, kv_w, b_w_q, b_w_o, ffn_w_gate, ffn_w_up, ffn_w_down,
           moe_w_router, moe_w_gate, moe_w_up, moe_w_down, ln_g, ln_b):
    bsz, seq, d = x.shape
    sb = bsz // N_STREAMS
    hs = [x[s * sb:(s + 1) * sb].reshape(sb * seq, d) for s in range(N_STREAMS)]
    ln_g = ln_g.reshape(DEPTH, 2, 1, d)
    ln_b = ln_b.reshape(DEPTH, 2, 1, d)
    kv_dils = tuple(dil for _, dil in ATT_GROUPS for _ in range(2))
    kv_groups = None
    for l in range(DEPTH):
        if l < N_A_LAYERS:
            w_in, w_out = a_w_in[l].astype(BF16), a_w_out[l].astype(BF16)
            hs = [_a_mixer(h, w_in, a_conv[l], w_out, ln_g[l, 0], ln_b[l, 0], seq) for h in hs]
        else:
            if kv_groups is None:
                kv_wb = kv_w.astype(BF16)
                kvs = [_proj_residue(h, sb, kv_wb, kv_dils) for h in hs]
                kv_groups = [[(kv[2 * gi], kv[2 * gi + 1]) for gi in range(N_GROUPS)] for kv in kvs]
            j = l - N_A_LAYERS
            w_q, w_o = b_w_q[j].astype(BF16), b_w_o[j].astype(BF16)
            hs = [_attention_layer(h, sb, w_q, w_o, kvg, ln_g[l, 0], ln_b[l, 0]) for h, kvg in zip(hs, kv_groups)]
        i = l // 2
        if l % 2 == 0:
            w_gate, w_up, w_down = (w[i].astype(BF16) for w in (ffn_w_gate, ffn_w_up, ffn_w_down))
            hs = [_dense_ffn(h, w_gate, w_up, w_down, ln_g[l, 1], ln_b[l, 1]) for h in hs]
        else:
            hs = _moe_layer(hs, moe_w_router[i], moe_w_gate.astype(BF16), moe_w_up.astype(BF16),
                            moe_w_down.astype(BF16), i, ln_g[l, 1], ln_b[l, 1])
    return jnp.concatenate(hs, axis=0).reshape(bsz, seq, d)
```

```python
import functools

import jax
import jax.numpy as jnp
from jax import lax
from jax.experimental import pallas as pl
from jax.experimental.pallas import tpu as pltpu
from jax.experimental.pallas import tpu_sc as plsc

F32 = jnp.float32
U32 = jnp.uint32
BF16 = jnp.bfloat16
I32 = jnp.int32

DEPTH = 4
N_A_LAYERS = DEPTH // 2
CONV_WIDTH = 3
ATT_GROUPS = ((128, 1), (512, 4), (2048, 16))
N_GROUPS = len(ATT_GROUPS)
HEAD_DIM = 64
N_EXPERTS = 8
TOP_K = 2
ALPHA = (2.0 * DEPTH) ** 0.25
LN_EPS = 1e-5

QBLK = 128
assert all(w // d == QBLK for w, d in ATT_GROUPS)

LANES = 128
SUBLANES = 8
VMEM_LIMIT = 56 * 1024 * 1024
NEG_BIG = -1e30

TM_MIX = 512
TM_EXP = 512
TM_RES = 512
TM_WIDE = 1024
N_STREAMS = 1

_NT = (((1,), (1,)), ((), ()))


def _layer_norm(z, g, b):
    mu = jnp.mean(z, axis=-1, keepdims=True)
    zc = z - mu
    var = jnp.mean(zc * zc, axis=-1, keepdims=True)
    return zc * lax.rsqrt(var + LN_EPS) * g + b


def _params(*semantics):
    return pltpu.CompilerParams(dimension_semantics=semantics, vmem_limit_bytes=VMEM_LIMIT)


def _a_mixer_kernel(x_ref, win_ref, conv_ref, wout_ref, g_ref, b_ref, o_ref, ubuf, *, tiles_per_seq):
    tm, d = x_ref.shape
    piece = tm // A_MIXER_PIECES

    @pl.when(pl.program_id(0) % tiles_per_seq == 0)
    def _():
        ubuf[0:SUBLANES, :] = jnp.zeros((SUBLANES, d), F32)

    cw = conv_ref[...]
    for h in range(A_MIXER_PIECES):
        rows = slice(h * piece, (h + 1) * piece)
        first = SUBLANES + h * piece
        x = x_ref[rows, :]
        p = jnp.dot(x.astype(BF16), win_ref[...], preferred_element_type=F32)
        ubuf[first:first + piece, :] = p[:, d:2 * d] * p[:, 2 * d:]
        conv = (cw[2:3, :] * ubuf[first:first + piece, :]
                + cw[1:2, :] * ubuf[first - 1:first - 1 + piece, :]
                + cw[0:1, :] * ubuf[first - 2:first - 2 + piece, :])
        y = jnp.dot((p[:, :d] * conv).astype(BF16), wout_ref[...], preferred_element_type=F32)
        o_ref[rows, :] = _layer_norm(ALPHA * x + y, g_ref[...], b_ref[...])
    ubuf[0:SUBLANES, :] = ubuf[tm:tm + SUBLANES, :]


A_MIXER_PIECES = 2


def _a_mixer(x, w_in, conv_w, w_out, g, b, seq_len):
    t, d = x.shape
    tm = TM_WIDE
    const = lambda i: (0, 0)
    resident = dict(pipeline_mode=pl.Buffered(1))
    return pl.pallas_call(
        functools.partial(_a_mixer_kernel, tiles_per_seq=seq_len // tm),
        grid=(t // tm,),
        in_specs=[
            pl.BlockSpec((tm, d), lambda i: (i, 0)),
            pl.BlockSpec((d, 3 * d), const, **resident),
            pl.BlockSpec((CONV_WIDTH, d), const),
            pl.BlockSpec((d, d), const, **resident),
            pl.BlockSpec((1, d), const),
            pl.BlockSpec((1, d), const),
        ],
        out_specs=pl.BlockSpec((tm, d), lambda i: (i, 0)),
        out_shape=jax.ShapeDtypeStruct((t, d), F32),
        scratch_shapes=[pltpu.VMEM((tm + SUBLANES, d), F32)],
        compiler_params=_params("arbitrary"),
        name="a_mixer",
    )(x, w_in, conv_w, w_out, g, b)


def _swiglu_partial(xb, wg_ref, wu_ref, wd_ref):
    gate = jnp.dot(xb, wg_ref[...], preferred_element_type=F32)
    up = jnp.dot(xb, wu_ref[...], preferred_element_type=F32)
    h = (gate * jax.nn.sigmoid(gate) * up).astype(BF16)
    return jnp.dot(h, wd_ref[...], preferred_element_type=F32)


def _dense_ffn_kernel(x_ref, wg_ref, wu_ref, wd_ref, g_ref, b_ref, o_ref):
    x = x_ref[...]
    y = _swiglu_partial(x.astype(BF16), wg_ref, wu_ref, wd_ref)
    o_ref[...] = _layer_norm(ALPHA * x + y, g_ref[...], b_ref[...])


def _dense_ffn(x, w_gate, w_up, w_down, g, b):
    t, d = x.shape
    d_ff = w_gate.shape[1]
    tm = TM_MIX
    const = lambda i: (0, 0)
    resident = dict(pipeline_mode=pl.Buffered(1))
    return pl.pallas_call(
        _dense_ffn_kernel,
        grid=(t // tm,),
        in_specs=[
            pl.BlockSpec((tm, d), lambda i: (i, 0)),
            pl.BlockSpec((d, d_ff), const, **resident),
            pl.BlockSpec((d, d_ff), const, **resident),
            pl.BlockSpec((d_ff, d), const, **resident),
            pl.BlockSpec((1, d), const),
            pl.BlockSpec((1, d), const),
        ],
        out_specs=pl.BlockSpec((tm, d), lambda i: (i, 0)),
        out_shape=jax.ShapeDtypeStruct((t, d), F32),
        compiler_params=_params("arbitrary"),
        name="dense_ffn",
    )(x, w_gate, w_up, w_down, g, b)


EXPERT_COL_SPLITS = 2


def _expert_ffn_kernel(te_ref, nv_ref, x_ref, wg_ref, wu_ref, wd_ref, o_ref):
    del te_ref
    valid = pl.program_id(0) < nv_ref[0]

    @pl.when(valid)
    def _():
        xb = _unpack_bf16_pairs(x_ref[...]).astype(BF16)
        piece = wg_ref.shape[1] // EXPERT_COL_SPLITS
        y = None
        for c in range(EXPERT_COL_SPLITS):
            cols = slice(c * piece, (c + 1) * piece)
            gate = jnp.dot(xb, wg_ref[:, cols], preferred_element_type=F32)
            up = jnp.dot(xb, wu_ref[:, cols], preferred_element_type=F32)
            h = (gate * jax.nn.sigmoid(gate) * up).astype(BF16)
            part = jnp.dot(h, wd_ref[cols, :], preferred_element_type=F32)
            y = part if y is None else y + part
        o_ref[...] = _pack_bf16_pairs(y)

    @pl.when(jnp.logical_not(valid))
    def _():
        o_ref[...] = jnp.zeros(o_ref.shape, U32)


def _expert_ffn(xs, tile_expert, n_valid, w_gate, w_up, w_down, layer):
    d = w_gate.shape[2]
    d_exp = w_gate.shape[3]
    tm = TM_EXP
    n_rows = xs.shape[0] // tm * tm
    weights_of_tile = lambda i, te, nv: (layer, te[i], 0, 0)
    resident = dict(pipeline_mode=pl.Buffered(1))
    return pl.pallas_call(
        _expert_ffn_kernel,
        grid_spec=pltpu.PrefetchScalarGridSpec(
            num_scalar_prefetch=2,
            grid=(n_rows // tm,),
            in_specs=[
                pl.BlockSpec((tm, d // 2), lambda i, te, nv: (i, 0)),
                pl.BlockSpec((None, None, d, d_exp), weights_of_tile, **resident),
                pl.BlockSpec((None, None, d, d_exp), weights_of_tile, **resident),
                pl.BlockSpec((None, None, d_exp, d), weights_of_tile, **resident),
            ],
            out_specs=pl.BlockSpec((tm, d // 2), lambda i, te, nv: (i, 0)),
        ),
        out_shape=jax.ShapeDtypeStruct((n_rows, d // 2), U32),
        compiler_params=_params("arbitrary"),
        name="expert_ffn",
    )(tile_expert, n_valid, xs, w_gate, w_up, w_down)


def _split_bf16(v):
    hi = v.astype(BF16)
    lo = (v - hi.astype(F32)).astype(BF16)
    return hi, lo


def _pack_bf16_pairs(v):
    w = v.shape[1] // 2
    lo = lax.bitcast_convert_type(v[:, :w].astype(BF16).astype(F32), U32)
    hi = lax.bitcast_convert_type(v[:, w:].astype(BF16).astype(F32), U32)
    return (lo >> 16) | hi


def _unpack_bf16_pairs(p):
    lo = lax.bitcast_convert_type(p << 16, F32)
    hi = lax.bitcast_convert_type(p & jnp.uint32(0xFFFF0000), F32)
    return jnp.concatenate([lo, hi], axis=1)


def _router_kernel(x_ref, wr_ref, idx_ref, gate_ref, xp_ref, rank_ref, count_ref, run_ref):
    xp_ref[...] = _pack_bf16_pairs(x_ref[...])
    xh, xl = _split_bf16(x_ref[...])
    wh, wl = _split_bf16(wr_ref[...])
    logits = (lax.dot_general(wh, xh, _NT, preferred_element_type=F32)
              + lax.dot_general(wh, xl, _NT, preferred_element_type=F32)
              + lax.dot_general(wl, xh, _NT, preferred_element_type=F32))
    e = lax.broadcasted_iota(I32, logits.shape, 0)
    m1 = jnp.max(logits, axis=0, keepdims=True)
    i1 = jnp.min(jnp.where(logits == m1, e, N_EXPERTS), axis=0, keepdims=True)
    rest = jnp.where(e == i1, -jnp.inf, logits)
    m2 = jnp.max(rest, axis=0, keepdims=True)
    i2 = jnp.min(jnp.where(rest == m2, e, N_EXPERTS), axis=0, keepdims=True)
    r = jnp.exp(m2 - m1)
    idx_ref[...] = jnp.concatenate([i1, i2], axis=0)
    gate_ref[...] = jnp.concatenate([1.0 / (1.0 + r), r / (1.0 + r)], axis=0)

    @pl.when(pl.program_id(0) == 0)
    def _():
        run_ref[...] = jnp.zeros(run_ref.shape, F32)

    tm = logits.shape[1]
    uses = jnp.where(jnp.logical_or(e == i1, e == i2), 1.0, 0.0)
    earlier = lax.broadcasted_iota(I32, (tm, tm), 0) < lax.broadcasted_iota(I32, (tm, tm), 1)
    prefix = jnp.dot(uses.astype(BF16), jnp.where(earlier, 1.0, 0.0).astype(BF16), preferred_element_type=F32)
    prefix = prefix + run_ref[:, 0:1]
    ranks = [jnp.sum(jnp.where(e == i, prefix, 0.0), axis=0, keepdims=True) for i in (i1, i2)]
    rank_ref[...] = jnp.concatenate(ranks, axis=0).astype(I32)
    run_ref[...] = run_ref[...] + jnp.sum(uses, axis=1, keepdims=True)
    count_ref[...] = run_ref[...]


def _router(x, w_router_t):
    t, d = x.shape
    tm = TM_WIDE
    return pl.pallas_call(
        _router_kernel,
        grid=(t // tm,),
        in_specs=[
            pl.BlockSpec((tm, d), lambda i: (i, 0)),
            pl.BlockSpec((N_EXPERTS, d), lambda i: (0, 0)),
        ],
        out_specs=[
            pl.BlockSpec((TOP_K, tm), lambda i: (0, i)),
            pl.BlockSpec((TOP_K, tm), lambda i: (0, i)),
            pl.BlockSpec((tm, d // 2), lambda i: (i, 0)),
            pl.BlockSpec((TOP_K, tm), lambda i: (0, i)),
            pl.BlockSpec((N_EXPERTS, LANES), lambda i: (0, 0)),
        ],
        out_shape=[jax.ShapeDtypeStruct((TOP_K, t), I32), jax.ShapeDtypeStruct((TOP_K, t), F32),
                   jax.ShapeDtypeStruct((t, d // 2), U32), jax.ShapeDtypeStruct((TOP_K, t), I32),
                   jax.ShapeDtypeStruct((N_EXPERTS, LANES), F32)],
        scratch_shapes=[pltpu.VMEM((N_EXPERTS, LANES), F32)],
        compiler_params=_params("arbitrary"),
        name="router",
    )(x, w_router_t)


def _moe_route(x, w_router):
    t, d = x.shape
    idx, gates, x_packed, rank, count = _router(x, w_router.T)

    counts = count[:, 0].astype(I32)
    padded = ((counts + TM_EXP - 1) // TM_EXP) * TM_EXP
    ends = jnp.cumsum(padded)
    experts = jnp.arange(N_EXPERTS, dtype=I32)[:, None, None]
    pos = rank + jnp.sum(jnp.where(idx[None] == experts, (ends - padded)[:, None, None], 0), axis=0)
    n_tiles = TOP_K * t // TM_EXP + N_EXPERTS
    tile_ids = jnp.arange(n_tiles, dtype=I32)
    tile_expert = jnp.minimum(
        jnp.sum((tile_ids[:, None] >= (ends // TM_EXP)[None, :]).astype(I32), axis=1), N_EXPERTS - 1).astype(I32)
    n_valid = (ends[-1:] // TM_EXP).astype(I32)

    n_rows = n_tiles * TM_EXP
    return dict(x_packed=x_packed, pos0=pos[0], pos1=pos[1], gates=gates.T, tile_expert=tile_expert, n_valid=n_valid,
                unused=_unused_rows(counts, padded, ends, n_rows), n_rows=n_rows)


def _moe_layer(streams, w_router, w_gate, w_up, w_down, layer, g, b):
    routes = [_moe_route(x, w_router) for x in streams]
    xs = [_sc_dispatch(r["x_packed"], r["pos0"], r["pos1"], r["unused"], r["n_rows"]) for r in routes]
    results = []
    for x, r, x_sorted in zip(streams, routes, xs):
        ys = _expert_ffn(x_sorted, r["tile_expert"], r["n_valid"], w_gate, w_up, w_down, layer)
        rows = x.shape[0] // COMBINE_PIECES
        gathered = [_sc_gather2(ys, r["pos0"][p * rows:(p + 1) * rows], r["pos1"][p * rows:(p + 1) * rows])
                    for p in range(COMBINE_PIECES)]
        out = None
        for p, (y0, y1) in enumerate(gathered):
            out = _moe_sum(x, y0, y1, r["gates"], g, b, p, out)
        results.append(out)
    return results


COMBINE_PIECES = 2
SC_CORES = 2
SC_SUBCORES = 16
SC_ROWS = 128


def _sc_worker_base(per_worker):
    return (lax.axis_index("subcore") * SC_CORES + lax.axis_index("core")) * per_worker


def _unused_rows(counts, padded, ends, n_rows):
    lane = jnp.arange(SC_ROWS, dtype=I32)
    chunk0 = jnp.arange(TM_EXP // SC_ROWS, dtype=I32) * SC_ROWS
    spare = n_rows + lane
    pad = (ends - padded + counts)[:, None, None] + chunk0[None, :, None] + lane[None, None, :]
    pad = jnp.where(pad < ends[:, None, None], pad, spare[None, None, :])
    tail = ends[-1] + jnp.arange(N_EXPERTS * TM_EXP // SC_ROWS, dtype=I32)[:, None] * SC_ROWS + lane[None, :]
    tail = jnp.where(tail < n_rows, tail, spare[None, :])
    return jnp.concatenate([pad.reshape(-1), tail.reshape(-1)]).astype(I32)


def _sc_dispatch(x, pos0, pos1, unused, n_rows):
    t, d = x.shape
    workers = SC_CORES * SC_SUBCORES
    per_worker = t // workers
    zero_chunks = unused.shape[0] // SC_ROWS // workers
    mesh = plsc.VectorSubcoreMesh(core_axis_name="core", subcore_axis_name="subcore")

    @pl.kernel(out_type=jax.ShapeDtypeStruct((n_rows + SC_ROWS, d), x.dtype), mesh=mesh,
               scratch_types=[pltpu.VMEM((SC_ROWS,), I32), pltpu.VMEM((SC_ROWS,), I32),
                              pltpu.VMEM((SC_ROWS, d), x.dtype)])
    def run(x_hbm, p0_hbm, p1_hbm, unused_hbm, zeros_hbm, o_hbm, i0_v, i1_v, rows_v):
        pltpu.sync_copy(zeros_hbm, rows_v)
        zbase = _sc_worker_base(zero_chunks * SC_ROWS)

        @pl.loop(0, zero_chunks)
        def _(c):
            pltpu.sync_copy(unused_hbm.at[pl.ds(zbase + c * SC_ROWS, SC_ROWS)], i0_v)
            pltpu.sync_copy(rows_v, o_hbm.at[i0_v])

        base = _sc_worker_base(per_worker)

        @pl.loop(0, per_worker // SC_ROWS)
        def _(c):
            off = base + c * SC_ROWS
            pltpu.sync_copy(p0_hbm.at[pl.ds(off, SC_ROWS)], i0_v)
            pltpu.sync_copy(p1_hbm.at[pl.ds(off, SC_ROWS)], i1_v)
            pltpu.sync_copy(x_hbm.at[pl.ds(off, SC_ROWS)], rows_v)
            pltpu.sync_copy(rows_v, o_hbm.at[i0_v])
            pltpu.sync_copy(rows_v, o_hbm.at[i1_v])

    return run(x, pos0, pos1, unused, jnp.zeros((SC_ROWS, d), x.dtype))


def _sc_gather2(ys, pos0, pos1):
    t = pos0.shape[0]
    d = ys.shape[1]
    per_worker = t // (SC_CORES * SC_SUBCORES)
    mesh = plsc.VectorSubcoreMesh(core_axis_name="core", subcore_axis_name="subcore")
    out = jax.ShapeDtypeStruct((t, d), ys.dtype)

    @pl.kernel(out_type=(out, out), mesh=mesh,
               scratch_types=[pltpu.VMEM((SC_ROWS,), I32), pltpu.VMEM((SC_ROWS, d), ys.dtype)])
    def run(y_hbm, p0_hbm, p1_hbm, o0_hbm, o1_hbm, i_v, rows_v):
        base = _sc_worker_base(per_worker)

        @pl.loop(0, per_worker // SC_ROWS)
        def _(c):
            off = base + c * SC_ROWS
            for p_hbm, o_hbm in ((p0_hbm, o0_hbm), (p1_hbm, o1_hbm)):
                pltpu.sync_copy(p_hbm.at[pl.ds(off, SC_ROWS)], i_v)
                pltpu.sync_copy(y_hbm.at[i_v], rows_v)
                pltpu.sync_copy(rows_v, o_hbm.at[pl.ds(off, SC_ROWS)])

    return run(ys, pos0, pos1)


def _moe_sum_kernel(x_ref, y0_ref, y1_ref, gate_ref, g_ref, b_ref, *rest):
    o_ref = rest[-1]
    gates = gate_ref[...]
    ff = gates[:, 0:1] * _unpack_bf16_pairs(y0_ref[...]) + gates[:, 1:2] * _unpack_bf16_pairs(y1_ref[...])
    o_ref[...] = _layer_norm(ALPHA * x_ref[...] + ff, g_ref[...], b_ref[...])


def _moe_sum(x, y0, y1, gates, g, b, piece, partial_out):
    t, d = x.shape
    rows = y0.shape[0]
    tm = TM_WIDE
    first = piece * rows // tm
    const = lambda i: (0, 0)
    row = lambda i: (i, 0)
    full_row = lambda i: (first + i, 0)
    in_specs = [pl.BlockSpec((tm, d), full_row), pl.BlockSpec((tm, d // 2), row), pl.BlockSpec((tm, d // 2), row),
                pl.BlockSpec((tm, TOP_K), full_row), pl.BlockSpec((1, d), const), pl.BlockSpec((1, d), const)]
    operands = [x, y0, y1, gates, g, b]
    aliases = {}
    if partial_out is not None:
        in_specs.append(pl.BlockSpec(memory_space=pl.ANY))
        operands.append(partial_out)
        aliases = {len(operands) - 1: 0}
    return pl.pallas_call(
        _moe_sum_kernel,
        grid=(rows // tm,),
        in_specs=in_specs,
        out_specs=pl.BlockSpec((tm, d), full_row),
        out_shape=jax.ShapeDtypeStruct((t, d), F32),
        input_output_aliases=aliases,
        compiler_params=_params("arbitrary"),
        name="moe_sum",
    )(*operands)


MAX_QBLKS_PER_STEP = 8


def _attn_kernel(q_ref, kp_ref, kc_ref, vp_ref, vc_ref, o_ref, m_ref, l_ref, kcat_ref, vcat_ref):
    n_pairs = q_ref.shape[-1] // LANES
    qi = lax.broadcasted_iota(I32, (QBLK, 2 * QBLK), 0)
    ki = lax.broadcasted_iota(I32, (QBLK, 2 * QBLK), 1)
    in_band = ki - QBLK <= qi
    lane = lax.broadcasted_iota(I32, (QBLK, LANES), 1)
    low_half = lane < HEAD_DIM
    for pr in range(n_pairs):
        vcat_ref[:, (2 * pr + 1) * LANES:(2 * pr + 2) * LANES] = jnp.ones((vcat_ref.shape[0], LANES), BF16)
    m_ref[...] = jnp.zeros(m_ref.shape, F32)
    l_ref[...] = jnp.ones(l_ref.shape, F32)

    def residue_class(r, carry):
        kcat_ref[0:QBLK, :] = kp_ref[r]
        kcat_ref[QBLK:, :] = kc_ref[r]
        for pr in range(n_pairs):
            vcat_ref[0:QBLK, 2 * pr * LANES:(2 * pr + 1) * LANES] = vp_ref[r, :, pr * LANES:(pr + 1) * LANES]
            vcat_ref[QBLK:, 2 * pr * LANES:(2 * pr + 1) * LANES] = vc_ref[r, :, pr * LANES:(pr + 1) * LANES]
        lax.fori_loop(0, q_ref.shape[1] // QBLK,
                      functools.partial(block, q_ref.at[r], o_ref.at[r], m_ref.at[r], l_ref.at[r]), 0, unroll=2)
        return carry

    def block(q_r, o_r, m_r, l_r, j, carry):
        row0 = pl.multiple_of(j * QBLK, QBLK)
        rows = pl.ds(row0, QBLK)
        keys = pl.ds(row0, 2 * QBLK)
        has_prev = jnp.logical_or(j > 0, pl.program_id(2) > 0)
        first_ok = qi + jnp.where(has_prev, 0, QBLK)
        mask = jnp.logical_and(jnp.logical_or(ki >= first_ok, ki >= QBLK), in_band)
        for pr in range(n_pairs):
            cols = slice(pr * LANES, (pr + 1) * LANES)
            q2 = q_r[rows, cols]
            k2 = kcat_ref[keys, cols]
            v_ext = vcat_ref[keys, 2 * pr * LANES:(2 * pr + 2) * LANES]
            o_pair = None
            for a in range(2):
                own = low_half if a == 0 else jnp.logical_not(low_half)
                qm = jnp.where(own, q2, jnp.zeros_like(q2))
                s = lax.dot_general(qm, k2, _NT, preferred_element_type=F32)
                s = jnp.where(mask, s, NEG_BIG)
                m = jnp.max(s, axis=-1, keepdims=True)
                p = jnp.exp(s - m).astype(BF16)
                oe = jnp.dot(p, v_ext, preferred_element_type=F32)
                o_pair = oe[:, :LANES] if a == 0 else jnp.where(low_half, o_pair, oe[:, :LANES])
                h = 2 * pr + a
                m_r[rows, h:h + 1] = m
                l_r[rows, h:h + 1] = oe[:, LANES + h:LANES + h + 1]
            o_r[rows, cols] = o_pair.astype(o_r.dtype)
        return carry

    lax.fori_loop(0, q_ref.shape[0], residue_class, 0)


def _group_attention(q, k, v):
    bsz, dil, length, width = q.shape
    blocks_per_step = min(MAX_QBLKS_PER_STEP, length // QBLK)
    classes_per_step = min(dil, MAX_QBLKS_PER_STEP // blocks_per_step)
    step = blocks_per_step * QBLK
    blk = (None, classes_per_step, step, width)
    cur = lambda b, r, n: (b, r, n, 0)
    prev_blk = (None, classes_per_step, QBLK, width)
    prev = lambda b, r, n: (b, r, jnp.maximum(blocks_per_step * n - 1, 0), 0)
    stat_spec = pl.BlockSpec((None, classes_per_step, step, LANES), cur)
    stat_shape = jax.ShapeDtypeStruct((bsz, dil, length, LANES), F32)
    return pl.pallas_call(
        _attn_kernel,
        grid=(bsz, dil // classes_per_step, length // step),
        in_specs=[
            pl.BlockSpec(blk, cur),
            pl.BlockSpec(prev_blk, prev),
            pl.BlockSpec(blk, cur),
            pl.BlockSpec(prev_blk, prev),
            pl.BlockSpec(blk, cur),
        ],
        out_specs=[pl.BlockSpec(blk, cur), stat_spec, stat_spec],
        out_shape=[jax.ShapeDtypeStruct((bsz, dil, length, width), BF16), stat_shape, stat_shape],
        scratch_shapes=[pltpu.VMEM((step + QBLK, width), BF16), pltpu.VMEM((step + QBLK, 2 * width), BF16)],
        compiler_params=_params("arbitrary", "arbitrary", "arbitrary"),
        name="group_attention",
    )(q, k, k, v, v)


def _to_token_order(src_ref, dst_ref):
    dil, n, w = src_ref.shape
    for r in range(dil):
        for c in range(w // LANES):
            dst_ref[c, pl.ds(r, n, stride=dil), :] = src_ref[r, :, c * LANES:(c + 1) * LANES].astype(dst_ref.dtype)
    return jnp.concatenate([dst_ref[c] for c in range(w // LANES)], axis=1)


def _attn_out_kernel(x_ref, o0_ref, o1_ref, o2_ref, m0_ref, m1_ref, m2_ref, l0_ref, l1_ref, l2_ref,
                     wo_ref, g_ref, b_ref, out_ref, o_tok, stat_tok):
    width = o0_ref.shape[-1]

    def token_order(ref, scratch):
        if ref.shape[0] == 1:
            return ref[0].astype(F32)
        return _to_token_order(ref, scratch)

    ms = [token_order(r, stat_tok.at[i]) for i, r in enumerate((m0_ref, m1_ref, m2_ref))]
    ls = [token_order(r, stat_tok.at[N_GROUPS + i]) for i, r in enumerate((l0_ref, l1_ref, l2_ref))]
    top = jnp.maximum(jnp.maximum(ms[0], ms[1]), ms[2])
    es = [jnp.exp(m - top) for m in ms]
    den = es[0] * ls[0] + es[1] * ls[1] + es[2] * ls[2]
    head = lax.broadcasted_iota(I32, (LANES, width), 0)
    lane = lax.broadcasted_iota(I32, (LANES, width), 1)
    spread = (lane // HEAD_DIM == head).astype(BF16)
    mixed = jnp.zeros(out_ref.shape[:1] + (width,), F32)
    for e, o_ref in zip(es, (o0_ref, o1_ref, o2_ref)):
        hi, lo = _split_bf16(e / den)
        wide = (jnp.dot(hi, spread, preferred_element_type=F32)
                + jnp.dot(lo, spread, preferred_element_type=F32))
        mixed = mixed + wide * token_order(o_ref, o_tok)
    y = jnp.dot(mixed.astype(BF16), wo_ref[...], preferred_element_type=F32)
    out_ref[...] = _layer_norm(ALPHA * x_ref[...] + y, g_ref[...], b_ref[...])


def _attn_out(x, bsz, outs, stats, w_o, g, b):
    t, d = x.shape
    width = w_o.shape[0]
    tm = TM_RES
    tiles_per_seq = t // bsz // tm
    const = lambda bi, c: (0, 0)
    row = lambda bi, c: (bi * tiles_per_seq + c, 0)

    def res_spec(a):
        dil, w = a.shape[1], a.shape[3]
        return pl.BlockSpec((None, dil, tm // dil, w), lambda bi, c: (bi, 0, c, 0))

    return pl.pallas_call(
        _attn_out_kernel,
        grid=(bsz, tiles_per_seq),
        in_specs=([pl.BlockSpec((tm, d), row)] + [res_spec(a) for a in outs] + [res_spec(a) for a in stats]
                  + [pl.BlockSpec((width, d), const), pl.BlockSpec((1, d), const), pl.BlockSpec((1, d), const)]),
        out_specs=pl.BlockSpec((tm, d), row),
        out_shape=jax.ShapeDtypeStruct((t, d), F32),
        scratch_shapes=[pltpu.VMEM((width // LANES, tm, LANES), F32), pltpu.VMEM((2 * N_GROUPS, 1, tm, LANES), F32)],
        compiler_params=_params("arbitrary", "arbitrary"),
        name="attn_out",
    )(x, *outs, *stats, w_o, g, b)


def _proj_residue_kernel(x_ref, w_ref, *rest, dils, scale):
    outs, y_ref = rest[:-1], rest[-1]
    _, n_slabs, tm, _ = y_ref.shape
    width = n_slabs * LANES
    xb = x_ref[...].astype(BF16)
    for c, (o_ref, dil) in enumerate(zip(outs, dils)):
        y = jnp.dot(xb, w_ref[:, c * width:(c + 1) * width], preferred_element_type=F32)
        if scale != 1.0:
            y = y * scale
        if dil == 1:
            o_ref[0] = y.astype(o_ref.dtype)
        else:
            for s in range(n_slabs):
                y_ref[c, s] = y[:, s * LANES:(s + 1) * LANES]
            for r in range(dil):
                rows = [y_ref[c, s, pl.ds(r, tm // dil, stride=dil), :] for s in range(n_slabs)]
                o_ref[r] = jnp.concatenate(rows, axis=1).astype(o_ref.dtype)


def _proj_residue(x, bsz, w, dils, scale=1.0):
    t, d = x.shape
    seq = t // bsz
    width = w.shape[1] // len(dils)
    tm = TM_WIDE
    tiles_per_seq = seq // tm
    return pl.pallas_call(
        functools.partial(_proj_residue_kernel, dils=dils, scale=scale),
        grid=(bsz, tiles_per_seq),
        in_specs=[pl.BlockSpec((tm, d), lambda bi, c: (bi * tiles_per_seq + c, 0)),
                  pl.BlockSpec(w.shape, lambda bi, c: (0, 0))],
        out_specs=[pl.BlockSpec((None, dil, tm // dil, width), lambda bi, c: (bi, 0, c, 0)) for dil in dils],
        out_shape=[jax.ShapeDtypeStruct((bsz, dil, seq // dil, width), BF16) for dil in dils],
        scratch_shapes=[pltpu.VMEM((len(dils), width // LANES, tm, LANES), F32)],
        compiler_params=_params("arbitrary", "arbitrary"),
        name="proj_residue",
    )(x, w)


def _attention_layer(x, bsz, w_q, w_o, kv_groups, g, b):
    dils = tuple(dil for _, dil in ATT_GROUPS)
    qs = _proj_residue(x, bsz, w_q, dils, scale=HEAD_DIM ** -0.5)
    outs, maxes, dens = [], [], []
    for q, (k_res, v_res) in zip(qs, kv_groups):
        o, m, l = _group_attention(q, k_res, v_res)
        outs.append(o)
        maxes.append(m)
        dens.append(l)
    return _attn_out(x, bsz, outs, maxes + dens, w_o, g, b)


def kernel(x, a_w_in, a_conv, a_w_out, kv_w, b_w_q, b_w_o, ffn_w_gate, ffn_w_up, ffn_w_down,
           moe_w_router, moe_w_gate, moe_w_up, moe_w_down, ln_g, ln_b):
    bsz, seq, d = x.shape
    sb = bsz // N_STREAMS
    hs = [x[s * sb:(s + 1) * sb].reshape(sb * seq, d) for s in range(N_STREAMS)]
    ln_g = ln_g.reshape(DEPTH, 2, 1, d)
    ln_b = ln_b.reshape(DEPTH, 2, 1, d)
    kv_dils = tuple(dil for _, dil in ATT_GROUPS for _ in range(2))
    kv_groups = None
    for l in range(DEPTH):
        if l < N_A_LAYERS:
            w_in, w_out = a_w_in[l].astype(BF16), a_w_out[l].astype(BF16)
            hs = [_a_mixer(h, w_in, a_conv[l], w_out, ln_g[l, 0], ln_b[l, 0], seq) for h in hs]
        else:
            if kv_groups is None:
                kv_wb = kv_w.astype(BF16)
                kvs = [_proj_residue(h, sb, kv_wb, kv_dils) for h in hs]
                kv_groups = [[(kv[2 * gi], kv[2 * gi + 1]) for gi in range(N_GROUPS)] for kv in kvs]
            j = l - N_A_LAYERS
            w_q, w_o = b_w_q[j].astype(BF16), b_w_o[j].astype(BF16)
            hs = [_attention_layer(h, sb, w_q, w_o, kvg, ln_g[l, 0], ln_b[l, 0]) for h, kvg in zip(hs, kv_groups)]
        i = l // 2
        if l % 2 == 0:
            w_gate, w_up, w_down = (w[i].astype(BF16) for w in (ffn_w_gate, ffn_w_up, ffn_w_down))
            hs = [_dense_ffn(h, w_gate, w_up, w_down, ln_g[l, 1], ln_b[l, 1]) for h in hs]
        else:
            hs = _moe_layer(hs, moe_w_router[i], moe_w_gate.astype(BF16), moe_w_up.astype(BF16),
                            moe_w_down.astype(BF16), i, ln_g[l, 1], ln_b[l, 1])
    return jnp.concatenate(hs, axis=0).reshape(bsz, seq, d)
```

```python
import functools

import jax
import jax.numpy as jnp
from jax import lax
from jax.experimental import pallas as pl
from jax.experimental.pallas import tpu as pltpu
from jax.experimental.pallas import tpu_sc as plsc

F32 = jnp.float32
U32 = jnp.uint32
BF16 = jnp.bfloat16
I32 = jnp.int32

DEPTH = 4
N_A_LAYERS = DEPTH // 2
CONV_WIDTH = 3
ATT_GROUPS = ((128, 1), (512, 4), (2048, 16))
N_GROUPS = len(ATT_GROUPS)
HEAD_DIM = 64
N_EXPERTS = 8
TOP_K = 2
ALPHA = (2.0 * DEPTH) ** 0.25
LN_EPS = 1e-5

QBLK = 128
assert all(w // d == QBLK for w, d in ATT_GROUPS)

LANES = 128
SUBLANES = 8
VMEM_LIMIT = 56 * 1024 * 1024
NEG_BIG = -1e30

TM_MIX = 512
TM_EXP = 512
TM_RES = 512
TM_WIDE = 1024
N_STREAMS = 1

_NT = (((1,), (1,)), ((), ()))


def _layer_norm(z, g, b):
    mu = jnp.mean(z, axis=-1, keepdims=True)
    zc = z - mu
    var = jnp.mean(zc * zc, axis=-1, keepdims=True)
    return zc * lax.rsqrt(var + LN_EPS) * g + b


def _params(*semantics):
    return pltpu.CompilerParams(dimension_semantics=semantics, vmem_limit_bytes=VMEM_LIMIT)


def _a_mixer_kernel(x_ref, win_ref, conv_ref, wout_ref, g_ref, b_ref, o_ref, ubuf, *, tiles_per_seq):
    tm, d = x_ref.shape
    piece = tm // A_MIXER_PIECES

    @pl.when(pl.program_id(0) % tiles_per_seq == 0)
    def _():
        ubuf[0:SUBLANES, :] = jnp.zeros((SUBLANES, d), F32)

    cw = conv_ref[...]
    for h in range(A_MIXER_PIECES):
        rows = slice(h * piece, (h + 1) * piece)
        first = SUBLANES + h * piece
        x = x_ref[rows, :]
        p = jnp.dot(x.astype(BF16), win_ref[...], preferred_element_type=F32)
        ubuf[first:first + piece, :] = p[:, d:2 * d] * p[:, 2 * d:]
        conv = (cw[2:3, :] * ubuf[first:first + piece, :]
                + cw[1:2, :] * ubuf[first - 1:first - 1 + piece, :]
                + cw[0:1, :] * ubuf[first - 2:first - 2 + piece, :])
        y = jnp.dot((p[:, :d] * conv).astype(BF16), wout_ref[...], preferred_element_type=F32)
        o_ref[rows, :] = _layer_norm(ALPHA * x + y, g_ref[...], b_ref[...])
    ubuf[0:SUBLANES, :] = ubuf[tm:tm + SUBLANES, :]


A_MIXER_PIECES = 2


def _a_mixer(x, w_in, conv_w, w_out, g, b, seq_len):
    t, d = x.shape
    tm = TM_WIDE
    const = lambda i: (0, 0)
    resident = dict(pipeline_mode=pl.Buffered(1))
    return pl.pallas_call(
        functools.partial(_a_mixer_kernel, tiles_per_seq=seq_len // tm),
        grid=(t // tm,),
        in_specs=[
            pl.BlockSpec((tm, d), lambda i: (i, 0)),
            pl.BlockSpec((d, 3 * d), const, **resident),
            pl.BlockSpec((CONV_WIDTH, d), const),
            pl.BlockSpec((d, d), const, **resident),
            pl.BlockSpec((1, d), const),
            pl.BlockSpec((1, d), const),
        ],
        out_specs=pl.BlockSpec((tm, d), lambda i: (i, 0)),
        out_shape=jax.ShapeDtypeStruct((t, d), F32),
        scratch_shapes=[pltpu.VMEM((tm + SUBLANES, d), F32)],
        compiler_params=_params("arbitrary"),
        name="a_mixer",
    )(x, w_in, conv_w, w_out, g, b)


def _swiglu_partial(xb, wg_ref, wu_ref, wd_ref):
    gate = jnp.dot(xb, wg_ref[...], preferred_element_type=F32)
    up = jnp.dot(xb, wu_ref[...], preferred_element_type=F32)
    h = (gate * jax.nn.sigmoid(gate) * up).astype(BF16)
    return jnp.dot(h, wd_ref[...], preferred_element_type=F32)


def _dense_ffn_kernel(x_ref, wg_ref, wu_ref, wd_ref, g_ref, b_ref, o_ref):
    x = x_ref[...]
    y = _swiglu_partial(x.astype(BF16), wg_ref, wu_ref, wd_ref)
    o_ref[...] = _layer_norm(ALPHA * x + y, g_ref[...], b_ref[...])


def _dense_ffn(x, w_gate, w_up, w_down, g, b):
    t, d = x.shape
    d_ff = w_gate.shape[1]
    tm = TM_MIX
    const = lambda i: (0, 0)
    resident = dict(pipeline_mode=pl.Buffered(1))
    return pl.pallas_call(
        _dense_ffn_kernel,
        grid=(t // tm,),
        in_specs=[
            pl.BlockSpec((tm, d), lambda i: (i, 0)),
            pl.BlockSpec((d, d_ff), const, **resident),
            pl.BlockSpec((d, d_ff), const, **resident),
            pl.BlockSpec((d_ff, d), const, **resident),
            pl.BlockSpec((1, d), const),
            pl.BlockSpec((1, d), const),
        ],
        out_specs=pl.BlockSpec((tm, d), lambda i: (i, 0)),
        out_shape=jax.ShapeDtypeStruct((t, d), F32),
        compiler_params=_params("arbitrary"),
        name="dense_ffn",
    )(x, w_gate, w_up, w_down, g, b)


EXPERT_COL_SPLITS = 2


def _expert_ffn_kernel(te_ref, nv_ref, x_ref, wg_ref, wu_ref, wd_ref, o_ref):
    del te_ref
    valid = pl.program_id(0) < nv_ref[0]

    @pl.when(valid)
    def _():
        xb = _unpack_bf16_pairs(x_ref[...]).astype(BF16)
        piece = wg_ref.shape[1] // EXPERT_COL_SPLITS
        y = None
        for c in range(EXPERT_COL_SPLITS):
            cols = slice(c * piece, (c + 1) * piece)
            gate = jnp.dot(xb, wg_ref[:, cols], preferred_element_type=F32)
            up = jnp.dot(xb, wu_ref[:, cols], preferred_element_type=F32)
            h = (gate * jax.nn.sigmoid(gate) * up).astype(BF16)
            part = jnp.dot(h, wd_ref[cols, :], preferred_element_type=F32)
            y = part if y is None else y + part
        o_ref[...] = _pack_bf16_pairs(y)

    @pl.when(jnp.logical_not(valid))
    def _():
        o_ref[...] = jnp.zeros(o_ref.shape, U32)


def _expert_ffn(xs, tile_expert, n_valid, w_gate, w_up, w_down, layer):
    d = w_gate.shape[2]
    d_exp = w_gate.shape[3]
    tm = TM_EXP
    n_rows = xs.shape[0] // tm * tm
    weights_of_tile = lambda i, te, nv: (layer, te[i], 0, 0)
    resident = dict(pipeline_mode=pl.Buffered(1))
    return pl.pallas_call(
        _expert_ffn_kernel,
        grid_spec=pltpu.PrefetchScalarGridSpec(
            num_scalar_prefetch=2,
            grid=(n_rows // tm,),
            in_specs=[
                pl.BlockSpec((tm, d // 2), lambda i, te, nv: (i, 0)),
                pl.BlockSpec((None, None, d, d_exp), weights_of_tile, **resident),
                pl.BlockSpec((None, None, d, d_exp), weights_of_tile, **resident),
                pl.BlockSpec((None, None, d_exp, d), weights_of_tile, **resident),
            ],
            out_specs=pl.BlockSpec((tm, d // 2), lambda i, te, nv: (i, 0)),
        ),
        out_shape=jax.ShapeDtypeStruct((n_rows, d // 2), U32),
        compiler_params=_params("arbitrary"),
        name="expert_ffn",
    )(tile_expert, n_valid, xs, w_gate, w_up, w_down)


def _split_bf16(v):
    hi = v.astype(BF16)
    lo = (v - hi.astype(F32)).astype(BF16)
    return hi, lo


def _pack_bf16_pairs(v):
    w = v.shape[1] // 2
    lo = lax.bitcast_convert_type(v[:, :w].astype(BF16).astype(F32), U32)
    hi = lax.bitcast_convert_type(v[:, w:].astype(BF16).astype(F32), U32)
    return (lo >> 16) | hi


def _unpack_bf16_pairs(p):
    lo = lax.bitcast_convert_type(p << 16, F32)
    hi = lax.bitcast_convert_type(p & jnp.uint32(0xFFFF0000), F32)
    return jnp.concatenate([lo, hi], axis=1)


def _router_kernel(x_ref, wr_ref, idx_ref, gate_ref, xp_ref, rank_ref, count_ref, run_ref):
    xp_ref[...] = _pack_bf16_pairs(x_ref[...])
    xh, xl = _split_bf16(x_ref[...])
    wh, wl = _split_bf16(wr_ref[...])
    logits = (lax.dot_general(wh, xh, _NT, preferred_element_type=F32)
              + lax.dot_general(wh, xl, _NT, preferred_element_type=F32)
              + lax.dot_general(wl, xh, _NT, preferred_element_type=F32))
    e = lax.broadcasted_iota(I32, logits.shape, 0)
    m1 = jnp.max(logits, axis=0, keepdims=True)
    i1 = jnp.min(jnp.where(logits == m1, e, N_EXPERTS), axis=0, keepdims=True)
    rest = jnp.where(e == i1, -jnp.inf, logits)
    m2 = jnp.max(rest, axis=0, keepdims=True)
    i2 = jnp.min(jnp.where(rest == m2, e, N_EXPERTS), axis=0, keepdims=True)
    r = jnp.exp(m2 - m1)
    idx_ref[...] = jnp.concatenate([i1, i2], axis=0)
    gate_ref[...] = jnp.concatenate([1.0 / (1.0 + r), r / (1.0 + r)], axis=0)

    @pl.when(pl.program_id(0) == 0)
    def _():
        run_ref[...] = jnp.zeros(run_ref.shape, F32)

    tm = logits.shape[1]
    uses = jnp.where(jnp.logical_or(e == i1, e == i2), 1.0, 0.0)
    earlier = lax.broadcasted_iota(I32, (tm, tm), 0) < lax.broadcasted_iota(I32, (tm, tm), 1)
    prefix = jnp.dot(uses.astype(BF16), jnp.where(earlier, 1.0, 0.0).astype(BF16), preferred_element_type=F32)
    prefix = prefix + run_ref[:, 0:1]
    ranks = [jnp.sum(jnp.where(e == i, prefix, 0.0), axis=0, keepdims=True) for i in (i1, i2)]
    rank_ref[...] = jnp.concatenate(ranks, axis=0).astype(I32)
    run_ref[...] = run_ref[...] + jnp.sum(uses, axis=1, keepdims=True)
    count_ref[...] = run_ref[...]


def _router(x, w_router_t):
    t, d = x.shape
    tm = TM_WIDE
    return pl.pallas_call(
        _router_kernel,
        grid=(t // tm,),
        in_specs=[
            pl.BlockSpec((tm, d), lambda i: (i, 0)),
            pl.BlockSpec((N_EXPERTS, d), lambda i: (0, 0)),
        ],
        out_specs=[
            pl.BlockSpec((TOP_K, tm), lambda i: (0, i)),
            pl.BlockSpec((TOP_K, tm), lambda i: (0, i)),
            pl.BlockSpec((tm, d // 2), lambda i: (i, 0)),
            pl.BlockSpec((TOP_K, tm), lambda i: (0, i)),
            pl.BlockSpec((N_EXPERTS, LANES), lambda i: (0, 0)),
        ],
        out_shape=[jax.ShapeDtypeStruct((TOP_K, t), I32), jax.ShapeDtypeStruct((TOP_K, t), F32),
                   jax.ShapeDtypeStruct((t, d // 2), U32), jax.ShapeDtypeStruct((TOP_K, t), I32),
                   jax.ShapeDtypeStruct((N_EXPERTS, LANES), F32)],
        scratch_shapes=[pltpu.VMEM((N_EXPERTS, LANES), F32)],
        compiler_params=_params("arbitrary"),
        name="router",
    )(x, w_router_t)


def _moe_route(x, w_router):
    t, d = x.shape
    idx, gates, x_packed, rank, count = _router(x, w_router.T)

    counts = count[:, 0].astype(I32)
    padded = ((counts + TM_EXP - 1) // TM_EXP) * TM_EXP
    ends = jnp.cumsum(padded)
    experts = jnp.arange(N_EXPERTS, dtype=I32)[:, None, None]
    pos = rank + jnp.sum(jnp.where(idx[None] == experts, (ends - padded)[:, None, None], 0), axis=0)
    n_tiles = TOP_K * t // TM_EXP + N_EXPERTS
    tile_ids = jnp.arange(n_tiles, dtype=I32)
    tile_expert = jnp.minimum(
        jnp.sum((tile_ids[:, None] >= (ends // TM_EXP)[None, :]).astype(I32), axis=1), N_EXPERTS - 1).astype(I32)
    n_valid = (ends[-1:] // TM_EXP).astype(I32)

    n_rows = n_tiles * TM_EXP
    return dict(x_packed=x_packed, pos0=pos[0], pos1=pos[1], gates=gates.T, tile_expert=tile_expert, n_valid=n_valid,
                unused=_unused_rows(counts, padded, ends, n_rows), n_rows=n_rows)


def _moe_layer(streams, w_router, w_gate, w_up, w_down, layer, g, b):
    routes = [_moe_route(x, w_router) for x in streams]
    xs = [_sc_dispatch(r["x_packed"], r["pos0"], r["pos1"], r["unused"], r["n_rows"]) for r in routes]
    results = []
    for x, r, x_sorted in zip(streams, routes, xs):
        ys = _expert_ffn(x_sorted, r["tile_expert"], r["n_valid"], w_gate, w_up, w_down, layer)
        rows = x.shape[0] // COMBINE_PIECES
        gathered = [_sc_gather2(ys, r["pos0"][p * rows:(p + 1) * rows], r["pos1"][p * rows:(p + 1) * rows])
                    for p in range(COMBINE_PIECES)]
        out = None
        for p, (y0, y1) in enumerate(gathered):
            out = _moe_sum(x, y0, y1, r["gates"], g, b, p, out)
        results.append(out)
    return results


COMBINE_PIECES = 2
SC_CORES = 2
SC_SUBCORES = 16
SC_ROWS = 128


def _sc_worker_base(per_worker):
    return (lax.axis_index("subcore") * SC_CORES + lax.axis_index("core")) * per_worker


def _unused_rows(counts, padded, ends, n_rows):
    lane = jnp.arange(SC_ROWS, dtype=I32)
    chunk0 = jnp.arange(TM_EXP // SC_ROWS, dtype=I32) * SC_ROWS
    spare = n_rows + lane
    pad = (ends - padded + counts)[:, None, None] + chunk0[None, :, None] + lane[None, None, :]
    pad = jnp.where(pad < ends[:, None, None], pad, spare[None, None, :])
    tail = ends[-1] + jnp.arange(N_EXPERTS * TM_EXP // SC_ROWS, dtype=I32)[:, None] * SC_ROWS + lane[None, :]
    tail = jnp.where(tail < n_rows, tail, spare[None, :])
    return jnp.concatenate([pad.reshape(-1), tail.reshape(-1)]).astype(I32)


def _sc_dispatch(x, pos0, pos1, unused, n_rows):
    t, d = x.shape
    workers = SC_CORES * SC_SUBCORES
    per_worker = t // workers
    zero_chunks = unused.shape[0] // SC_ROWS // workers
    mesh = plsc.VectorSubcoreMesh(core_axis_name="core", subcore_axis_name="subcore")

    @pl.kernel(out_type=jax.ShapeDtypeStruct((n_rows + SC_ROWS, d), x.dtype), mesh=mesh,
               scratch_types=[pltpu.VMEM((SC_ROWS,), I32), pltpu.VMEM((SC_ROWS,), I32),
                              pltpu.VMEM((SC_ROWS, d), x.dtype)])
    def run(x_hbm, p0_hbm, p1_hbm, unused_hbm, zeros_hbm, o_hbm, i0_v, i1_v, rows_v):
        pltpu.sync_copy(zeros_hbm, rows_v)
        zbase = _sc_worker_base(zero_chunks * SC_ROWS)

        @pl.loop(0, zero_chunks)
        def _(c):
            pltpu.sync_copy(unused_hbm.at[pl.ds(zbase + c * SC_ROWS, SC_ROWS)], i0_v)
            pltpu.sync_copy(rows_v, o_hbm.at[i0_v])

        base = _sc_worker_base(per_worker)

        @pl.loop(0, per_worker // SC_ROWS)
        def _(c):
            off = base + c * SC_ROWS
            pltpu.sync_copy(p0_hbm.at[pl.ds(off, SC_ROWS)], i0_v)
            pltpu.sync_copy(p1_hbm.at[pl.ds(off, SC_ROWS)], i1_v)
            pltpu.sync_copy(x_hbm.at[pl.ds(off, SC_ROWS)], rows_v)
            pltpu.sync_copy(rows_v, o_hbm.at[i0_v])
            pltpu.sync_copy(rows_v, o_hbm.at[i1_v])

    return run(x, pos0, pos1, unused, jnp.zeros((SC_ROWS, d), x.dtype))


def _sc_gather2(ys, pos0, pos1):
    t = pos0.shape[0]
    d = ys.shape[1]
    per_worker = t // (SC_CORES * SC_SUBCORES)
    mesh = plsc.VectorSubcoreMesh(core_axis_name="core", subcore_axis_name="subcore")
    out = jax.ShapeDtypeStruct((t, d), ys.dtype)

    @pl.kernel(out_type=(out, out), mesh=mesh,
               scratch_types=[pltpu.VMEM((SC_ROWS,), I32), pltpu.VMEM((SC_ROWS, d), ys.dtype)])
    def run(y_hbm, p0_hbm, p1_hbm, o0_hbm, o1_hbm, i_v, rows_v):
        base = _sc_worker_base(per_worker)

        @pl.loop(0, per_worker // SC_ROWS)
        def _(c):
            off = base + c * SC_ROWS
            for p_hbm, o_hbm in ((p0_hbm, o0_hbm), (p1_hbm, o1_hbm)):
                pltpu.sync_copy(p_hbm.at[pl.ds(off, SC_ROWS)], i_v)
                pltpu.sync_copy(y_hbm.at[i_v], rows_v)
                pltpu.sync_copy(rows_v, o_hbm.at[pl.ds(off, SC_ROWS)])

    return run(ys, pos0, pos1)


def _moe_sum_kernel(x_ref, y0_ref, y1_ref, gate_ref, g_ref, b_ref, *rest):
    o_ref = rest[-1]
    gates = gate_ref[...]
    ff = gates[:, 0:1] * _unpack_bf16_pairs(y0_ref[...]) + gates[:, 1:2] * _unpack_bf16_pairs(y1_ref[...])
    o_ref[...] = _layer_norm(ALPHA * x_ref[...] + ff, g_ref[...], b_ref[...])


def _moe_sum(x, y0, y1, gates, g, b, piece, partial_out):
    t, d = x.shape
    rows = y0.shape[0]
    tm = TM_WIDE
    first = piece * rows // tm
    const = lambda i: (0, 0)
    row = lambda i: (i, 0)
    full_row = lambda i: (first + i, 0)
    in_specs = [pl.BlockSpec((tm, d), full_row), pl.BlockSpec((tm, d // 2), row), pl.BlockSpec((tm, d // 2), row),
                pl.BlockSpec((tm, TOP_K), full_row), pl.BlockSpec((1, d), const), pl.BlockSpec((1, d), const)]
    operands = [x, y0, y1, gates, g, b]
    aliases = {}
    if partial_out is not None:
        in_specs.append(pl.BlockSpec(memory_space=pl.ANY))
        operands.append(partial_out)
        aliases = {len(operands) - 1: 0}
    return pl.pallas_call(
        _moe_sum_kernel,
        grid=(rows // tm,),
        in_specs=in_specs,
        out_specs=pl.BlockSpec((tm, d), full_row),
        out_shape=jax.ShapeDtypeStruct((t, d), F32),
        input_output_aliases=aliases,
        compiler_params=_params("arbitrary"),
        name="moe_sum",
    )(*operands)


MAX_QBLKS_PER_STEP = 8


def _attn_kernel(q_ref, kp_ref, kc_ref, vp_ref, vc_ref, o_ref, m_ref, l_ref, kcat_ref, vcat_ref):
    n_pairs = q_ref.shape[-1] // LANES
    qi = lax.broadcasted_iota(I32, (QBLK, 2 * QBLK), 0)
    ki = lax.broadcasted_iota(I32, (QBLK, 2 * QBLK), 1)
    in_band = ki - QBLK <= qi
    lane = lax.broadcasted_iota(I32, (QBLK, LANES), 1)
    low_half = lane < HEAD_DIM
    for pr in range(n_pairs):
        vcat_ref[:, :, (2 * pr + 1) * LANES:(2 * pr + 2) * LANES] = jnp.ones(vcat_ref.shape[:2] + (LANES,), BF16)
    m_ref[...] = jnp.zeros(m_ref.shape, F32)
    l_ref[...] = jnp.ones(l_ref.shape, F32)

    def block(r, j):
        q_r, o_r, m_r, l_r = q_ref.at[r], o_ref.at[r], m_ref.at[r], l_ref.at[r]
        rows = slice(j * QBLK, (j + 1) * QBLK)
        keys = slice(j * QBLK, (j + 2) * QBLK)
        first_ok = qi if j > 0 else qi + jnp.where(pl.program_id(2) > 0, 0, QBLK)
        mask = jnp.logical_and(jnp.logical_or(ki >= first_ok, ki >= QBLK), in_band)
        for pr in range(n_pairs):
            cols = slice(pr * LANES, (pr + 1) * LANES)
            q2 = q_r[rows, cols]
            k2 = kcat_ref[r, keys, cols]
            v_ext = vcat_ref[r, keys, 2 * pr * LANES:(2 * pr + 2) * LANES]
            o_pair = None
            for a in range(2):
                own = low_half if a == 0 else jnp.logical_not(low_half)
                qm = jnp.where(own, q2, jnp.zeros_like(q2))
                s = lax.dot_general(qm, k2, _NT, preferred_element_type=F32)
                s = jnp.where(mask, s, NEG_BIG)
                m = jnp.max(s, axis=-1, keepdims=True)
                p = jnp.exp(s - m).astype(BF16)
                oe = jnp.dot(p, v_ext, preferred_element_type=F32)
                o_pair = oe[:, :LANES] if a == 0 else jnp.where(low_half, o_pair, oe[:, :LANES])
                h = 2 * pr + a
                m_r[rows, h:h + 1] = m
                l_r[rows, h:h + 1] = oe[:, LANES + h:LANES + h + 1]
            o_r[rows, cols] = o_pair.astype(o_r.dtype)

    for r in range(q_ref.shape[0]):
        kcat_ref[r, 0:QBLK, :] = kp_ref[r]
        kcat_ref[r, QBLK:, :] = kc_ref[r]
        for pr in range(n_pairs):
            vcat_ref[r, 0:QBLK, 2 * pr * LANES:(2 * pr + 1) * LANES] = vp_ref[r, :, pr * LANES:(pr + 1) * LANES]
            vcat_ref[r, QBLK:, 2 * pr * LANES:(2 * pr + 1) * LANES] = vc_ref[r, :, pr * LANES:(pr + 1) * LANES]
    for r in range(q_ref.shape[0]):
        for j in range(q_ref.shape[1] // QBLK):
            block(r, j)


def _group_attention(q, k, v):
    bsz, dil, length, width = q.shape
    blocks_per_step = min(MAX_QBLKS_PER_STEP, length // QBLK)
    classes_per_step = min(dil, MAX_QBLKS_PER_STEP // blocks_per_step)
    step = blocks_per_step * QBLK
    blk = (None, classes_per_step, step, width)
    cur = lambda b, r, n: (b, r, n, 0)
    prev_blk = (None, classes_per_step, QBLK, width)
    prev = lambda b, r, n: (b, r, jnp.maximum(blocks_per_step * n - 1, 0), 0)
    stat_spec = pl.BlockSpec((None, classes_per_step, step, LANES), cur)
    stat_shape = jax.ShapeDtypeStruct((bsz, dil, length, LANES), F32)
    return pl.pallas_call(
        _attn_kernel,
        grid=(bsz, dil // classes_per_step, length // step),
        in_specs=[
            pl.BlockSpec(blk, cur),
            pl.BlockSpec(prev_blk, prev),
            pl.BlockSpec(blk, cur),
            pl.BlockSpec(prev_blk, prev),
            pl.BlockSpec(blk, cur),
        ],
        out_specs=[pl.BlockSpec(blk, cur), stat_spec, stat_spec],
        out_shape=[jax.ShapeDtypeStruct((bsz, dil, length, width), BF16), stat_shape, stat_shape],
        scratch_shapes=[pltpu.VMEM((classes_per_step, step + QBLK, width), BF16),
                        pltpu.VMEM((classes_per_step, step + QBLK, 2 * width), BF16)],
        compiler_params=_params("arbitrary", "arbitrary", "arbitrary"),
        name="group_attention",
    )(q, k, k, v, v)


def _to_token_order(src_ref, dst_ref):
    dil, n, w = src_ref.shape
    for r in range(dil):
        for c in range(w // LANES):
            dst_ref[c, pl.ds(r, n, stride=dil), :] = src_ref[r, :, c * LANES:(c + 1) * LANES].astype(dst_ref.dtype)
    return jnp.concatenate([dst_ref[c] for c in range(w // LANES)], axis=1)


def _attn_out_kernel(x_ref, o0_ref, o1_ref, o2_ref, m0_ref, m1_ref, m2_ref, l0_ref, l1_ref, l2_ref,
                     wo_ref, g_ref, b_ref, out_ref, o_tok, stat_tok):
    width = o0_ref.shape[-1]

    def token_order(ref, scratch):
        if ref.shape[0] == 1:
            return ref[0].astype(F32)
        return _to_token_order(ref, scratch)

    ms = [token_order(r, stat_tok.at[i]) for i, r in enumerate((m0_ref, m1_ref, m2_ref))]
    ls = [token_order(r, stat_tok.at[N_GROUPS + i]) for i, r in enumerate((l0_ref, l1_ref, l2_ref))]
    top = jnp.maximum(jnp.maximum(ms[0], ms[1]), ms[2])
    es = [jnp.exp(m - top) for m in ms]
    den = es[0] * ls[0] + es[1] * ls[1] + es[2] * ls[2]
    head = lax.broadcasted_iota(I32, (LANES, width), 0)
    lane = lax.broadcasted_iota(I32, (LANES, width), 1)
    spread = (lane // HEAD_DIM == head).astype(BF16)
    mixed = jnp.zeros(out_ref.shape[:1] + (width,), F32)
    for e, o_ref in zip(es, (o0_ref, o1_ref, o2_ref)):
        hi, lo = _split_bf16(e / den)
        wide = (jnp.dot(hi, spread, preferred_element_type=F32)
                + jnp.dot(lo, spread, preferred_element_type=F32))
        mixed = mixed + wide * token_order(o_ref, o_tok)
    y = jnp.dot(mixed.astype(BF16), wo_ref[...], preferred_element_type=F32)
    out_ref[...] = _layer_norm(ALPHA * x_ref[...] + y, g_ref[...], b_ref[...])


def _attn_out(x, bsz, outs, stats, w_o, g, b):
    t, d = x.shape
    width = w_o.shape[0]
    tm = TM_RES
    tiles_per_seq = t // bsz // tm
    const = lambda bi, c: (0, 0)
    row = lambda bi, c: (bi * tiles_per_seq + c, 0)

    def res_spec(a):
        dil, w = a.shape[1], a.shape[3]
        return pl.BlockSpec((None, dil, tm // dil, w), lambda bi, c: (bi, 0, c, 0))

    return pl.pallas_call(
        _attn_out_kernel,
        grid=(bsz, tiles_per_seq),
        in_specs=([pl.BlockSpec((tm, d), row)] + [res_spec(a) for a in outs] + [res_spec(a) for a in stats]
                  + [pl.BlockSpec((width, d), const), pl.BlockSpec((1, d), const), pl.BlockSpec((1, d), const)]),
        out_specs=pl.BlockSpec((tm, d), row),
        out_shape=jax.ShapeDtypeStruct((t, d), F32),
        scratch_shapes=[pltpu.VMEM((width // LANES, tm, LANES), F32), pltpu.VMEM((2 * N_GROUPS, 1, tm, LANES), F32)],
        compiler_params=_params("arbitrary", "arbitrary"),
        name="attn_out",
    )(x, *outs, *stats, w_o, g, b)


def _proj_residue_kernel(x_ref, w_ref, *rest, dils, scale):
    outs, y_ref = rest[:-1], rest[-1]
    _, n_slabs, tm, _ = y_ref.shape
    width = n_slabs * LANES
    xb = x_ref[...].astype(BF16)
    for c, (o_ref, dil) in enumerate(zip(outs, dils)):
        y = jnp.dot(xb, w_ref[:, c * width:(c + 1) * width], preferred_element_type=F32)
        if scale != 1.0:
            y = y * scale
        if dil == 1:
            o_ref[0] = y.astype(o_ref.dtype)
        else:
            for s in range(n_slabs):
                y_ref[c, s] = y[:, s * LANES:(s + 1) * LANES]
            for r in range(dil):
                rows = [y_ref[c, s, pl.ds(r, tm // dil, stride=dil), :] for s in range(n_slabs)]
                o_ref[r] = jnp.concatenate(rows, axis=1).astype(o_ref.dtype)


def _proj_residue(x, bsz, w, dils, scale=1.0):
    t, d = x.shape
    seq = t // bsz
    width = w.shape[1] // len(dils)
    tm = TM_WIDE
    tiles_per_seq = seq // tm
    return pl.pallas_call(
        functools.partial(_proj_residue_kernel, dils=dils, scale=scale),
        grid=(bsz, tiles_per_seq),
        in_specs=[pl.BlockSpec((tm, d), lambda bi, c: (bi * tiles_per_seq + c, 0)),
                  pl.BlockSpec(w.shape, lambda bi, c: (0, 0))],
        out_specs=[pl.BlockSpec((None, dil, tm // dil, width), lambda bi, c: (bi, 0, c, 0)) for dil in dils],
        out_shape=[jax.ShapeDtypeStruct((bsz, dil, seq // dil, width), BF16) for dil in dils],
        scratch_shapes=[pltpu.VMEM((len(dils), width // LANES, tm, LANES), F32)],
        compiler_params=_params("arbitrary", "arbitrary"),
        name="proj_residue",
    )(x, w)


def _attention_layer(x, bsz, w_q, w_o, kv_groups, g, b):
    dils = tuple(dil for _, dil in ATT_GROUPS)
    qs = _proj_residue(x, bsz, w_q, dils, scale=HEAD_DIM ** -0.5)
    outs, maxes, dens = [], [], []
    for q, (k_res, v_res) in zip(qs, kv_groups):
        o, m, l = _group_attention(q, k_res, v_res)
        outs.append(o)
        maxes.append(m)
        dens.append(l)
    return _attn_out(x, bsz, outs, maxes + dens, w_o, g, b)


def kernel(x, a_w_in, a_conv, a_w_out, kv_w, b_w_q, b_w_o, ffn_w_gate, ffn_w_up, ffn_w_down,
           moe_w_router, moe_w_gate, moe_w_up, moe_w_down, ln_g, ln_b):
    bsz, seq, d = x.shape
    sb = bsz // N_STREAMS
    hs = [x[s * sb:(s + 1) * sb].reshape(sb * seq, d) for s in range(N_STREAMS)]
    ln_g = ln_g.reshape(DEPTH, 2, 1, d)
    ln_b = ln_b.reshape(DEPTH, 2, 1, d)
    kv_dils = tuple(dil for _, dil in ATT_GROUPS for _ in range(2))
    kv_groups = None
    for l in range(DEPTH):
        if l < N_A_LAYERS:
            w_in, w_out = a_w_in[l].astype(BF16), a_w_out[l].astype(BF16)
            hs = [_a_mixer(h, w_in, a_conv[l], w_out, ln_g[l, 0], ln_b[l, 0], seq) for h in hs]
        else:
            if kv_groups is None:
                kv_wb = kv_w.astype(BF16)
                kvs = [_proj_residue(h, sb, kv_wb, kv_dils) for h in hs]
                kv_groups = [[(kv[2 * gi], kv[2 * gi + 1]) for gi in range(N_GROUPS)] for kv in kvs]
            j = l - N_A_LAYERS
            w_q, w_o = b_w_q[j].astype(BF16), b_w_o[j].astype(BF16)
            hs = [_attention_layer(h, sb, w_q, w_o, kvg, ln_g[l, 0], ln_b[l, 0]) for h, kvg in zip(hs, kv_groups)]
        i = l // 2
        if l % 2 == 0:
            w_gate, w_up, w_down = (w[i].astype(BF16) for w in (ffn_w_gate, ffn_w_up, ffn_w_down))
            hs = [_dense_ffn(h, w_gate, w_up, w_down, ln_g[l, 1], ln_b[l, 1]) for h in hs]
        else:
            hs = _moe_layer(hs, moe_w_router[i], moe_w_gate.astype(BF16), moe_w_up.astype(BF16),
                            moe_w_down.astype(BF16), i, ln_g[l, 1], ln_b[l, 1])
    return jnp.concatenate(hs, axis=0).reshape(bsz, seq, d)
```

```python
import functools

import jax
import jax.numpy as jnp
from jax import lax
from jax.experimental import pallas as pl
from jax.experimental.pallas import tpu as pltpu
from jax.experimental.pallas import tpu_sc as plsc

F32 = jnp.float32
U32 = jnp.uint32
BF16 = jnp.bfloat16
I32 = jnp.int32

DEPTH = 4
N_A_LAYERS = DEPTH // 2
CONV_WIDTH = 3
ATT_GROUPS = ((128, 1), (512, 4), (2048, 16))
N_GROUPS = len(ATT_GROUPS)
HEAD_DIM = 64
N_EXPERTS = 8
TOP_K = 2
ALPHA = (2.0 * DEPTH) ** 0.25
LN_EPS = 1e-5

QBLK = 128
assert all(w // d == QBLK for w, d in ATT_GROUPS)

LANES = 128
SUBLANES = 8
VMEM_LIMIT = 56 * 1024 * 1024
NEG_BIG = -1e30

TM_MIX = 512
TM_EXP = 512
TM_RES = 512
TM_WIDE = 1024
N_STREAMS = 1

_NT = (((1,), (1,)), ((), ()))


def _layer_norm(z, g, b):
    mu = jnp.mean(z, axis=-1, keepdims=True)
    zc = z - mu
    var = jnp.mean(zc * zc, axis=-1, keepdims=True)
    return zc * lax.rsqrt(var + LN_EPS) * g + b


def _params(*semantics):
    return pltpu.CompilerParams(dimension_semantics=semantics, vmem_limit_bytes=VMEM_LIMIT)


def _a_mixer_kernel(x_ref, win_ref, conv_ref, wout_ref, g_ref, b_ref, o_ref, ubuf, *, tiles_per_seq):
    tm, d = x_ref.shape
    piece = tm // A_MIXER_PIECES

    @pl.when(pl.program_id(0) % tiles_per_seq == 0)
    def _():
        ubuf[0:SUBLANES, :] = jnp.zeros((SUBLANES, d), F32)

    cw = conv_ref[...]
    for h in range(A_MIXER_PIECES):
        rows = slice(h * piece, (h + 1) * piece)
        first = SUBLANES + h * piece
        x = x_ref[rows, :]
        p = jnp.dot(x.astype(BF16), win_ref[...], preferred_element_type=F32)
        ubuf[first:first + piece, :] = p[:, d:2 * d] * p[:, 2 * d:]
        conv = (cw[2:3, :] * ubuf[first:first + piece, :]
                + cw[1:2, :] * ubuf[first - 1:first - 1 + piece, :]
                + cw[0:1, :] * ubuf[first - 2:first - 2 + piece, :])
        y = jnp.dot((p[:, :d] * conv).astype(BF16), wout_ref[...], preferred_element_type=F32)
        o_ref[rows, :] = _layer_norm(ALPHA * x + y, g_ref[...], b_ref[...])
    ubuf[0:SUBLANES, :] = ubuf[tm:tm + SUBLANES, :]


A_MIXER_PIECES = 2


def _a_mixer(x, w_in, conv_w, w_out, g, b, seq_len):
    t, d = x.shape
    tm = TM_WIDE
    const = lambda i: (0, 0)
    resident = dict(pipeline_mode=pl.Buffered(1))
    return pl.pallas_call(
        functools.partial(_a_mixer_kernel, tiles_per_seq=seq_len // tm),
        grid=(t // tm,),
        in_specs=[
            pl.BlockSpec((tm, d), lambda i: (i, 0)),
            pl.BlockSpec((d, 3 * d), const, **resident),
            pl.BlockSpec((CONV_WIDTH, d), const),
            pl.BlockSpec((d, d), const, **resident),
            pl.BlockSpec((1, d), const),
            pl.BlockSpec((1, d), const),
        ],
        out_specs=pl.BlockSpec((tm, d), lambda i: (i, 0)),
        out_shape=jax.ShapeDtypeStruct((t, d), F32),
        scratch_shapes=[pltpu.VMEM((tm + SUBLANES, d), F32)],
        compiler_params=_params("arbitrary"),
        name="a_mixer",
    )(x, w_in, conv_w, w_out, g, b)


def _swiglu_partial(xb, wg_ref, wu_ref, wd_ref):
    gate = jnp.dot(xb, wg_ref[...], preferred_element_type=F32)
    up = jnp.dot(xb, wu_ref[...], preferred_element_type=F32)
    h = (gate * jax.nn.sigmoid(gate) * up).astype(BF16)
    return jnp.dot(h, wd_ref[...], preferred_element_type=F32)


def _dense_ffn_kernel(x_ref, wg_ref, wu_ref, wd_ref, g_ref, b_ref, o_ref):
    x = x_ref[...]
    y = _swiglu_partial(x.astype(BF16), wg_ref, wu_ref, wd_ref)
    o_ref[...] = _layer_norm(ALPHA * x + y, g_ref[...], b_ref[...])


def _dense_ffn(x, w_gate, w_up, w_down, g, b):
    t, d = x.shape
    d_ff = w_gate.shape[1]
    tm = TM_MIX
    const = lambda i: (0, 0)
    resident = dict(pipeline_mode=pl.Buffered(1))
    return pl.pallas_call(
        _dense_ffn_kernel,
        grid=(t // tm,),
        in_specs=[
            pl.BlockSpec((tm, d), lambda i: (i, 0)),
            pl.BlockSpec((d, d_ff), const, **resident),
            pl.BlockSpec((d, d_ff), const, **resident),
            pl.BlockSpec((d_ff, d), const, **resident),
            pl.BlockSpec((1, d), const),
            pl.BlockSpec((1, d), const),
        ],
        out_specs=pl.BlockSpec((tm, d), lambda i: (i, 0)),
        out_shape=jax.ShapeDtypeStruct((t, d), F32),
        compiler_params=_params("arbitrary"),
        name="dense_ffn",
    )(x, w_gate, w_up, w_down, g, b)


EXPERT_COL_SPLITS = 2


def _expert_ffn_kernel(te_ref, nv_ref, x_ref, wg_ref, wu_ref, wd_ref, o_ref):
    del te_ref
    valid = pl.program_id(0) < nv_ref[0]

    @pl.when(valid)
    def _():
        xb = _unpack_bf16_pairs(x_ref[...]).astype(BF16)
        piece = wg_ref.shape[1] // EXPERT_COL_SPLITS
        y = None
        for c in range(EXPERT_COL_SPLITS):
            cols = slice(c * piece, (c + 1) * piece)
            gate = jnp.dot(xb, wg_ref[:, cols], preferred_element_type=F32)
            up = jnp.dot(xb, wu_ref[:, cols], preferred_element_type=F32)
            h = (gate * jax.nn.sigmoid(gate) * up).astype(BF16)
            part = jnp.dot(h, wd_ref[cols, :], preferred_element_type=F32)
            y = part if y is None else y + part
        o_ref[...] = _pack_bf16_pairs(y)

    @pl.when(jnp.logical_not(valid))
    def _():
        o_ref[...] = jnp.zeros(o_ref.shape, U32)


def _expert_ffn(xs, tile_expert, n_valid, w_gate, w_up, w_down, layer):
    d = w_gate.shape[2]
    d_exp = w_gate.shape[3]
    tm = TM_EXP
    n_rows = xs.shape[0] // tm * tm
    weights_of_tile = lambda i, te, nv: (layer, te[i], 0, 0)
    resident = dict(pipeline_mode=pl.Buffered(1))
    return pl.pallas_call(
        _expert_ffn_kernel,
        grid_spec=pltpu.PrefetchScalarGridSpec(
            num_scalar_prefetch=2,
            grid=(n_rows // tm,),
            in_specs=[
                pl.BlockSpec((tm, d // 2), lambda i, te, nv: (i, 0)),
                pl.BlockSpec((None, None, d, d_exp), weights_of_tile, **resident),
                pl.BlockSpec((None, None, d, d_exp), weights_of_tile, **resident),
                pl.BlockSpec((None, None, d_exp, d), weights_of_tile, **resident),
            ],
            out_specs=pl.BlockSpec((tm, d // 2), lambda i, te, nv: (i, 0)),
        ),
        out_shape=jax.ShapeDtypeStruct((n_rows, d // 2), U32),
        compiler_params=_params("arbitrary"),
        name="expert_ffn",
    )(tile_expert, n_valid, xs, w_gate, w_up, w_down)


def _split_bf16(v):
    hi = v.astype(BF16)
    lo = (v - hi.astype(F32)).astype(BF16)
    return hi, lo


def _pack_bf16_pairs(v):
    w = v.shape[1] // 2
    lo = lax.bitcast_convert_type(v[:, :w].astype(BF16).astype(F32), U32)
    hi = lax.bitcast_convert_type(v[:, w:].astype(BF16).astype(F32), U32)
    return (lo >> 16) | hi


def _unpack_bf16_pairs(p):
    lo = lax.bitcast_convert_type(p << 16, F32)
    hi = lax.bitcast_convert_type(p & jnp.uint32(0xFFFF0000), F32)
    return jnp.concatenate([lo, hi], axis=1)


def _router_kernel(x_ref, wr_ref, idx_ref, gate_ref, xp_ref, rank_ref, count_ref, run_ref):
    xp_ref[...] = _pack_bf16_pairs(x_ref[...])
    xh, xl = _split_bf16(x_ref[...])
    wh, wl = _split_bf16(wr_ref[...])
    logits = (lax.dot_general(wh, xh, _NT, preferred_element_type=F32)
              + lax.dot_general(wh, xl, _NT, preferred_element_type=F32)
              + lax.dot_general(wl, xh, _NT, preferred_element_type=F32))
    e = lax.broadcasted_iota(I32, logits.shape, 0)
    m1 = jnp.max(logits, axis=0, keepdims=True)
    i1 = jnp.min(jnp.where(logits == m1, e, N_EXPERTS), axis=0, keepdims=True)
    rest = jnp.where(e == i1, -jnp.inf, logits)
    m2 = jnp.max(rest, axis=0, keepdims=True)
    i2 = jnp.min(jnp.where(rest == m2, e, N_EXPERTS), axis=0, keepdims=True)
    r = jnp.exp(m2 - m1)
    idx_ref[...] = jnp.concatenate([i1, i2], axis=0)
    gate_ref[...] = jnp.concatenate([1.0 / (1.0 + r), r / (1.0 + r)], axis=0)

    @pl.when(pl.program_id(0) == 0)
    def _():
        run_ref[...] = jnp.zeros(run_ref.shape, F32)

    tm = logits.shape[1]
    uses = jnp.where(jnp.logical_or(e == i1, e == i2), 1.0, 0.0)
    earlier = lax.broadcasted_iota(I32, (tm, tm), 0) < lax.broadcasted_iota(I32, (tm, tm), 1)
    prefix = jnp.dot(uses.astype(BF16), jnp.where(earlier, 1.0, 0.0).astype(BF16), preferred_element_type=F32)
    prefix = prefix + run_ref[:, 0:1]
    ranks = [jnp.sum(jnp.where(e == i, prefix, 0.0), axis=0, keepdims=True) for i in (i1, i2)]
    rank_ref[...] = jnp.concatenate(ranks, axis=0).astype(I32)
    run_ref[...] = run_ref[...] + jnp.sum(uses, axis=1, keepdims=True)
    count_ref[...] = run_ref[...]


def _router(x, w_router_t):
    t, d = x.shape
    tm = TM_WIDE
    return pl.pallas_call(
        _router_kernel,
        grid=(t // tm,),
        in_specs=[
            pl.BlockSpec((tm, d), lambda i: (i, 0)),
            pl.BlockSpec((N_EXPERTS, d), lambda i: (0, 0)),
        ],
        out_specs=[
            pl.BlockSpec((TOP_K, tm), lambda i: (0, i)),
            pl.BlockSpec((TOP_K, tm), lambda i: (0, i)),
            pl.BlockSpec((tm, d // 2), lambda i: (i, 0)),
            pl.BlockSpec((TOP_K, tm), lambda i: (0, i)),
            pl.BlockSpec((N_EXPERTS, LANES), lambda i: (0, 0)),
        ],
        out_shape=[jax.ShapeDtypeStruct((TOP_K, t), I32), jax.ShapeDtypeStruct((TOP_K, t), F32),
                   jax.ShapeDtypeStruct((t, d // 2), U32), jax.ShapeDtypeStruct((TOP_K, t), I32),
                   jax.ShapeDtypeStruct((N_EXPERTS, LANES), F32)],
        scratch_shapes=[pltpu.VMEM((N_EXPERTS, LANES), F32)],
        compiler_params=_params("arbitrary"),
        name="router",
    )(x, w_router_t)


def _moe_route(x, w_router):
    t, d = x.shape
    idx, gates, x_packed, rank, count = _router(x, w_router.T)

    counts = count[:, 0].astype(I32)
    padded = ((counts + TM_EXP - 1) // TM_EXP) * TM_EXP
    ends = jnp.cumsum(padded)
    experts = jnp.arange(N_EXPERTS, dtype=I32)[:, None, None]
    pos = rank + jnp.sum(jnp.where(idx[None] == experts, (ends - padded)[:, None, None], 0), axis=0)
    n_tiles = TOP_K * t // TM_EXP + N_EXPERTS
    tile_ids = jnp.arange(n_tiles, dtype=I32)
    tile_expert = jnp.minimum(
        jnp.sum((tile_ids[:, None] >= (ends // TM_EXP)[None, :]).astype(I32), axis=1), N_EXPERTS - 1).astype(I32)
    n_valid = (ends[-1:] // TM_EXP).astype(I32)

    n_rows = n_tiles * TM_EXP
    return dict(x_packed=x_packed, pos0=pos[0], pos1=pos[1], gates=gates.T, tile_expert=tile_expert, n_valid=n_valid,
                unused=_unused_rows(counts, padded, ends, n_rows), n_rows=n_rows)


def _moe_layer(streams, w_router, w_gate, w_up, w_down, layer, g, b):
    routes = [_moe_route(x, w_router) for x in streams]
    xs = [_sc_dispatch(r["x_packed"], r["pos0"], r["pos1"], r["unused"], r["n_rows"]) for r in routes]
    results = []
    for x, r, x_sorted in zip(streams, routes, xs):
        ys = _expert_ffn(x_sorted, r["tile_expert"], r["n_valid"], w_gate, w_up, w_down, layer)
        rows = x.shape[0] // COMBINE_PIECES
        gathered = [_sc_gather2(ys, r["pos0"][p * rows:(p + 1) * rows], r["pos1"][p * rows:(p + 1) * rows])
                    for p in range(COMBINE_PIECES)]
        out = None
        for p, (y0, y1) in enumerate(gathered):
            out = _moe_sum(x, y0, y1, r["gates"], g, b, p, out)
        results.append(out)
    return results


COMBINE_PIECES = 2
SC_CORES = 2
SC_SUBCORES = 16
SC_ROWS = 128


def _sc_worker_base(per_worker):
    return (lax.axis_index("subcore") * SC_CORES + lax.axis_index("core")) * per_worker


def _unused_rows(counts, padded, ends, n_rows):
    lane = jnp.arange(SC_ROWS, dtype=I32)
    chunk0 = jnp.arange(TM_EXP // SC_ROWS, dtype=I32) * SC_ROWS
    spare = n_rows + lane
    pad = (ends - padded + counts)[:, None, None] + chunk0[None, :, None] + lane[None, None, :]
    pad = jnp.where(pad < ends[:, None, None], pad, spare[None, None, :])
    tail = ends[-1] + jnp.arange(N_EXPERTS * TM_EXP // SC_ROWS, dtype=I32)[:, None] * SC_ROWS + lane[None, :]
    tail = jnp.where(tail < n_rows, tail, spare[None, :])
    return jnp.concatenate([pad.reshape(-1), tail.reshape(-1)]).astype(I32)


def _sc_dispatch(x, pos0, pos1, unused, n_rows):
    t, d = x.shape
    workers = SC_CORES * SC_SUBCORES
    per_worker = t // workers
    zero_chunks = unused.shape[0] // SC_ROWS // workers
    mesh = plsc.VectorSubcoreMesh(core_axis_name="core", subcore_axis_name="subcore")

    @pl.kernel(out_type=jax.ShapeDtypeStruct((n_rows + SC_ROWS, d), x.dtype), mesh=mesh,
               scratch_types=[pltpu.VMEM((SC_ROWS,), I32), pltpu.VMEM((SC_ROWS,), I32),
                              pltpu.VMEM((SC_ROWS, d), x.dtype)])
    def run(x_hbm, p0_hbm, p1_hbm, unused_hbm, zeros_hbm, o_hbm, i0_v, i1_v, rows_v):
        pltpu.sync_copy(zeros_hbm, rows_v)
        zbase = _sc_worker_base(zero_chunks * SC_ROWS)

        @pl.loop(0, zero_chunks)
        def _(c):
            pltpu.sync_copy(unused_hbm.at[pl.ds(zbase + c * SC_ROWS, SC_ROWS)], i0_v)
            pltpu.sync_copy(rows_v, o_hbm.at[i0_v])

        base = _sc_worker_base(per_worker)

        @pl.loop(0, per_worker // SC_ROWS)
        def _(c):
            off = base + c * SC_ROWS
            pltpu.sync_copy(p0_hbm.at[pl.ds(off, SC_ROWS)], i0_v)
            pltpu.sync_copy(p1_hbm.at[pl.ds(off, SC_ROWS)], i1_v)
            pltpu.sync_copy(x_hbm.at[pl.ds(off, SC_ROWS)], rows_v)
            pltpu.sync_copy(rows_v, o_hbm.at[i0_v])
            pltpu.sync_copy(rows_v, o_hbm.at[i1_v])

    return run(x, pos0, pos1, unused, jnp.zeros((SC_ROWS, d), x.dtype))


def _sc_gather2(ys, pos0, pos1):
    t = pos0.shape[0]
    d = ys.shape[1]
    per_worker = t // (SC_CORES * SC_SUBCORES)
    mesh = plsc.VectorSubcoreMesh(core_axis_name="core", subcore_axis_name="subcore")
    out = jax.ShapeDtypeStruct((t, d), ys.dtype)

    @pl.kernel(out_type=(out, out), mesh=mesh,
               scratch_types=[pltpu.VMEM((SC_ROWS,), I32), pltpu.VMEM((SC_ROWS, d), ys.dtype)])
    def run(y_hbm, p0_hbm, p1_hbm, o0_hbm, o1_hbm, i_v, rows_v):
        base = _sc_worker_base(per_worker)

        @pl.loop(0, per_worker // SC_ROWS)
        def _(c):
            off = base + c * SC_ROWS
            for p_hbm, o_hbm in ((p0_hbm, o0_hbm), (p1_hbm, o1_hbm)):
                pltpu.sync_copy(p_hbm.at[pl.ds(off, SC_ROWS)], i_v)
                pltpu.sync_copy(y_hbm.at[i_v], rows_v)
                pltpu.sync_copy(rows_v, o_hbm.at[pl.ds(off, SC_ROWS)])

    return run(ys, pos0, pos1)


def _moe_sum_kernel(x_ref, y0_ref, y1_ref, gate_ref, g_ref, b_ref, *rest):
    o_ref = rest[-1]
    gates = gate_ref[...]
    ff = gates[:, 0:1] * _unpack_bf16_pairs(y0_ref[...]) + gates[:, 1:2] * _unpack_bf16_pairs(y1_ref[...])
    o_ref[...] = _layer_norm(ALPHA * x_ref[...] + ff, g_ref[...], b_ref[...])


def _moe_sum(x, y0, y1, gates, g, b, piece, partial_out):
    t, d = x.shape
    rows = y0.shape[0]
    tm = TM_WIDE
    first = piece * rows // tm
    const = lambda i: (0, 0)
    row = lambda i: (i, 0)
    full_row = lambda i: (first + i, 0)
    in_specs = [pl.BlockSpec((tm, d), full_row), pl.BlockSpec((tm, d // 2), row), pl.BlockSpec((tm, d // 2), row),
                pl.BlockSpec((tm, TOP_K), full_row), pl.BlockSpec((1, d), const), pl.BlockSpec((1, d), const)]
    operands = [x, y0, y1, gates, g, b]
    aliases = {}
    if partial_out is not None:
        in_specs.append(pl.BlockSpec(memory_space=pl.ANY))
        operands.append(partial_out)
        aliases = {len(operands) - 1: 0}
    return pl.pallas_call(
        _moe_sum_kernel,
        grid=(rows // tm,),
        in_specs=in_specs,
        out_specs=pl.BlockSpec((tm, d), full_row),
        out_shape=jax.ShapeDtypeStruct((t, d), F32),
        input_output_aliases=aliases,
        compiler_params=_params("arbitrary"),
        name="moe_sum",
    )(*operands)


MAX_QBLKS_PER_STEP = 8


def _attn_kernel(q_ref, kp_ref, kc_ref, vp_ref, vc_ref, o_ref, m_ref, l_ref, kcat_ref, vcat_ref):
    n_pairs = q_ref.shape[-1] // LANES
    qi = lax.broadcasted_iota(I32, (QBLK, 2 * QBLK), 0)
    ki = lax.broadcasted_iota(I32, (QBLK, 2 * QBLK), 1)
    in_band = ki - QBLK <= qi
    lane = lax.broadcasted_iota(I32, (QBLK, LANES), 1)
    low_half = lane < HEAD_DIM
    for pr in range(n_pairs):
        vcat_ref[:, :, (2 * pr + 1) * LANES:(2 * pr + 2) * LANES] = jnp.ones(vcat_ref.shape[:2] + (LANES,), BF16)
    m_ref[...] = jnp.zeros(m_ref.shape, F32)
    l_ref[...] = jnp.ones(l_ref.shape, F32)

    def block(r, j):
        q_r, o_r, m_r, l_r = q_ref.at[r], o_ref.at[r], m_ref.at[r], l_ref.at[r]
        rows = slice(j * QBLK, (j + 1) * QBLK)
        keys = slice(j * QBLK, (j + 2) * QBLK)
        first_ok = qi if j > 0 else qi + jnp.where(pl.program_id(2) > 0, 0, QBLK)
        mask = jnp.logical_and(jnp.logical_or(ki >= first_ok, ki >= QBLK), in_band)
        for pr in range(n_pairs):
            cols = slice(pr * LANES, (pr + 1) * LANES)
            q2 = q_r[rows, cols]
            k2 = kcat_ref[r, keys, cols]
            v_ext = vcat_ref[r, keys, 2 * pr * LANES:(2 * pr + 2) * LANES]
            o_pair = None
            for a in range(2):
                own = low_half if a == 0 else jnp.logical_not(low_half)
                qm = jnp.where(own, q2, jnp.zeros_like(q2))
                s = lax.dot_general(qm, k2, _NT, preferred_element_type=F32)
                s = jnp.where(mask, s, NEG_BIG)
                m = jnp.max(s, axis=-1, keepdims=True)
                p = jnp.exp(s - m).astype(BF16)
                oe = jnp.dot(p, v_ext, preferred_element_type=F32)
                o_pair = oe[:, :LANES] if a == 0 else jnp.where(low_half, o_pair, oe[:, :LANES])
                h = 2 * pr + a
                m_r[rows, h:h + 1] = m
                l_r[rows, h:h + 1] = oe[:, LANES + h:LANES + h + 1]
            o_r[rows, cols] = o_pair.astype(o_r.dtype)

    for r in range(q_ref.shape[0]):
        kcat_ref[r, 0:QBLK, :] = kp_ref[r]
        kcat_ref[r, QBLK:, :] = kc_ref[r]
        for pr in range(n_pairs):
            vcat_ref[r, 0:QBLK, 2 * pr * LANES:(2 * pr + 1) * LANES] = vp_ref[r, :, pr * LANES:(pr + 1) * LANES]
            vcat_ref[r, QBLK:, 2 * pr * LANES:(2 * pr + 1) * LANES] = vc_ref[r, :, pr * LANES:(pr + 1) * LANES]
    for r in range(q_ref.shape[0]):
        for j in range(q_ref.shape[1] // QBLK):
            block(r, j)


def _group_attention(q, k, v):
    bsz, dil, length, width = q.shape
    blocks_per_step = min(MAX_QBLKS_PER_STEP, length // QBLK)
    classes_per_step = min(dil, MAX_QBLKS_PER_STEP // blocks_per_step)
    step = blocks_per_step * QBLK
    blk = (None, classes_per_step, step, width)
    cur = lambda b, r, n: (b, r, n, 0)
    prev_blk = (None, classes_per_step, QBLK, width)
    prev = lambda b, r, n: (b, r, jnp.maximum(blocks_per_step * n - 1, 0), 0)
    stat_spec = pl.BlockSpec((None, classes_per_step, step, LANES), cur)
    stat_shape = jax.ShapeDtypeStruct((bsz, dil, length, LANES), F32)
    return pl.pallas_call(
        _attn_kernel,
        grid=(bsz, dil // classes_per_step, length // step),
        in_specs=[
            pl.BlockSpec(blk, cur),
            pl.BlockSpec(prev_blk, prev),
            pl.BlockSpec(blk, cur),
            pl.BlockSpec(prev_blk, prev),
            pl.BlockSpec(blk, cur),
        ],
        out_specs=[pl.BlockSpec(blk, cur), stat_spec, stat_spec],
        out_shape=[jax.ShapeDtypeStruct((bsz, dil, length, width), BF16), stat_shape, stat_shape],
        scratch_shapes=[pltpu.VMEM((classes_per_step, step + QBLK, width), BF16),
                        pltpu.VMEM((classes_per_step, step + QBLK, 2 * width), BF16)],
        compiler_params=_params("arbitrary", "arbitrary", "arbitrary"),
        name="group_attention",
    )(q, k, k, v, v)


def _to_token_order(src_ref, dst_ref):
    dil, n, w = src_ref.shape
    for r in range(dil):
        for c in range(w // LANES):
            dst_ref[c, pl.ds(r, n, stride=dil), :] = src_ref[r, :, c * LANES:(c + 1) * LANES].astype(dst_ref.dtype)
    return jnp.concatenate([dst_ref[c] for c in range(w // LANES)], axis=1)


def _attn_out_kernel(x_ref, o0_ref, o1_ref, o2_ref, m0_ref, m1_ref, m2_ref, l0_ref, l1_ref, l2_ref,
                     wo_ref, g_ref, b_ref, out_ref, o_tok, stat_tok):
    width = o0_ref.shape[-1]

    def token_order(ref, scratch):
        dil, n, w = ref.shape
        if dil == 1:
            return ref[0].astype(F32)
        if dil % SUBLANES == 0:
            return jnp.swapaxes(ref[...].astype(F32), 0, 1).reshape(dil * n, w)
        return _to_token_order(ref, scratch)

    ms = [token_order(r, stat_tok.at[i]) for i, r in enumerate((m0_ref, m1_ref, m2_ref))]
    ls = [token_order(r, stat_tok.at[N_GROUPS + i]) for i, r in enumerate((l0_ref, l1_ref, l2_ref))]
    top = jnp.maximum(jnp.maximum(ms[0], ms[1]), ms[2])
    es = [jnp.exp(m - top) for m in ms]
    den = es[0] * ls[0] + es[1] * ls[1] + es[2] * ls[2]
    head = lax.broadcasted_iota(I32, (LANES, width), 0)
    lane = lax.broadcasted_iota(I32, (LANES, width), 1)
    spread = (lane // HEAD_DIM == head).astype(BF16)
    mixed = jnp.zeros(out_ref.shape[:1] + (width,), F32)
    for e, o_ref in zip(es, (o0_ref, o1_ref, o2_ref)):
        hi, lo = _split_bf16(e / den)
        wide = (jnp.dot(hi, spread, preferred_element_type=F32)
                + jnp.dot(lo, spread, preferred_element_type=F32))
        mixed = mixed + wide * token_order(o_ref, o_tok)
    y = jnp.dot(mixed.astype(BF16), wo_ref[...], preferred_element_type=F32)
    out_ref[...] = _layer_norm(ALPHA * x_ref[...] + y, g_ref[...], b_ref[...])


def _attn_out(x, bsz, outs, stats, w_o, g, b):
    t, d = x.shape
    width = w_o.shape[0]
    tm = TM_RES
    tiles_per_seq = t // bsz // tm
    const = lambda bi, c: (0, 0)
    row = lambda bi, c: (bi * tiles_per_seq + c, 0)

    def res_spec(a):
        dil, w = a.shape[1], a.shape[3]
        return pl.BlockSpec((None, dil, tm // dil, w), lambda bi, c: (bi, 0, c, 0))

    return pl.pallas_call(
        _attn_out_kernel,
        grid=(bsz, tiles_per_seq),
        in_specs=([pl.BlockSpec((tm, d), row)] + [res_spec(a) for a in outs] + [res_spec(a) for a in stats]
                  + [pl.BlockSpec((width, d), const), pl.BlockSpec((1, d), const), pl.BlockSpec((1, d), const)]),
        out_specs=pl.BlockSpec((tm, d), row),
        out_shape=jax.ShapeDtypeStruct((t, d), F32),
        scratch_shapes=[pltpu.VMEM((width // LANES, tm, LANES), F32), pltpu.VMEM((2 * N_GROUPS, 1, tm, LANES), F32)],
        compiler_params=_params("arbitrary", "arbitrary"),
        name="attn_out",
    )(x, *outs, *stats, w_o, g, b)


def _proj_residue_kernel(x_ref, w_ref, *rest, dils, scale):
    outs, y_ref = rest[:-1], rest[-1]
    _, n_slabs, tm, _ = y_ref.shape
    width = n_slabs * LANES
    xb = x_ref[...].astype(BF16)
    for c, (o_ref, dil) in enumerate(zip(outs, dils)):
        y = jnp.dot(xb, w_ref[:, c * width:(c + 1) * width], preferred_element_type=F32)
        if scale != 1.0:
            y = y * scale
        if dil == 1:
            o_ref[0] = y.astype(o_ref.dtype)
        elif dil % SUBLANES == 0:
            o_ref[...] = jnp.swapaxes(y.reshape(tm // dil, dil, width), 0, 1).astype(o_ref.dtype)
        else:
            for s in range(n_slabs):
                y_ref[c, s] = y[:, s * LANES:(s + 1) * LANES]
            for r in range(dil):
                rows = [y_ref[c, s, pl.ds(r, tm // dil, stride=dil), :] for s in range(n_slabs)]
                o_ref[r] = jnp.concatenate(rows, axis=1).astype(o_ref.dtype)


def _proj_residue(x, bsz, w, dils, scale=1.0):
    t, d = x.shape
    seq = t // bsz
    width = w.shape[1] // len(dils)
    tm = TM_WIDE
    tiles_per_seq = seq // tm
    return pl.pallas_call(
        functools.partial(_proj_residue_kernel, dils=dils, scale=scale),
        grid=(bsz, tiles_per_seq),
        in_specs=[pl.BlockSpec((tm, d), lambda bi, c: (bi * tiles_per_seq + c, 0)),
                  pl.BlockSpec(w.shape, lambda bi, c: (0, 0))],
        out_specs=[pl.BlockSpec((None, dil, tm // dil, width), lambda bi, c: (bi, 0, c, 0)) for dil in dils],
        out_shape=[jax.ShapeDtypeStruct((bsz, dil, seq // dil, width), BF16) for dil in dils],
        scratch_shapes=[pltpu.VMEM((len(dils), width // LANES, tm, LANES), F32)],
        compiler_params=_params("arbitrary", "arbitrary"),
        name="proj_residue",
    )(x, w)


def _attention_layer(x, bsz, w_q, w_o, kv_groups, g, b):
    dils = tuple(dil for _, dil in ATT_GROUPS)
    qs = _proj_residue(x, bsz, w_q, dils, scale=HEAD_DIM ** -0.5)
    outs, maxes, dens = [], [], []
    for q, (k_res, v_res) in zip(qs, kv_groups):
        o, m, l = _group_attention(q, k_res, v_res)
        outs.append(o)
        maxes.append(m)
        dens.append(l)
    return _attn_out(x, bsz, outs, maxes + dens, w_o, g, b)


def kernel(x, a_w_in, a_conv, a_w_out, kv_w, b_w_q, b_w_o, ffn_w_gate, ffn_w_up, ffn_w_down,
           moe_w_router, moe_w_gate, moe_w_up, moe_w_down, ln_g, ln_b):
    bsz, seq, d = x.shape
    sb = bsz // N_STREAMS
    hs = [x[s * sb:(s + 1) * sb].reshape(sb * seq, d) for s in range(N_STREAMS)]
    ln_g = ln_g.reshape(DEPTH, 2, 1, d)
    ln_b = ln_b.reshape(DEPTH, 2, 1, d)
    kv_dils = tuple(dil for _, dil in ATT_GROUPS for _ in range(2))
    kv_groups = None
    for l in range(DEPTH):
        if l < N_A_LAYERS:
            w_in, w_out = a_w_in[l].astype(BF16), a_w_out[l].astype(BF16)
            hs = [_a_mixer(h, w_in, a_conv[l], w_out, ln_g[l, 0], ln_b[l, 0], seq) for h in hs]
        else:
            if kv_groups is None:
                kv_wb = kv_w.astype(BF16)
                kvs = [_proj_residue(h, sb, kv_wb, kv_dils) for h in hs]
                kv_groups = [[(kv[2 * gi], kv[2 * gi + 1]) for gi in range(N_GROUPS)] for kv in kvs]
            j = l - N_A_LAYERS
            w_q, w_o = b_w_q[j].astype(BF16), b_w_o[j].astype(BF16)
            hs = [_attention_layer(h, sb, w_q, w_o, kvg, ln_g[l, 0], ln_b[l, 0]) for h, kvg in zip(hs, kv_groups)]
        i = l // 2
        if l % 2 == 0:
            w_gate, w_up, w_down = (w[i].astype(BF16) for w in (ffn_w_gate, ffn_w_up, ffn_w_down))
            hs = [_dense_ffn(h, w_gate, w_up, w_down, ln_g[l, 1], ln_b[l, 1]) for h in hs]
        else:
            hs = _moe_layer(hs, moe_w_router[i], moe_w_gate.astype(BF16), moe_w_up.astype(BF16),
                            moe_w_down.astype(BF16), i, ln_g[l, 1], ln_b[l, 1])
    return jnp.concatenate(hs, axis=0).reshape(bsz, seq, d)
```

```python
import functools

import jax
import jax.numpy as jnp
from jax import lax
from jax.experimental import pallas as pl
from jax.experimental.pallas import tpu as pltpu
from jax.experimental.pallas import tpu_sc as plsc

F32 = jnp.float32
U32 = jnp.uint32
BF16 = jnp.bfloat16
I32 = jnp.int32

DEPTH = 4
N_A_LAYERS = DEPTH // 2
CONV_WIDTH = 3
ATT_GROUPS = ((128, 1), (512, 4), (2048, 16))
N_GROUPS = len(ATT_GROUPS)
HEAD_DIM = 64
N_EXPERTS = 8
TOP_K = 2
ALPHA = (2.0 * DEPTH) ** 0.25
LN_EPS = 1e-5

QBLK = 128
assert all(w // d == QBLK for w, d in ATT_GROUPS)

LANES = 128
SUBLANES = 8
VMEM_LIMIT = 56 * 1024 * 1024
NEG_BIG = -1e30

TM_MIX = 512
TM_EXP = 512
TM_RES = 1024
TM_WIDE = 1024
N_STREAMS = 1

_NT = (((1,), (1,)), ((), ()))


def _layer_norm(z, g, b):
    mu = jnp.mean(z, axis=-1, keepdims=True)
    zc = z - mu
    var = jnp.mean(zc * zc, axis=-1, keepdims=True)
    return zc * lax.rsqrt(var + LN_EPS) * g + b


def _params(*semantics):
    return pltpu.CompilerParams(dimension_semantics=semantics, vmem_limit_bytes=VMEM_LIMIT)


def _a_mixer_kernel(x_ref, win_ref, conv_ref, wout_ref, g_ref, b_ref, o_ref, ubuf, *, tiles_per_seq):
    tm, d = x_ref.shape
    piece = tm // A_MIXER_PIECES

    @pl.when(pl.program_id(0) % tiles_per_seq == 0)
    def _():
        ubuf[0:SUBLANES, :] = jnp.zeros((SUBLANES, d), F32)

    cw = conv_ref[...]
    for h in range(A_MIXER_PIECES):
        rows = slice(h * piece, (h + 1) * piece)
        first = SUBLANES + h * piece
        x = x_ref[rows, :]
        p = jnp.dot(x.astype(BF16), win_ref[...], preferred_element_type=F32)
        ubuf[first:first + piece, :] = p[:, d:2 * d] * p[:, 2 * d:]
        conv = (cw[2:3, :] * ubuf[first:first + piece, :]
                + cw[1:2, :] * ubuf[first - 1:first - 1 + piece, :]
                + cw[0:1, :] * ubuf[first - 2:first - 2 + piece, :])
        y = jnp.dot((p[:, :d] * conv).astype(BF16), wout_ref[...], preferred_element_type=F32)
        o_ref[rows, :] = _layer_norm(ALPHA * x + y, g_ref[...], b_ref[...])
    ubuf[0:SUBLANES, :] = ubuf[tm:tm + SUBLANES, :]


A_MIXER_PIECES = 2


def _a_mixer(x, w_in, conv_w, w_out, g, b, seq_len):
    t, d = x.shape
    tm = TM_WIDE
    const = lambda i: (0, 0)
    resident = dict(pipeline_mode=pl.Buffered(1))
    return pl.pallas_call(
        functools.partial(_a_mixer_kernel, tiles_per_seq=seq_len // tm),
        grid=(t // tm,),
        in_specs=[
            pl.BlockSpec((tm, d), lambda i: (i, 0)),
            pl.BlockSpec((d, 3 * d), const, **resident),
            pl.BlockSpec((CONV_WIDTH, d), const),
            pl.BlockSpec((d, d), const, **resident),
            pl.BlockSpec((1, d), const),
            pl.BlockSpec((1, d), const),
        ],
        out_specs=pl.BlockSpec((tm, d), lambda i: (i, 0)),
        out_shape=jax.ShapeDtypeStruct((t, d), F32),
        scratch_shapes=[pltpu.VMEM((tm + SUBLANES, d), F32)],
        compiler_params=_params("arbitrary"),
        name="a_mixer",
    )(x, w_in, conv_w, w_out, g, b)


def _swiglu_partial(xb, wg_ref, wu_ref, wd_ref):
    gate = jnp.dot(xb, wg_ref[...], preferred_element_type=F32)
    up = jnp.dot(xb, wu_ref[...], preferred_element_type=F32)
    h = (gate * jax.nn.sigmoid(gate) * up).astype(BF16)
    return jnp.dot(h, wd_ref[...], preferred_element_type=F32)


def _dense_ffn_kernel(x_ref, wg_ref, wu_ref, wd_ref, g_ref, b_ref, o_ref):
    x = x_ref[...]
    y = _swiglu_partial(x.astype(BF16), wg_ref, wu_ref, wd_ref)
    o_ref[...] = _layer_norm(ALPHA * x + y, g_ref[...], b_ref[...])


def _dense_ffn(x, w_gate, w_up, w_down, g, b):
    t, d = x.shape
    d_ff = w_gate.shape[1]
    tm = TM_MIX
    const = lambda i: (0, 0)
    resident = dict(pipeline_mode=pl.Buffered(1))
    return pl.pallas_call(
        _dense_ffn_kernel,
        grid=(t // tm,),
        in_specs=[
            pl.BlockSpec((tm, d), lambda i: (i, 0)),
            pl.BlockSpec((d, d_ff), const, **resident),
            pl.BlockSpec((d, d_ff), const, **resident),
            pl.BlockSpec((d_ff, d), const, **resident),
            pl.BlockSpec((1, d), const),
            pl.BlockSpec((1, d), const),
        ],
        out_specs=pl.BlockSpec((tm, d), lambda i: (i, 0)),
        out_shape=jax.ShapeDtypeStruct((t, d), F32),
        compiler_params=_params("arbitrary"),
        name="dense_ffn",
    )(x, w_gate, w_up, w_down, g, b)


EXPERT_COL_SPLITS = 2


def _expert_ffn_kernel(te_ref, nv_ref, x_ref, wg_ref, wu_ref, wd_ref, o_ref):
    del te_ref
    valid = pl.program_id(0) < nv_ref[0]

    @pl.when(valid)
    def _():
        xb = _unpack_bf16_pairs(x_ref[...]).astype(BF16)
        piece = wg_ref.shape[1] // EXPERT_COL_SPLITS
        y = None
        for c in range(EXPERT_COL_SPLITS):
            cols = slice(c * piece, (c + 1) * piece)
            gate = jnp.dot(xb, wg_ref[:, cols], preferred_element_type=F32)
            up = jnp.dot(xb, wu_ref[:, cols], preferred_element_type=F32)
            h = (gate * jax.nn.sigmoid(gate) * up).astype(BF16)
            part = jnp.dot(h, wd_ref[cols, :], preferred_element_type=F32)
            y = part if y is None else y + part
        o_ref[...] = _pack_bf16_pairs(y)

    @pl.when(jnp.logical_not(valid))
    def _():
        o_ref[...] = jnp.zeros(o_ref.shape, U32)


def _expert_ffn(xs, tile_expert, n_valid, w_gate, w_up, w_down, layer):
    d = w_gate.shape[2]
    d_exp = w_gate.shape[3]
    tm = TM_EXP
    n_rows = xs.shape[0] // tm * tm
    weights_of_tile = lambda i, te, nv: (layer, te[i], 0, 0)
    resident = dict(pipeline_mode=pl.Buffered(1))
    return pl.pallas_call(
        _expert_ffn_kernel,
        grid_spec=pltpu.PrefetchScalarGridSpec(
            num_scalar_prefetch=2,
            grid=(n_rows // tm,),
            in_specs=[
                pl.BlockSpec((tm, d // 2), lambda i, te, nv: (i, 0)),
                pl.BlockSpec((None, None, d, d_exp), weights_of_tile, **resident),
                pl.BlockSpec((None, None, d, d_exp), weights_of_tile, **resident),
                pl.BlockSpec((None, None, d_exp, d), weights_of_tile, **resident),
            ],
            out_specs=pl.BlockSpec((tm, d // 2), lambda i, te, nv: (i, 0)),
        ),
        out_shape=jax.ShapeDtypeStruct((n_rows, d // 2), U32),
        compiler_params=_params("arbitrary"),
        name="expert_ffn",
    )(tile_expert, n_valid, xs, w_gate, w_up, w_down)


def _split_bf16(v):
    hi = v.astype(BF16)
    lo = (v - hi.astype(F32)).astype(BF16)
    return hi, lo


def _pack_bf16_pairs(v):
    w = v.shape[1] // 2
    lo = lax.bitcast_convert_type(v[:, :w].astype(BF16).astype(F32), U32)
    hi = lax.bitcast_convert_type(v[:, w:].astype(BF16).astype(F32), U32)
    return (lo >> 16) | hi


def _unpack_bf16_pairs(p):
    lo = lax.bitcast_convert_type(p << 16, F32)
    hi = lax.bitcast_convert_type(p & jnp.uint32(0xFFFF0000), F32)
    return jnp.concatenate([lo, hi], axis=1)


def _router_kernel(x_ref, wr_ref, idx_ref, gate_ref, xp_ref, rank_ref, count_ref, run_ref):
    xp_ref[...] = _pack_bf16_pairs(x_ref[...])
    xh, xl = _split_bf16(x_ref[...])
    wh, wl = _split_bf16(wr_ref[...])
    logits = (lax.dot_general(wh, xh, _NT, preferred_element_type=F32)
              + lax.dot_general(wh, xl, _NT, preferred_element_type=F32)
              + lax.dot_general(wl, xh, _NT, preferred_element_type=F32))
    e = lax.broadcasted_iota(I32, logits.shape, 0)
    m1 = jnp.max(logits, axis=0, keepdims=True)
    i1 = jnp.min(jnp.where(logits == m1, e, N_EXPERTS), axis=0, keepdims=True)
    rest = jnp.where(e == i1, -jnp.inf, logits)
    m2 = jnp.max(rest, axis=0, keepdims=True)
    i2 = jnp.min(jnp.where(rest == m2, e, N_EXPERTS), axis=0, keepdims=True)
    r = jnp.exp(m2 - m1)
    idx_ref[...] = jnp.concatenate([i1, i2], axis=0)
    gate_ref[...] = jnp.concatenate([1.0 / (1.0 + r), r / (1.0 + r)], axis=0)

    @pl.when(pl.program_id(0) == 0)
    def _():
        run_ref[...] = jnp.zeros(run_ref.shape, F32)

    tm = logits.shape[1]
    uses = jnp.where(jnp.logical_or(e == i1, e == i2), 1.0, 0.0)
    earlier = lax.broadcasted_iota(I32, (tm, tm), 0) < lax.broadcasted_iota(I32, (tm, tm), 1)
    prefix = jnp.dot(uses.astype(BF16), jnp.where(earlier, 1.0, 0.0).astype(BF16), preferred_element_type=F32)
    prefix = prefix + run_ref[:, 0:1]
    ranks = [jnp.sum(jnp.where(e == i, prefix, 0.0), axis=0, keepdims=True) for i in (i1, i2)]
    rank_ref[...] = jnp.concatenate(ranks, axis=0).astype(I32)
    run_ref[...] = run_ref[...] + jnp.sum(uses, axis=1, keepdims=True)
    count_ref[...] = run_ref[...]


def _router(x, w_router_t):
    t, d = x.shape
    tm = TM_WIDE
    return pl.pallas_call(
        _router_kernel,
        grid=(t // tm,),
        in_specs=[
            pl.BlockSpec((tm, d), lambda i: (i, 0)),
            pl.BlockSpec((N_EXPERTS, d), lambda i: (0, 0)),
        ],
        out_specs=[
            pl.BlockSpec((TOP_K, tm), lambda i: (0, i)),
            pl.BlockSpec((TOP_K, tm), lambda i: (0, i)),
            pl.BlockSpec((tm, d // 2), lambda i: (i, 0)),
            pl.BlockSpec((TOP_K, tm), lambda i: (0, i)),
            pl.BlockSpec((N_EXPERTS, LANES), lambda i: (0, 0)),
        ],
        out_shape=[jax.ShapeDtypeStruct((TOP_K, t), I32), jax.ShapeDtypeStruct((TOP_K, t), F32),
                   jax.ShapeDtypeStruct((t, d // 2), U32), jax.ShapeDtypeStruct((TOP_K, t), I32),
                   jax.ShapeDtypeStruct((N_EXPERTS, LANES), F32)],
        scratch_shapes=[pltpu.VMEM((N_EXPERTS, LANES), F32)],
        compiler_params=_params("arbitrary"),
        name="router",
    )(x, w_router_t)


def _moe_route(x, w_router):
    t, d = x.shape
    idx, gates, x_packed, rank, count = _router(x, w_router.T)

    counts = count[:, 0].astype(I32)
    padded = ((counts + TM_EXP - 1) // TM_EXP) * TM_EXP
    ends = jnp.cumsum(padded)
    experts = jnp.arange(N_EXPERTS, dtype=I32)[:, None, None]
    pos = rank + jnp.sum(jnp.where(idx[None] == experts, (ends - padded)[:, None, None], 0), axis=0)
    n_tiles = TOP_K * t // TM_EXP + N_EXPERTS
    tile_ids = jnp.arange(n_tiles, dtype=I32)
    tile_expert = jnp.minimum(
        jnp.sum((tile_ids[:, None] >= (ends // TM_EXP)[None, :]).astype(I32), axis=1), N_EXPERTS - 1).astype(I32)
    n_valid = (ends[-1:] // TM_EXP).astype(I32)

    n_rows = n_tiles * TM_EXP
    return dict(x_packed=x_packed, pos0=pos[0], pos1=pos[1], gates=gates.T, tile_expert=tile_expert, n_valid=n_valid,
                unused=_unused_rows(counts, padded, ends, n_rows), n_rows=n_rows)


def _moe_layer(streams, w_router, w_gate, w_up, w_down, layer, g, b):
    routes = [_moe_route(x, w_router) for x in streams]
    xs = [_sc_dispatch(r["x_packed"], r["pos0"], r["pos1"], r["unused"], r["n_rows"]) for r in routes]
    results = []
    for x, r, x_sorted in zip(streams, routes, xs):
        ys = _expert_ffn(x_sorted, r["tile_expert"], r["n_valid"], w_gate, w_up, w_down, layer)
        rows = x.shape[0] // COMBINE_PIECES
        gathered = [_sc_gather2(ys, r["pos0"][p * rows:(p + 1) * rows], r["pos1"][p * rows:(p + 1) * rows])
                    for p in range(COMBINE_PIECES)]
        out = None
        for p, (y0, y1) in enumerate(gathered):
            out = _moe_sum(x, y0, y1, r["gates"], g, b, p, out)
        results.append(out)
    return results


COMBINE_PIECES = 2
SC_CORES = 2
SC_SUBCORES = 16
SC_ROWS = 128


def _sc_worker_base(per_worker):
    return (lax.axis_index("subcore") * SC_CORES + lax.axis_index("core")) * per_worker


def _unused_rows(counts, padded, ends, n_rows):
    lane = jnp.arange(SC_ROWS, dtype=I32)
    chunk0 = jnp.arange(TM_EXP // SC_ROWS, dtype=I32) * SC_ROWS
    spare = n_rows + lane
    pad = (ends - padded + counts)[:, None, None] + chunk0[None, :, None] + lane[None, None, :]
    pad = jnp.where(pad < ends[:, None, None], pad, spare[None, None, :])
    tail = ends[-1] + jnp.arange(N_EXPERTS * TM_EXP // SC_ROWS, dtype=I32)[:, None] * SC_ROWS + lane[None, :]
    tail = jnp.where(tail < n_rows, tail, spare[None, :])
    return jnp.concatenate([pad.reshape(-1), tail.reshape(-1)]).astype(I32)


def _sc_dispatch(x, pos0, pos1, unused, n_rows):
    t, d = x.shape
    workers = SC_CORES * SC_SUBCORES
    per_worker = t // workers
    zero_chunks = unused.shape[0] // SC_ROWS // workers
    mesh = plsc.VectorSubcoreMesh(core_axis_name="core", subcore_axis_name="subcore")

    @pl.kernel(out_type=jax.ShapeDtypeStruct((n_rows + SC_ROWS, d), x.dtype), mesh=mesh,
               scratch_types=[pltpu.VMEM((SC_ROWS,), I32), pltpu.VMEM((SC_ROWS,), I32),
                              pltpu.VMEM((SC_ROWS, d), x.dtype)])
    def run(x_hbm, p0_hbm, p1_hbm, unused_hbm, zeros_hbm, o_hbm, i0_v, i1_v, rows_v):
        pltpu.sync_copy(zeros_hbm, rows_v)
        zbase = _sc_worker_base(zero_chunks * SC_ROWS)

        @pl.loop(0, zero_chunks)
        def _(c):
            pltpu.sync_copy(unused_hbm.at[pl.ds(zbase + c * SC_ROWS, SC_ROWS)], i0_v)
            pltpu.sync_copy(rows_v, o_hbm.at[i0_v])

        base = _sc_worker_base(per_worker)

        @pl.loop(0, per_worker // SC_ROWS)
        def _(c):
            off = base + c * SC_ROWS
            pltpu.sync_copy(p0_hbm.at[pl.ds(off, SC_ROWS)], i0_v)
            pltpu.sync_copy(p1_hbm.at[pl.ds(off, SC_ROWS)], i1_v)
            pltpu.sync_copy(x_hbm.at[pl.ds(off, SC_ROWS)], rows_v)
            pltpu.sync_copy(rows_v, o_hbm.at[i0_v])
            pltpu.sync_copy(rows_v, o_hbm.at[i1_v])

    return run(x, pos0, pos1, unused, jnp.zeros((SC_ROWS, d), x.dtype))


def _sc_gather2(ys, pos0, pos1):
    t = pos0.shape[0]
    d = ys.shape[1]
    per_worker = t // (SC_CORES * SC_SUBCORES)
    mesh = plsc.VectorSubcoreMesh(core_axis_name="core", subcore_axis_name="subcore")
    out = jax.ShapeDtypeStruct((t, d), ys.dtype)

    @pl.kernel(out_type=(out, out), mesh=mesh,
               scratch_types=[pltpu.VMEM((SC_ROWS,), I32), pltpu.VMEM((SC_ROWS, d), ys.dtype)])
    def run(y_hbm, p0_hbm, p1_hbm, o0_hbm, o1_hbm, i_v, rows_v):
        base = _sc_worker_base(per_worker)

        @pl.loop(0, per_worker // SC_ROWS)
        def _(c):
            off = base + c * SC_ROWS
            for p_hbm, o_hbm in ((p0_hbm, o0_hbm), (p1_hbm, o1_hbm)):
                pltpu.sync_copy(p_hbm.at[pl.ds(off, SC_ROWS)], i_v)
                pltpu.sync_copy(y_hbm.at[i_v], rows_v)
                pltpu.sync_copy(rows_v, o_hbm.at[pl.ds(off, SC_ROWS)])

    return run(ys, pos0, pos1)


def _moe_sum_kernel(x_ref, y0_ref, y1_ref, gate_ref, g_ref, b_ref, *rest):
    o_ref = rest[-1]
    gates = gate_ref[...]
    ff = gates[:, 0:1] * _unpack_bf16_pairs(y0_ref[...]) + gates[:, 1:2] * _unpack_bf16_pairs(y1_ref[...])
    o_ref[...] = _layer_norm(ALPHA * x_ref[...] + ff, g_ref[...], b_ref[...])


def _moe_sum(x, y0, y1, gates, g, b, piece, partial_out):
    t, d = x.shape
    rows = y0.shape[0]
    tm = TM_WIDE
    first = piece * rows // tm
    const = lambda i: (0, 0)
    row = lambda i: (i, 0)
    full_row = lambda i: (first + i, 0)
    in_specs = [pl.BlockSpec((tm, d), full_row), pl.BlockSpec((tm, d // 2), row), pl.BlockSpec((tm, d // 2), row),
                pl.BlockSpec((tm, TOP_K), full_row), pl.BlockSpec((1, d), const), pl.BlockSpec((1, d), const)]
    operands = [x, y0, y1, gates, g, b]
    aliases = {}
    if partial_out is not None:
        in_specs.append(pl.BlockSpec(memory_space=pl.ANY))
        operands.append(partial_out)
        aliases = {len(operands) - 1: 0}
    return pl.pallas_call(
        _moe_sum_kernel,
        grid=(rows // tm,),
        in_specs=in_specs,
        out_specs=pl.BlockSpec((tm, d), full_row),
        out_shape=jax.ShapeDtypeStruct((t, d), F32),
        input_output_aliases=aliases,
        compiler_params=_params("arbitrary"),
        name="moe_sum",
    )(*operands)


MAX_QBLKS_PER_STEP = 8


def _attn_kernel(q_ref, kp_ref, kc_ref, vp_ref, vc_ref, o_ref, m_ref, l_ref, kcat_ref, vcat_ref):
    n_pairs = q_ref.shape[-1] // LANES
    qi = lax.broadcasted_iota(I32, (QBLK, 2 * QBLK), 0)
    ki = lax.broadcasted_iota(I32, (QBLK, 2 * QBLK), 1)
    in_band = ki - QBLK <= qi
    lane = lax.broadcasted_iota(I32, (QBLK, LANES), 1)
    low_half = lane < HEAD_DIM
    for pr in range(n_pairs):
        vcat_ref[:, :, (2 * pr + 1) * LANES:(2 * pr + 2) * LANES] = jnp.ones(vcat_ref.shape[:2] + (LANES,), BF16)
    m_ref[...] = jnp.zeros(m_ref.shape, F32)
    l_ref[...] = jnp.ones(l_ref.shape, F32)

    def block(r, j):
        q_r, o_r, m_r, l_r = q_ref.at[r], o_ref.at[r], m_ref.at[r], l_ref.at[r]
        rows = slice(j * QBLK, (j + 1) * QBLK)
        keys = slice(j * QBLK, (j + 2) * QBLK)
        first_ok = qi if j > 0 else qi + jnp.where(pl.program_id(2) > 0, 0, QBLK)
        mask = jnp.logical_and(jnp.logical_or(ki >= first_ok, ki >= QBLK), in_band)
        for pr in range(n_pairs):
            cols = slice(pr * LANES, (pr + 1) * LANES)
            q2 = q_r[rows, cols]
            k2 = kcat_ref[r, keys, cols]
            v_ext = vcat_ref[r, keys, 2 * pr * LANES:(2 * pr + 2) * LANES]
            o_pair = None
            for a in range(2):
                own = low_half if a == 0 else jnp.logical_not(low_half)
                qm = jnp.where(own, q2, jnp.zeros_like(q2))
                s = lax.dot_general(qm, k2, _NT, preferred_element_type=F32)
                s = jnp.where(mask, s, NEG_BIG)
                m = jnp.max(s, axis=-1, keepdims=True)
                p = jnp.exp(s - m).astype(BF16)
                oe = jnp.dot(p, v_ext, preferred_element_type=F32)
                o_pair = oe[:, :LANES] if a == 0 else jnp.where(low_half, o_pair, oe[:, :LANES])
                h = 2 * pr + a
                m_r[rows, h:h + 1] = m
                l_r[rows, h:h + 1] = oe[:, LANES + h:LANES + h + 1]
            o_r[rows, cols] = o_pair.astype(o_r.dtype)

    for r in range(q_ref.shape[0]):
        kcat_ref[r, 0:QBLK, :] = kp_ref[r]
        kcat_ref[r, QBLK:, :] = kc_ref[r]
        for pr in range(n_pairs):
            vcat_ref[r, 0:QBLK, 2 * pr * LANES:(2 * pr + 1) * LANES] = vp_ref[r, :, pr * LANES:(pr + 1) * LANES]
            vcat_ref[r, QBLK:, 2 * pr * LANES:(2 * pr + 1) * LANES] = vc_ref[r, :, pr * LANES:(pr + 1) * LANES]
    for r in range(q_ref.shape[0]):
        for j in range(q_ref.shape[1] // QBLK):
            block(r, j)


def _group_attention(q, k, v):
    bsz, dil, length, width = q.shape
    blocks_per_step = min(MAX_QBLKS_PER_STEP, length // QBLK)
    classes_per_step = min(dil, MAX_QBLKS_PER_STEP // blocks_per_step)
    step = blocks_per_step * QBLK
    blk = (None, classes_per_step, step, width)
    cur = lambda b, r, n: (b, r, n, 0)
    prev_blk = (None, classes_per_step, QBLK, width)
    prev = lambda b, r, n: (b, r, jnp.maximum(blocks_per_step * n - 1, 0), 0)
    stat_spec = pl.BlockSpec((None, classes_per_step, step, LANES), cur)
    stat_shape = jax.ShapeDtypeStruct((bsz, dil, length, LANES), F32)
    return pl.pallas_call(
        _attn_kernel,
        grid=(bsz, dil // classes_per_step, length // step),
        in_specs=[
            pl.BlockSpec(blk, cur),
            pl.BlockSpec(prev_blk, prev),
            pl.BlockSpec(blk, cur),
            pl.BlockSpec(prev_blk, prev),
            pl.BlockSpec(blk, cur),
        ],
        out_specs=[pl.BlockSpec(blk, cur), stat_spec, stat_spec],
        out_shape=[jax.ShapeDtypeStruct((bsz, dil, length, width), BF16), stat_shape, stat_shape],
        scratch_shapes=[pltpu.VMEM((classes_per_step, step + QBLK, width), BF16),
                        pltpu.VMEM((classes_per_step, step + QBLK, 2 * width), BF16)],
        compiler_params=_params("arbitrary", "arbitrary", "arbitrary"),
        name="group_attention",
    )(q, k, k, v, v)


def _to_token_order(src_ref, dst_ref):
    dil, n, w = src_ref.shape
    for r in range(dil):
        for c in range(w // LANES):
            dst_ref[c, pl.ds(r, n, stride=dil), :] = src_ref[r, :, c * LANES:(c + 1) * LANES].astype(dst_ref.dtype)
    return jnp.concatenate([dst_ref[c] for c in range(w // LANES)], axis=1)


def _attn_out_kernel(x_ref, o0_ref, o1_ref, o2_ref, m0_ref, m1_ref, m2_ref, l0_ref, l1_ref, l2_ref,
                     wo_ref, g_ref, b_ref, out_ref, o_tok, stat_tok):
    width = o0_ref.shape[-1]

    def token_order(ref, scratch):
        dil, n, w = ref.shape
        if dil == 1:
            return ref[0].astype(F32)
        if dil % SUBLANES == 0:
            return jnp.swapaxes(ref[...].astype(F32), 0, 1).reshape(dil * n, w)
        return _to_token_order(ref, scratch)

    ms = [token_order(r, stat_tok.at[i]) for i, r in enumerate((m0_ref, m1_ref, m2_ref))]
    ls = [token_order(r, stat_tok.at[N_GROUPS + i]) for i, r in enumerate((l0_ref, l1_ref, l2_ref))]
    top = jnp.maximum(jnp.maximum(ms[0], ms[1]), ms[2])
    es = [jnp.exp(m - top) for m in ms]
    den = es[0] * ls[0] + es[1] * ls[1] + es[2] * ls[2]
    head = lax.broadcasted_iota(I32, (LANES, width), 0)
    lane = lax.broadcasted_iota(I32, (LANES, width), 1)
    spread = (lane // HEAD_DIM == head).astype(BF16)
    mixed = jnp.zeros(out_ref.shape[:1] + (width,), F32)
    for e, o_ref in zip(es, (o0_ref, o1_ref, o2_ref)):
        hi, lo = _split_bf16(e / den)
        wide = (jnp.dot(hi, spread, preferred_element_type=F32)
                + jnp.dot(lo, spread, preferred_element_type=F32))
        mixed = mixed + wide * token_order(o_ref, o_tok)
    y = jnp.dot(mixed.astype(BF16), wo_ref[...], preferred_element_type=F32)
    out_ref[...] = _layer_norm(ALPHA * x_ref[...] + y, g_ref[...], b_ref[...])


def _attn_out(x, bsz, outs, stats, w_o, g, b):
    t, d = x.shape
    width = w_o.shape[0]
    tm = TM_RES
    tiles_per_seq = t // bsz // tm
    const = lambda bi, c: (0, 0)
    row = lambda bi, c: (bi * tiles_per_seq + c, 0)

    def res_spec(a):
        dil, w = a.shape[1], a.shape[3]
        return pl.BlockSpec((None, dil, tm // dil, w), lambda bi, c: (bi, 0, c, 0))

    return pl.pallas_call(
        _attn_out_kernel,
        grid=(bsz, tiles_per_seq),
        in_specs=([pl.BlockSpec((tm, d), row)] + [res_spec(a) for a in outs] + [res_spec(a) for a in stats]
                  + [pl.BlockSpec((width, d), const), pl.BlockSpec((1, d), const), pl.BlockSpec((1, d), const)]),
        out_specs=pl.BlockSpec((tm, d), row),
        out_shape=jax.ShapeDtypeStruct((t, d), F32),
        scratch_shapes=[pltpu.VMEM((width // LANES, tm, LANES), F32), pltpu.VMEM((2 * N_GROUPS, 1, tm, LANES), F32)],
        compiler_params=_params("arbitrary", "arbitrary"),
        name="attn_out",
    )(x, *outs, *stats, w_o, g, b)


def _proj_residue_kernel(x_ref, w_ref, *rest, dils, scale):
    outs, y_ref = rest[:-1], rest[-1]
    _, n_slabs, tm, _ = y_ref.shape
    width = n_slabs * LANES
    xb = x_ref[...].astype(BF16)
    for c, (o_ref, dil) in enumerate(zip(outs, dils)):
        y = jnp.dot(xb, w_ref[:, c * width:(c + 1) * width], preferred_element_type=F32)
        if scale != 1.0:
            y = y * scale
        if dil == 1:
            o_ref[0] = y.astype(o_ref.dtype)
        elif dil % SUBLANES == 0:
            o_ref[...] = jnp.swapaxes(y.reshape(tm // dil, dil, width), 0, 1).astype(o_ref.dtype)
        else:
            for s in range(n_slabs):
                y_ref[c, s] = y[:, s * LANES:(s + 1) * LANES]
            for r in range(dil):
                rows = [y_ref[c, s, pl.ds(r, tm // dil, stride=dil), :] for s in range(n_slabs)]
                o_ref[r] = jnp.concatenate(rows, axis=1).astype(o_ref.dtype)


def _proj_residue(x, bsz, w, dils, scale=1.0):
    t, d = x.shape
    seq = t // bsz
    width = w.shape[1] // len(dils)
    tm = TM_WIDE
    tiles_per_seq = seq // tm
    return pl.pallas_call(
        functools.partial(_proj_residue_kernel, dils=dils, scale=scale),
        grid=(bsz, tiles_per_seq),
        in_specs=[pl.BlockSpec((tm, d), lambda bi, c: (bi * tiles_per_seq + c, 0)),
                  pl.BlockSpec(w.shape, lambda bi, c: (0, 0))],
        out_specs=[pl.BlockSpec((None, dil, tm // dil, width), lambda bi, c: (bi, 0, c, 0)) for dil in dils],
        out_shape=[jax.ShapeDtypeStruct((bsz, dil, seq // dil, width), BF16) for dil in dils],
        scratch_shapes=[pltpu.VMEM((len(dils), width // LANES, tm, LANES), F32)],
        compiler_params=_params("arbitrary", "arbitrary"),
        name="proj_residue",
    )(x, w)


def _attention_layer(x, bsz, w_q, w_o, kv_groups, g, b):
    dils = tuple(dil for _, dil in ATT_GROUPS)
    qs = _proj_residue(x, bsz, w_q, dils, scale=HEAD_DIM ** -0.5)
    outs, maxes, dens = [], [], []
    for q, (k_res, v_res) in zip(qs, kv_groups):
        o, m, l = _group_attention(q, k_res, v_res)
        outs.append(o)
        maxes.append(m)
        dens.append(l)
    return _attn_out(x, bsz, outs, maxes + dens, w_o, g, b)


def kernel(x, a_w_in, a_conv, a_w_out, kv_w, b_w_q, b_w_o, ffn_w_gate, ffn_w_up, ffn_w_down,
           moe_w_router, moe_w_gate, moe_w_up, moe_w_down, ln_g, ln_b):
    bsz, seq, d = x.shape
    sb = bsz // N_STREAMS
    hs = [x[s * sb:(s + 1) * sb].reshape(sb * seq, d) for s in range(N_STREAMS)]
    ln_g = ln_g.reshape(DEPTH, 2, 1, d)
    ln_b = ln_b.reshape(DEPTH, 2, 1, d)
    kv_dils = tuple(dil for _, dil in ATT_GROUPS for _ in range(2))
    kv_groups = None
    for l in range(DEPTH):
        if l < N_A_LAYERS:
            w_in, w_out = a_w_in[l].astype(BF16), a_w_out[l].astype(BF16)
            hs = [_a_mixer(h, w_in, a_conv[l], w_out, ln_g[l, 0], ln_b[l, 0], seq) for h in hs]
        else:
            if kv_groups is None:
                kv_wb = kv_w.astype(BF16)
                kvs = [_proj_residue(h, sb, kv_wb, kv_dils) for h in hs]
                kv_groups = [[(kv[2 * gi], kv[2 * gi + 1]) for gi in range(N_GROUPS)] for kv in kvs]
            j = l - N_A_LAYERS
            w_q, w_o = b_w_q[j].astype(BF16), b_w_o[j].astype(BF16)
            hs = [_attention_layer(h, sb, w_q, w_o, kvg, ln_g[l, 0], ln_b[l, 0]) for h, kvg in zip(hs, kv_groups)]
        i = l // 2
        if l % 2 == 0:
            w_gate, w_up, w_down = (w[i].astype(BF16) for w in (ffn_w_gate, ffn_w_up, ffn_w_down))
            hs = [_dense_ffn(h, w_gate, w_up, w_down, ln_g[l, 1], ln_b[l, 1]) for h in hs]
        else:
            hs = _moe_layer(hs, moe_w_router[i], moe_w_gate.astype(BF16), moe_w_up.astype(BF16),
                            moe_w_down.astype(BF16), i, ln_g[l, 1], ln_b[l, 1])
    return jnp.concatenate(hs, axis=0).reshape(bsz, seq, d)
```

```python
import functools

import jax
import jax.numpy as jnp
from jax import lax
from jax.experimental import pallas as pl
from jax.experimental.pallas import tpu as pltpu
from jax.experimental.pallas import tpu_sc as plsc

F32 = jnp.float32
U32 = jnp.uint32
BF16 = jnp.bfloat16
I32 = jnp.int32

DEPTH = 4
N_A_LAYERS = DEPTH // 2
CONV_WIDTH = 3
ATT_GROUPS = ((128, 1), (512, 4), (2048, 16))
N_GROUPS = len(ATT_GROUPS)
HEAD_DIM = 64
N_EXPERTS = 8
TOP_K = 2
ALPHA = (2.0 * DEPTH) ** 0.25
LN_EPS = 1e-5

QBLK = 128
assert all(w // d == QBLK for w, d in ATT_GROUPS)

LANES = 128
SUBLANES = 8
VMEM_LIMIT = 56 * 1024 * 1024
NEG_BIG = -1e30

TM_FFN = 512
TM_EXP = 512
TM_WIDE = 1024

_NT = (((1,), (1,)), ((), ()))


def _layer_norm(z, g, b):
    mu = jnp.mean(z, axis=-1, keepdims=True)
    zc = z - mu
    var = jnp.mean(zc * zc, axis=-1, keepdims=True)
    return zc * lax.rsqrt(var + LN_EPS) * g + b


def _params(*semantics):
    return pltpu.CompilerParams(dimension_semantics=semantics, vmem_limit_bytes=VMEM_LIMIT)


def _a_mixer_kernel(x_ref, win_ref, conv_ref, wout_ref, g_ref, b_ref, o_ref, ubuf, *, tiles_per_seq):
    tm, d = x_ref.shape
    piece = tm // A_MIXER_PIECES

    @pl.when(pl.program_id(0) % tiles_per_seq == 0)
    def _():
        ubuf[0:SUBLANES, :] = jnp.zeros((SUBLANES, d), F32)

    cw = conv_ref[...]
    for h in range(A_MIXER_PIECES):
        rows = slice(h * piece, (h + 1) * piece)
        first = SUBLANES + h * piece
        x = x_ref[rows, :]
        p = jnp.dot(x.astype(BF16), win_ref[...], preferred_element_type=F32)
        ubuf[first:first + piece, :] = p[:, d:2 * d] * p[:, 2 * d:]
        conv = (cw[2:3, :] * ubuf[first:first + piece, :]
                + cw[1:2, :] * ubuf[first - 1:first - 1 + piece, :]
                + cw[0:1, :] * ubuf[first - 2:first - 2 + piece, :])
        y = jnp.dot((p[:, :d] * conv).astype(BF16), wout_ref[...], preferred_element_type=F32)
        o_ref[rows, :] = _layer_norm(ALPHA * x + y, g_ref[...], b_ref[...])
    ubuf[0:SUBLANES, :] = ubuf[tm:tm + SUBLANES, :]


A_MIXER_PIECES = 2


def _a_mixer(x, w_in, conv_w, w_out, g, b, seq_len):
    t, d = x.shape
    tm = TM_WIDE
    const = lambda i: (0, 0)
    resident = dict(pipeline_mode=pl.Buffered(1))
    return pl.pallas_call(
        functools.partial(_a_mixer_kernel, tiles_per_seq=seq_len // tm),
        grid=(t // tm,),
        in_specs=[
            pl.BlockSpec((tm, d), lambda i: (i, 0)),
            pl.BlockSpec((d, 3 * d), const, **resident),
            pl.BlockSpec((CONV_WIDTH, d), const),
            pl.BlockSpec((d, d), const, **resident),
            pl.BlockSpec((1, d), const),
            pl.BlockSpec((1, d), const),
        ],
        out_specs=pl.BlockSpec((tm, d), lambda i: (i, 0)),
        out_shape=jax.ShapeDtypeStruct((t, d), F32),
        scratch_shapes=[pltpu.VMEM((tm + SUBLANES, d), F32)],
        compiler_params=_params("arbitrary"),
        name="a_mixer",
    )(x, w_in, conv_w, w_out, g, b)


def _swiglu(xb, wg_ref, wu_ref, wd_ref):
    gate = jnp.dot(xb, wg_ref[...], preferred_element_type=F32)
    up = jnp.dot(xb, wu_ref[...], preferred_element_type=F32)
    h = (gate * jax.nn.sigmoid(gate) * up).astype(BF16)
    return jnp.dot(h, wd_ref[...], preferred_element_type=F32)


def _dense_ffn_kernel(x_ref, wg_ref, wu_ref, wd_ref, g_ref, b_ref, o_ref):
    x = x_ref[...]
    y = _swiglu(x.astype(BF16), wg_ref, wu_ref, wd_ref)
    o_ref[...] = _layer_norm(ALPHA * x + y, g_ref[...], b_ref[...])


def _dense_ffn(x, w_gate, w_up, w_down, g, b):
    t, d = x.shape
    d_ff = w_gate.shape[1]
    tm = TM_FFN
    const = lambda i: (0, 0)
    resident = dict(pipeline_mode=pl.Buffered(1))
    return pl.pallas_call(
        _dense_ffn_kernel,
        grid=(t // tm,),
        in_specs=[
            pl.BlockSpec((tm, d), lambda i: (i, 0)),
            pl.BlockSpec((d, d_ff), const, **resident),
            pl.BlockSpec((d, d_ff), const, **resident),
            pl.BlockSpec((d_ff, d), const, **resident),
            pl.BlockSpec((1, d), const),
            pl.BlockSpec((1, d), const),
        ],
        out_specs=pl.BlockSpec((tm, d), lambda i: (i, 0)),
        out_shape=jax.ShapeDtypeStruct((t, d), F32),
        compiler_params=_params("arbitrary"),
        name="dense_ffn",
    )(x, w_gate, w_up, w_down, g, b)


EXPERT_COL_SPLITS = 2


def _expert_ffn_kernel(te_ref, nv_ref, x_ref, wg_ref, wu_ref, wd_ref, o_ref):
    del te_ref
    valid = pl.program_id(0) < nv_ref[0]

    @pl.when(valid)
    def _():
        xb = _unpack_bf16_pairs(x_ref[...]).astype(BF16)
        piece = wg_ref.shape[1] // EXPERT_COL_SPLITS
        y = None
        for c in range(EXPERT_COL_SPLITS):
            cols = slice(c * piece, (c + 1) * piece)
            gate = jnp.dot(xb, wg_ref[:, cols], preferred_element_type=F32)
            up = jnp.dot(xb, wu_ref[:, cols], preferred_element_type=F32)
            h = (gate * jax.nn.sigmoid(gate) * up).astype(BF16)
            part = jnp.dot(h, wd_ref[cols, :], preferred_element_type=F32)
            y = part if y is None else y + part
        o_ref[...] = _pack_bf16_pairs(y)

    @pl.when(jnp.logical_not(valid))
    def _():
        o_ref[...] = jnp.zeros(o_ref.shape, U32)


def _expert_ffn(xs, tile_expert, n_valid, w_gate, w_up, w_down, layer):
    d = w_gate.shape[2]
    d_exp = w_gate.shape[3]
    tm = TM_EXP
    n_rows = xs.shape[0] // tm * tm
    weights_of_tile = lambda i, te, nv: (layer, te[i], 0, 0)
    resident = dict(pipeline_mode=pl.Buffered(1))
    return pl.pallas_call(
        _expert_ffn_kernel,
        grid_spec=pltpu.PrefetchScalarGridSpec(
            num_scalar_prefetch=2,
            grid=(n_rows // tm,),
            in_specs=[
                pl.BlockSpec((tm, d // 2), lambda i, te, nv: (i, 0)),
                pl.BlockSpec((None, None, d, d_exp), weights_of_tile, **resident),
                pl.BlockSpec((None, None, d, d_exp), weights_of_tile, **resident),
                pl.BlockSpec((None, None, d_exp, d), weights_of_tile, **resident),
            ],
            out_specs=pl.BlockSpec((tm, d // 2), lambda i, te, nv: (i, 0)),
        ),
        out_shape=jax.ShapeDtypeStruct((n_rows, d // 2), U32),
        compiler_params=_params("arbitrary"),
        name="expert_ffn",
    )(tile_expert, n_valid, xs, w_gate, w_up, w_down)


def _split_bf16(v):
    hi = v.astype(BF16)
    lo = (v - hi.astype(F32)).astype(BF16)
    return hi, lo


def _pack_bf16_pairs(v):
    w = v.shape[1] // 2
    lo = lax.bitcast_convert_type(v[:, :w].astype(BF16).astype(F32), U32)
    hi = lax.bitcast_convert_type(v[:, w:].astype(BF16).astype(F32), U32)
    return (lo >> 16) | hi


def _unpack_bf16_pairs(p):
    lo = lax.bitcast_convert_type(p << 16, F32)
    hi = lax.bitcast_convert_type(p & jnp.uint32(0xFFFF0000), F32)
    return jnp.concatenate([lo, hi], axis=1)


def _router_kernel(x_ref, wr_ref, idx_ref, gate_ref, xp_ref, rank_ref, count_ref, run_ref):
    xp_ref[...] = _pack_bf16_pairs(x_ref[...])
    xh, xl = _split_bf16(x_ref[...])
    wh, wl = _split_bf16(wr_ref[...])
    logits = (lax.dot_general(wh, xh, _NT, preferred_element_type=F32)
              + lax.dot_general(wh, xl, _NT, preferred_element_type=F32)
              + lax.dot_general(wl, xh, _NT, preferred_element_type=F32))
    e = lax.broadcasted_iota(I32, logits.shape, 0)
    m1 = jnp.max(logits, axis=0, keepdims=True)
    i1 = jnp.min(jnp.where(logits == m1, e, N_EXPERTS), axis=0, keepdims=True)
    rest = jnp.where(e == i1, -jnp.inf, logits)
    m2 = jnp.max(rest, axis=0, keepdims=True)
    i2 = jnp.min(jnp.where(rest == m2, e, N_EXPERTS), axis=0, keepdims=True)
    r = jnp.exp(m2 - m1)
    idx_ref[...] = jnp.concatenate([i1, i2], axis=0)
    gate_ref[...] = jnp.concatenate([1.0 / (1.0 + r), r / (1.0 + r)], axis=0)

    @pl.when(pl.program_id(0) == 0)
    def _():
        run_ref[...] = jnp.zeros(run_ref.shape, F32)

    tm = logits.shape[1]
    uses = jnp.where(jnp.logical_or(e == i1, e == i2), 1.0, 0.0)
    earlier = lax.broadcasted_iota(I32, (tm, tm), 0) < lax.broadcasted_iota(I32, (tm, tm), 1)
    prefix = jnp.dot(uses.astype(BF16), jnp.where(earlier, 1.0, 0.0).astype(BF16), preferred_element_type=F32)
    prefix = prefix + run_ref[:, 0:1]
    ranks = [jnp.sum(jnp.where(e == i, prefix, 0.0), axis=0, keepdims=True) for i in (i1, i2)]
    rank_ref[...] = jnp.concatenate(ranks, axis=0).astype(I32)
    run_ref[...] = run_ref[...] + jnp.sum(uses, axis=1, keepdims=True)
    count_ref[...] = run_ref[...]


def _router(x, w_router_t):
    t, d = x.shape
    tm = TM_WIDE
    return pl.pallas_call(
        _router_kernel,
        grid=(t // tm,),
        in_specs=[
            pl.BlockSpec((tm, d), lambda i: (i, 0)),
            pl.BlockSpec((N_EXPERTS, d), lambda i: (0, 0)),
        ],
        out_specs=[
            pl.BlockSpec((TOP_K, tm), lambda i: (0, i)),
            pl.BlockSpec((TOP_K, tm), lambda i: (0, i)),
            pl.BlockSpec((tm, d // 2), lambda i: (i, 0)),
            pl.BlockSpec((TOP_K, tm), lambda i: (0, i)),
            pl.BlockSpec((N_EXPERTS, LANES), lambda i: (0, 0)),
        ],
        out_shape=[jax.ShapeDtypeStruct((TOP_K, t), I32), jax.ShapeDtypeStruct((TOP_K, t), F32),
                   jax.ShapeDtypeStruct((t, d // 2), U32), jax.ShapeDtypeStruct((TOP_K, t), I32),
                   jax.ShapeDtypeStruct((N_EXPERTS, LANES), F32)],
        scratch_shapes=[pltpu.VMEM((N_EXPERTS, LANES), F32)],
        compiler_params=_params("arbitrary"),
        name="router",
    )(x, w_router_t)


def _moe_route(x, w_router):
    t, d = x.shape
    idx, gates, x_packed, rank, count = _router(x, w_router.T)

    counts = count[:, 0].astype(I32)
    padded = ((counts + TM_EXP - 1) // TM_EXP) * TM_EXP
    ends = jnp.cumsum(padded)
    experts = jnp.arange(N_EXPERTS, dtype=I32)[:, None, None]
    pos = rank + jnp.sum(jnp.where(idx[None] == experts, (ends - padded)[:, None, None], 0), axis=0)
    n_tiles = TOP_K * t // TM_EXP + N_EXPERTS
    tile_ids = jnp.arange(n_tiles, dtype=I32)
    tile_expert = jnp.minimum(
        jnp.sum((tile_ids[:, None] >= (ends // TM_EXP)[None, :]).astype(I32), axis=1), N_EXPERTS - 1).astype(I32)
    n_valid = (ends[-1:] // TM_EXP).astype(I32)

    n_rows = n_tiles * TM_EXP
    return dict(x_packed=x_packed, pos0=pos[0], pos1=pos[1], gates=gates.T, tile_expert=tile_expert, n_valid=n_valid,
                unused=_unused_rows(counts, padded, ends, n_rows), n_rows=n_rows)


def _moe_layer(x, w_router, w_gate, w_up, w_down, layer, g, b):
    r = _moe_route(x, w_router)
    xs = _sc_dispatch(r["x_packed"], r["pos0"], r["pos1"], r["unused"], r["n_rows"])
    ys = _expert_ffn(xs, r["tile_expert"], r["n_valid"], w_gate, w_up, w_down, layer)
    rows = x.shape[0] // COMBINE_PIECES
    gathered = [_sc_gather2(ys, r["pos0"][p * rows:(p + 1) * rows], r["pos1"][p * rows:(p + 1) * rows])
                for p in range(COMBINE_PIECES)]
    out = None
    for p, (y0, y1) in enumerate(gathered):
        out = _moe_sum(x, y0, y1, r["gates"], g, b, p, out)
    return out


COMBINE_PIECES = 2
SC_CORES = 2
SC_SUBCORES = 16
SC_ROWS = 128


def _sc_worker_base(per_worker):
    return (lax.axis_index("subcore") * SC_CORES + lax.axis_index("core")) * per_worker


def _unused_rows(counts, padded, ends, n_rows):
    lane = jnp.arange(SC_ROWS, dtype=I32)
    chunk0 = jnp.arange(TM_EXP // SC_ROWS, dtype=I32) * SC_ROWS
    spare = n_rows + lane
    pad = (ends - padded + counts)[:, None, None] + chunk0[None, :, None] + lane[None, None, :]
    pad = jnp.where(pad < ends[:, None, None], pad, spare[None, None, :])
    tail = ends[-1] + jnp.arange(N_EXPERTS * TM_EXP // SC_ROWS, dtype=I32)[:, None] * SC_ROWS + lane[None, :]
    tail = jnp.where(tail < n_rows, tail, spare[None, :])
    return jnp.concatenate([pad.reshape(-1), tail.reshape(-1)]).astype(I32)


def _sc_dispatch(x, pos0, pos1, unused, n_rows):
    t, d = x.shape
    workers = SC_CORES * SC_SUBCORES
    per_worker = t // workers
    zero_chunks = unused.shape[0] // SC_ROWS // workers
    mesh = plsc.VectorSubcoreMesh(core_axis_name="core", subcore_axis_name="subcore")

    @pl.kernel(out_type=jax.ShapeDtypeStruct((n_rows + SC_ROWS, d), x.dtype), mesh=mesh,
               scratch_types=[pltpu.VMEM((SC_ROWS,), I32), pltpu.VMEM((SC_ROWS,), I32),
                              pltpu.VMEM((SC_ROWS, d), x.dtype)])
    def run(x_hbm, p0_hbm, p1_hbm, unused_hbm, zeros_hbm, o_hbm, i0_v, i1_v, rows_v):
        pltpu.sync_copy(zeros_hbm, rows_v)
        zbase = _sc_worker_base(zero_chunks * SC_ROWS)

        @pl.loop(0, zero_chunks)
        def _(c):
            pltpu.sync_copy(unused_hbm.at[pl.ds(zbase + c * SC_ROWS, SC_ROWS)], i0_v)
            pltpu.sync_copy(rows_v, o_hbm.at[i0_v])

        base = _sc_worker_base(per_worker)

        @pl.loop(0, per_worker // SC_ROWS)
        def _(c):
            off = base + c * SC_ROWS
            pltpu.sync_copy(p0_hbm.at[pl.ds(off, SC_ROWS)], i0_v)
            pltpu.sync_copy(p1_hbm.at[pl.ds(off, SC_ROWS)], i1_v)
            pltpu.sync_copy(x_hbm.at[pl.ds(off, SC_ROWS)], rows_v)
            pltpu.sync_copy(rows_v, o_hbm.at[i0_v])
            pltpu.sync_copy(rows_v, o_hbm.at[i1_v])

    return run(x, pos0, pos1, unused, jnp.zeros((SC_ROWS, d), x.dtype))


def _sc_gather2(ys, pos0, pos1):
    t = pos0.shape[0]
    d = ys.shape[1]
    per_worker = t // (SC_CORES * SC_SUBCORES)
    mesh = plsc.VectorSubcoreMesh(core_axis_name="core", subcore_axis_name="subcore")
    out = jax.ShapeDtypeStruct((t, d), ys.dtype)

    @pl.kernel(out_type=(out, out), mesh=mesh,
               scratch_types=[pltpu.VMEM((SC_ROWS,), I32), pltpu.VMEM((SC_ROWS, d), ys.dtype)])
    def run(y_hbm, p0_hbm, p1_hbm, o0_hbm, o1_hbm, i_v, rows_v):
        base = _sc_worker_base(per_worker)

        @pl.loop(0, per_worker // SC_ROWS)
        def _(c):
            off = base + c * SC_ROWS
            for p_hbm, o_hbm in ((p0_hbm, o0_hbm), (p1_hbm, o1_hbm)):
                pltpu.sync_copy(p_hbm.at[pl.ds(off, SC_ROWS)], i_v)
                pltpu.sync_copy(y_hbm.at[i_v], rows_v)
                pltpu.sync_copy(rows_v, o_hbm.at[pl.ds(off, SC_ROWS)])

    return run(ys, pos0, pos1)


def _moe_sum_kernel(x_ref, y0_ref, y1_ref, gate_ref, g_ref, b_ref, *rest):
    o_ref = rest[-1]
    gates = gate_ref[...]
    ff = gates[:, 0:1] * _unpack_bf16_pairs(y0_ref[...]) + gates[:, 1:2] * _unpack_bf16_pairs(y1_ref[...])
    o_ref[...] = _layer_norm(ALPHA * x_ref[...] + ff, g_ref[...], b_ref[...])


def _moe_sum(x, y0, y1, gates, g, b, piece, partial_out):
    t, d = x.shape
    rows = y0.shape[0]
    tm = TM_WIDE
    first = piece * rows // tm
    const = lambda i: (0, 0)
    row = lambda i: (i, 0)
    full_row = lambda i: (first + i, 0)
    in_specs = [pl.BlockSpec((tm, d), full_row), pl.BlockSpec((tm, d // 2), row), pl.BlockSpec((tm, d // 2), row),
                pl.BlockSpec((tm, TOP_K), full_row), pl.BlockSpec((1, d), const), pl.BlockSpec((1, d), const)]
    operands = [x, y0, y1, gates, g, b]
    aliases = {}
    if partial_out is not None:
        in_specs.append(pl.BlockSpec(memory_space=pl.ANY))
        operands.append(partial_out)
        aliases = {len(operands) - 1: 0}
    return pl.pallas_call(
        _moe_sum_kernel,
        grid=(rows // tm,),
        in_specs=in_specs,
        out_specs=pl.BlockSpec((tm, d), full_row),
        out_shape=jax.ShapeDtypeStruct((t, d), F32),
        input_output_aliases=aliases,
        compiler_params=_params("arbitrary"),
        name="moe_sum",
    )(*operands)


MAX_QBLKS_PER_STEP = 8


def _attn_kernel(q_ref, kp_ref, kc_ref, vp_ref, vc_ref, o_ref, m_ref, l_ref, kcat_ref, vcat_ref):
    n_pairs = q_ref.shape[-1] // LANES
    qi = lax.broadcasted_iota(I32, (QBLK, 2 * QBLK), 0)
    ki = lax.broadcasted_iota(I32, (QBLK, 2 * QBLK), 1)
    in_band = ki - QBLK <= qi
    lane = lax.broadcasted_iota(I32, (QBLK, LANES), 1)
    low_half = lane < HEAD_DIM
    for pr in range(n_pairs):
        vcat_ref[:, :, (2 * pr + 1) * LANES:(2 * pr + 2) * LANES] = jnp.ones(vcat_ref.shape[:2] + (LANES,), BF16)
    m_ref[...] = jnp.zeros(m_ref.shape, F32)
    l_ref[...] = jnp.ones(l_ref.shape, F32)

    def block(r, j):
        q_r, o_r, m_r, l_r = q_ref.at[r], o_ref.at[r], m_ref.at[r], l_ref.at[r]
        rows = slice(j * QBLK, (j + 1) * QBLK)
        keys = slice(j * QBLK, (j + 2) * QBLK)
        first_ok = qi if j > 0 else qi + jnp.where(pl.program_id(2) > 0, 0, QBLK)
        mask = jnp.logical_and(jnp.logical_or(ki >= first_ok, ki >= QBLK), in_band)
        for pr in range(n_pairs):
            cols = slice(pr * LANES, (pr + 1) * LANES)
            q2 = q_r[rows, cols]
            k2 = kcat_ref[r, keys, cols]
            v_ext = vcat_ref[r, keys, 2 * pr * LANES:(2 * pr + 2) * LANES]
            o_pair = None
            for a in range(2):
                own = low_half if a == 0 else jnp.logical_not(low_half)
                qm = jnp.where(own, q2, jnp.zeros_like(q2))
                s = lax.dot_general(qm, k2, _NT, preferred_element_type=F32)
                s = jnp.where(mask, s, NEG_BIG)
                m = jnp.max(s, axis=-1, keepdims=True)
                p = jnp.exp(s - m).astype(BF16)
                oe = jnp.dot(p, v_ext, preferred_element_type=F32)
                o_pair = oe[:, :LANES] if a == 0 else jnp.where(low_half, o_pair, oe[:, :LANES])
                h = 2 * pr + a
                m_r[rows, h:h + 1] = m
                l_r[rows, h:h + 1] = oe[:, LANES + h:LANES + h + 1]
            o_r[rows, cols] = o_pair.astype(o_r.dtype)

    for r in range(q_ref.shape[0]):
        kcat_ref[r, 0:QBLK, :] = kp_ref[r]
        kcat_ref[r, QBLK:, :] = kc_ref[r]
        for pr in range(n_pairs):
            vcat_ref[r, 0:QBLK, 2 * pr * LANES:(2 * pr + 1) * LANES] = vp_ref[r, :, pr * LANES:(pr + 1) * LANES]
            vcat_ref[r, QBLK:, 2 * pr * LANES:(2 * pr + 1) * LANES] = vc_ref[r, :, pr * LANES:(pr + 1) * LANES]
    for r in range(q_ref.shape[0]):
        for j in range(q_ref.shape[1] // QBLK):
            block(r, j)


def _group_attention(q, k, v):
    bsz, dil, length, width = q.shape
    blocks_per_step = min(MAX_QBLKS_PER_STEP, length // QBLK)
    classes_per_step = min(dil, MAX_QBLKS_PER_STEP // blocks_per_step)
    step = blocks_per_step * QBLK
    blk = (None, classes_per_step, step, width)
    cur = lambda b, r, n: (b, r, n, 0)
    prev_blk = (None, classes_per_step, QBLK, width)
    prev = lambda b, r, n: (b, r, jnp.maximum(blocks_per_step * n - 1, 0), 0)
    stat_spec = pl.BlockSpec((None, classes_per_step, step, LANES), cur)
    stat_shape = jax.ShapeDtypeStruct((bsz, dil, length, LANES), F32)
    return pl.pallas_call(
        _attn_kernel,
        grid=(bsz, dil // classes_per_step, length // step),
        in_specs=[
            pl.BlockSpec(blk, cur),
            pl.BlockSpec(prev_blk, prev),
            pl.BlockSpec(blk, cur),
            pl.BlockSpec(prev_blk, prev),
            pl.BlockSpec(blk, cur),
        ],
        out_specs=[pl.BlockSpec(blk, cur), stat_spec, stat_spec],
        out_shape=[jax.ShapeDtypeStruct((bsz, dil, length, width), BF16), stat_shape, stat_shape],
        scratch_shapes=[pltpu.VMEM((classes_per_step, step + QBLK, width), BF16),
                        pltpu.VMEM((classes_per_step, step + QBLK, 2 * width), BF16)],
        compiler_params=_params("arbitrary", "arbitrary", "arbitrary"),
        name="group_attention",
    )(q, k, k, v, v)


def _to_token_order(src_ref, dst_ref):
    dil, n, w = src_ref.shape
    for r in range(dil):
        for c in range(w // LANES):
            dst_ref[c, pl.ds(r, n, stride=dil), :] = src_ref[r, :, c * LANES:(c + 1) * LANES].astype(dst_ref.dtype)
    return jnp.concatenate([dst_ref[c] for c in range(w // LANES)], axis=1)


def _attn_out_kernel(x_ref, o0_ref, o1_ref, o2_ref, m0_ref, m1_ref, m2_ref, l0_ref, l1_ref, l2_ref,
                     wo_ref, g_ref, b_ref, out_ref, o_tok, stat_tok):
    width = o0_ref.shape[-1]

    def token_order(ref, scratch):
        dil, n, w = ref.shape
        if dil == 1:
            return ref[0].astype(F32)
        if dil % SUBLANES == 0:
            return jnp.swapaxes(ref[...].astype(F32), 0, 1).reshape(dil * n, w)
        return _to_token_order(ref, scratch)

    ms = [token_order(r, stat_tok.at[i]) for i, r in enumerate((m0_ref, m1_ref, m2_ref))]
    ls = [token_order(r, stat_tok.at[N_GROUPS + i]) for i, r in enumerate((l0_ref, l1_ref, l2_ref))]
    top = jnp.maximum(jnp.maximum(ms[0], ms[1]), ms[2])
    es = [jnp.exp(m - top) for m in ms]
    den = es[0] * ls[0] + es[1] * ls[1] + es[2] * ls[2]
    head = lax.broadcasted_iota(I32, (LANES, width), 0)
    lane = lax.broadcasted_iota(I32, (LANES, width), 1)
    spread = (lane // HEAD_DIM == head).astype(BF16)
    mixed = jnp.zeros(out_ref.shape[:1] + (width,), F32)
    for e, o_ref in zip(es, (o0_ref, o1_ref, o2_ref)):
        hi, lo = _split_bf16(e / den)
        wide = (jnp.dot(hi, spread, preferred_element_type=F32)
                + jnp.dot(lo, spread, preferred_element_type=F32))
        mixed = mixed + wide * token_order(o_ref, o_tok)
    y = jnp.dot(mixed.astype(BF16), wo_ref[...], preferred_element_type=F32)
    out_ref[...] = _layer_norm(ALPHA * x_ref[...] + y, g_ref[...], b_ref[...])


def _attn_out(x, bsz, outs, stats, w_o, g, b):
    t, d = x.shape
    width = w_o.shape[0]
    tm = TM_WIDE
    tiles_per_seq = t // bsz // tm
    const = lambda bi, c: (0, 0)
    row = lambda bi, c: (bi * tiles_per_seq + c, 0)

    def res_spec(a):
        dil, w = a.shape[1], a.shape[3]
        return pl.BlockSpec((None, dil, tm // dil, w), lambda bi, c: (bi, 0, c, 0))

    return pl.pallas_call(
        _attn_out_kernel,
        grid=(bsz, tiles_per_seq),
        in_specs=([pl.BlockSpec((tm, d), row)] + [res_spec(a) for a in outs] + [res_spec(a) for a in stats]
                  + [pl.BlockSpec((width, d), const), pl.BlockSpec((1, d), const), pl.BlockSpec((1, d), const)]),
        out_specs=pl.BlockSpec((tm, d), row),
        out_shape=jax.ShapeDtypeStruct((t, d), F32),
        scratch_shapes=[pltpu.VMEM((width // LANES, tm, LANES), F32), pltpu.VMEM((2 * N_GROUPS, 1, tm, LANES), F32)],
        compiler_params=_params("arbitrary", "arbitrary"),
        name="attn_out",
    )(x, *outs, *stats, w_o, g, b)


def _proj_residue_kernel(x_ref, w_ref, *rest, dils, scale):
    outs, y_ref = rest[:-1], rest[-1]
    _, n_slabs, tm, _ = y_ref.shape
    width = n_slabs * LANES
    xb = x_ref[...].astype(BF16)
    for c, (o_ref, dil) in enumerate(zip(outs, dils)):
        y = jnp.dot(xb, w_ref[:, c * width:(c + 1) * width], preferred_element_type=F32)
        if scale != 1.0:
            y = y * scale
        if dil == 1:
            o_ref[0] = y.astype(o_ref.dtype)
        elif dil % SUBLANES == 0:
            o_ref[...] = jnp.swapaxes(y.reshape(tm // dil, dil, width), 0, 1).astype(o_ref.dtype)
        else:
            for s in range(n_slabs):
                y_ref[c, s] = y[:, s * LANES:(s + 1) * LANES]
            for r in range(dil):
                rows = [y_ref[c, s, pl.ds(r, tm // dil, stride=dil), :] for s in range(n_slabs)]
                o_ref[r] = jnp.concatenate(rows, axis=1).astype(o_ref.dtype)


def _proj_residue(x, bsz, w, dils, scale=1.0):
    t, d = x.shape
    seq = t // bsz
    width = w.shape[1] // len(dils)
    tm = TM_WIDE
    tiles_per_seq = seq // tm
    return pl.pallas_call(
        functools.partial(_proj_residue_kernel, dils=dils, scale=scale),
        grid=(bsz, tiles_per_seq),
        in_specs=[pl.BlockSpec((tm, d), lambda bi, c: (bi * tiles_per_seq + c, 0)),
                  pl.BlockSpec(w.shape, lambda bi, c: (0, 0))],
        out_specs=[pl.BlockSpec((None, dil, tm // dil, width), lambda bi, c: (bi, 0, c, 0)) for dil in dils],
        out_shape=[jax.ShapeDtypeStruct((bsz, dil, seq // dil, width), BF16) for dil in dils],
        scratch_shapes=[pltpu.VMEM((len(dils), width // LANES, tm, LANES), F32)],
        compiler_params=_params("arbitrary", "arbitrary"),
        name="proj_residue",
    )(x, w)


def _attention_layer(x, bsz, w_q, w_o, kv_groups, g, b):
    dils = tuple(dil for _, dil in ATT_GROUPS)
    qs = _proj_residue(x, bsz, w_q, dils, scale=HEAD_DIM ** -0.5)
    outs, maxes, dens = [], [], []
    for q, (k_res, v_res) in zip(qs, kv_groups):
        o, m, l = _group_attention(q, k_res, v_res)
        outs.append(o)
        maxes.append(m)
        dens.append(l)
    return _attn_out(x, bsz, outs, maxes + dens, w_o, g, b)


def kernel(x, a_w_in, a_conv, a_w_out, kv_w, b_w_q, b_w_o, ffn_w_gate, ffn_w_up, ffn_w_down,
           moe_w_router, moe_w_gate, moe_w_up, moe_w_down, ln_g, ln_b):
    bsz, seq, d = x.shape
    h = x.reshape(bsz * seq, d)
    ln_g = ln_g.reshape(DEPTH, 2, 1, d)
    ln_b = ln_b.reshape(DEPTH, 2, 1, d)
    kv_dils = tuple(dil for _, dil in ATT_GROUPS for _ in range(2))
    kv_groups = None
    for l in range(DEPTH):
        if l < N_A_LAYERS:
            h = _a_mixer(h, a_w_in[l].astype(BF16), a_conv[l], a_w_out[l].astype(BF16), ln_g[l, 0], ln_b[l, 0], seq)
        else:
            if kv_groups is None:
                kv = _proj_residue(h, bsz, kv_w.astype(BF16), kv_dils)
                kv_groups = [(kv[2 * gi], kv[2 * gi + 1]) for gi in range(N_GROUPS)]
            j = l - N_A_LAYERS
            h = _attention_layer(h, bsz, b_w_q[j].astype(BF16), b_w_o[j].astype(BF16), kv_groups,
                                 ln_g[l, 0], ln_b[l, 0])
        i = l // 2
        if l % 2 == 0:
            w_gate, w_up, w_down = (w[i].astype(BF16) for w in (ffn_w_gate, ffn_w_up, ffn_w_down))
            h = _dense_ffn(h, w_gate, w_up, w_down, ln_g[l, 1], ln_b[l, 1])
        else:
            h = _moe_layer(h, moe_w_router[i], moe_w_gate.astype(BF16), moe_w_up.astype(BF16),
                           moe_w_down.astype(BF16), i, ln_g[l, 1], ln_b[l, 1])
    return h.reshape(bsz, seq, d)
```

```python
import functools

import jax
import jax.numpy as jnp
from jax import lax
from jax.experimental import pallas as pl
from jax.experimental.pallas import tpu as pltpu
from jax.experimental.pallas import tpu_sc as plsc

F32 = jnp.float32
U32 = jnp.uint32
BF16 = jnp.bfloat16
I32 = jnp.int32

DEPTH = 4
N_A_LAYERS = DEPTH // 2
CONV_WIDTH = 3
ATT_GROUPS = ((128, 1), (512, 4), (2048, 16))
N_GROUPS = len(ATT_GROUPS)
HEAD_DIM = 64
N_EXPERTS = 8
TOP_K = 2
ALPHA = (2.0 * DEPTH) ** 0.25
LN_EPS = 1e-5

QBLK = 128
assert all(w // d == QBLK for w, d in ATT_GROUPS)

LANES = 128
SUBLANES = 8
VMEM_LIMIT = 56 * 1024 * 1024
NEG_BIG = -1e30

TM_FFN = 512
TM_EXP = 512
TM_WIDE = 1024

_NT = (((1,), (1,)), ((), ()))


def _layer_norm(z, g, b):
    mu = jnp.mean(z, axis=-1, keepdims=True)
    zc = z - mu
    var = jnp.mean(zc * zc, axis=-1, keepdims=True)
    return zc * lax.rsqrt(var + LN_EPS) * g + b


def _params(*semantics):
    return pltpu.CompilerParams(dimension_semantics=semantics, vmem_limit_bytes=VMEM_LIMIT)


def _a_mixer_kernel(x_ref, win_ref, conv_ref, wout_ref, g_ref, b_ref, o_ref, ubuf, *, tiles_per_seq):
    tm, d = x_ref.shape
    piece = tm // A_MIXER_PIECES

    @pl.when(pl.program_id(0) % tiles_per_seq == 0)
    def _():
        ubuf[0:SUBLANES, :] = jnp.zeros((SUBLANES, d), F32)

    cw = conv_ref[...]
    for h in range(A_MIXER_PIECES):
        rows = slice(h * piece, (h + 1) * piece)
        first = SUBLANES + h * piece
        x = x_ref[rows, :]
        p = jnp.dot(x.astype(BF16), win_ref[...], preferred_element_type=F32)
        ubuf[first:first + piece, :] = p[:, d:2 * d] * p[:, 2 * d:]
        conv = (cw[2:3, :] * ubuf[first:first + piece, :]
                + cw[1:2, :] * ubuf[first - 1:first - 1 + piece, :]
                + cw[0:1, :] * ubuf[first - 2:first - 2 + piece, :])
        y = jnp.dot((p[:, :d] * conv).astype(BF16), wout_ref[...], preferred_element_type=F32)
        o_ref[rows, :] = _layer_norm(ALPHA * x + y, g_ref[...], b_ref[...])
    ubuf[0:SUBLANES, :] = ubuf[tm:tm + SUBLANES, :]


A_MIXER_PIECES = 2


def _a_mixer(x, w_in, conv_w, w_out, g, b, seq_len):
    t, d = x.shape
    tm = TM_WIDE
    const = lambda i: (0, 0)
    resident = dict(pipeline_mode=pl.Buffered(1))
    return pl.pallas_call(
        functools.partial(_a_mixer_kernel, tiles_per_seq=seq_len // tm),
        grid=(t // tm,),
        in_specs=[
            pl.BlockSpec((tm, d), lambda i: (i, 0)),
            pl.BlockSpec((d, 3 * d), const, **resident),
            pl.BlockSpec((CONV_WIDTH, d), const),
            pl.BlockSpec((d, d), const, **resident),
            pl.BlockSpec((1, d), const),
            pl.BlockSpec((1, d), const),
        ],
        out_specs=pl.BlockSpec((tm, d), lambda i: (i, 0)),
        out_shape=jax.ShapeDtypeStruct((t, d), F32),
        scratch_shapes=[pltpu.VMEM((tm + SUBLANES, d), F32)],
        compiler_params=_params("arbitrary"),
        name="a_mixer",
    )(x, w_in, conv_w, w_out, g, b)


def _swiglu(xb, wg_ref, wu_ref, wd_ref):
    gate = jnp.dot(xb, wg_ref[...], preferred_element_type=F32)
    up = jnp.dot(xb, wu_ref[...], preferred_element_type=F32)
    h = (gate * jax.nn.sigmoid(gate) * up).astype(BF16)
    return jnp.dot(h, wd_ref[...], preferred_element_type=F32)


def _dense_ffn_kernel(x_ref, wg_ref, wu_ref, wd_ref, g_ref, b_ref, o_ref):
    x = x_ref[...]
    y = _swiglu(x.astype(BF16), wg_ref, wu_ref, wd_ref)
    o_ref[...] = _layer_norm(ALPHA * x + y, g_ref[...], b_ref[...])


def _dense_ffn(x, w_gate, w_up, w_down, g, b):
    t, d = x.shape
    d_ff = w_gate.shape[1]
    tm = TM_FFN
    const = lambda i: (0, 0)
    resident = dict(pipeline_mode=pl.Buffered(1))
    return pl.pallas_call(
        _dense_ffn_kernel,
        grid=(t // tm,),
        in_specs=[
            pl.BlockSpec((tm, d), lambda i: (i, 0)),
            pl.BlockSpec((d, d_ff), const, **resident),
            pl.BlockSpec((d, d_ff), const, **resident),
            pl.BlockSpec((d_ff, d), const, **resident),
            pl.BlockSpec((1, d), const),
            pl.BlockSpec((1, d), const),
        ],
        out_specs=pl.BlockSpec((tm, d), lambda i: (i, 0)),
        out_shape=jax.ShapeDtypeStruct((t, d), F32),
        compiler_params=_params("arbitrary"),
        name="dense_ffn",
    )(x, w_gate, w_up, w_down, g, b)


EXPERT_COL_SPLITS = 2


def _expert_ffn_kernel(te_ref, nv_ref, x_ref, wg_ref, wu_ref, wd_ref, o_ref):
    del te_ref
    valid = pl.program_id(0) < nv_ref[0]

    @pl.when(valid)
    def _():
        xb = _unpack_bf16_pairs(x_ref[...]).astype(BF16)
        piece = wg_ref.shape[1] // EXPERT_COL_SPLITS
        y = None
        for c in range(EXPERT_COL_SPLITS):
            cols = slice(c * piece, (c + 1) * piece)
            gate = jnp.dot(xb, wg_ref[:, cols], preferred_element_type=F32)
            up = jnp.dot(xb, wu_ref[:, cols], preferred_element_type=F32)
            h = (gate * jax.nn.sigmoid(gate) * up).astype(BF16)
            part = jnp.dot(h, wd_ref[cols, :], preferred_element_type=F32)
            y = part if y is None else y + part
        o_ref[...] = _pack_bf16_pairs(y)

    @pl.when(jnp.logical_not(valid))
    def _():
        o_ref[...] = jnp.zeros(o_ref.shape, U32)


def _expert_ffn(xs, tile_expert, n_valid, w_gate, w_up, w_down, layer):
    d = w_gate.shape[2]
    d_exp = w_gate.shape[3]
    tm = TM_EXP
    n_rows = xs.shape[0] // tm * tm
    weights_of_tile = lambda i, te, nv: (layer, te[i], 0, 0)
    resident = dict(pipeline_mode=pl.Buffered(1))
    return pl.pallas_call(
        _expert_ffn_kernel,
        grid_spec=pltpu.PrefetchScalarGridSpec(
            num_scalar_prefetch=2,
            grid=(n_rows // tm,),
            in_specs=[
                pl.BlockSpec((tm, d // 2), lambda i, te, nv: (i, 0)),
                pl.BlockSpec((None, None, d, d_exp), weights_of_tile, **resident),
                pl.BlockSpec((None, None, d, d_exp), weights_of_tile, **resident),
                pl.BlockSpec((None, None, d_exp, d), weights_of_tile, **resident),
            ],
            out_specs=pl.BlockSpec((tm, d // 2), lambda i, te, nv: (i, 0)),
        ),
        out_shape=jax.ShapeDtypeStruct((n_rows, d // 2), U32),
        compiler_params=_params("arbitrary"),
        name="expert_ffn",
    )(tile_expert, n_valid, xs, w_gate, w_up, w_down)


def _split_bf16(v):
    hi = v.astype(BF16)
    lo = (v - hi.astype(F32)).astype(BF16)
    return hi, lo


def _pack_bf16_pairs(v):
    w = v.shape[1] // 2
    lo = lax.bitcast_convert_type(v[:, :w].astype(BF16).astype(F32), U32)
    hi = lax.bitcast_convert_type(v[:, w:].astype(BF16).astype(F32), U32)
    return (lo >> 16) | hi


def _unpack_bf16_pairs(p):
    lo = lax.bitcast_convert_type(p << 16, F32)
    hi = lax.bitcast_convert_type(p & jnp.uint32(0xFFFF0000), F32)
    return jnp.concatenate([lo, hi], axis=1)


def _router_kernel(x_ref, wr_ref, idx_ref, gate_ref, xp_ref, rank_ref, count_ref, run_ref):
    xp_ref[...] = _pack_bf16_pairs(x_ref[...])
    xh, xl = _split_bf16(x_ref[...])
    wh, wl = _split_bf16(wr_ref[...])
    logits = (lax.dot_general(wh, xh, _NT, preferred_element_type=F32)
              + lax.dot_general(wh, xl, _NT, preferred_element_type=F32)
              + lax.dot_general(wl, xh, _NT, preferred_element_type=F32))
    e = lax.broadcasted_iota(I32, logits.shape, 0)
    m1 = jnp.max(logits, axis=0, keepdims=True)
    i1 = jnp.min(jnp.where(logits == m1, e, N_EXPERTS), axis=0, keepdims=True)
    rest = jnp.where(e == i1, -jnp.inf, logits)
    m2 = jnp.max(rest, axis=0, keepdims=True)
    i2 = jnp.min(jnp.where(rest == m2, e, N_EXPERTS), axis=0, keepdims=True)
    r = jnp.exp(m2 - m1)
    idx_ref[...] = jnp.concatenate([i1, i2], axis=0)
    gate_ref[...] = jnp.concatenate([1.0 / (1.0 + r), r / (1.0 + r)], axis=0)

    @pl.when(pl.program_id(0) == 0)
    def _():
        run_ref[...] = jnp.zeros(run_ref.shape, F32)

    tm = logits.shape[1]
    uses = jnp.where(jnp.logical_or(e == i1, e == i2), 1.0, 0.0)
    earlier = lax.broadcasted_iota(I32, (tm, tm), 0) < lax.broadcasted_iota(I32, (tm, tm), 1)
    prefix = jnp.dot(uses.astype(BF16), jnp.where(earlier, 1.0, 0.0).astype(BF16), preferred_element_type=F32)
    prefix = prefix + run_ref[:, 0:1]
    ranks = [jnp.sum(jnp.where(e == i, prefix, 0.0), axis=0, keepdims=True) for i in (i1, i2)]
    rank_ref[...] = jnp.concatenate(ranks, axis=0).astype(I32)
    run_ref[...] = run_ref[...] + jnp.sum(uses, axis=1, keepdims=True)
    count_ref[...] = run_ref[...]


def _router(x, w_router_t):
    t, d = x.shape
    tm = TM_WIDE
    return pl.pallas_call(
        _router_kernel,
        grid=(t // tm,),
        in_specs=[
            pl.BlockSpec((tm, d), lambda i: (i, 0)),
            pl.BlockSpec((N_EXPERTS, d), lambda i: (0, 0)),
        ],
        out_specs=[
            pl.BlockSpec((TOP_K, tm), lambda i: (0, i)),
            pl.BlockSpec((TOP_K, tm), lambda i: (0, i)),
            pl.BlockSpec((tm, d // 2), lambda i: (i, 0)),
            pl.BlockSpec((TOP_K, tm), lambda i: (0, i)),
            pl.BlockSpec((N_EXPERTS, LANES), lambda i: (0, 0)),
        ],
        out_shape=[jax.ShapeDtypeStruct((TOP_K, t), I32), jax.ShapeDtypeStruct((TOP_K, t), F32),
                   jax.ShapeDtypeStruct((t, d // 2), U32), jax.ShapeDtypeStruct((TOP_K, t), I32),
                   jax.ShapeDtypeStruct((N_EXPERTS, LANES), F32)],
        scratch_shapes=[pltpu.VMEM((N_EXPERTS, LANES), F32)],
        compiler_params=_params("arbitrary"),
        name="router",
    )(x, w_router_t)


def _moe_route(x, w_router):
    t, d = x.shape
    idx, gates, x_packed, rank, count = _router(x, w_router.T)

    counts = count[:, 0].astype(I32)
    padded = ((counts + TM_EXP - 1) // TM_EXP) * TM_EXP
    ends = jnp.cumsum(padded)
    experts = jnp.arange(N_EXPERTS, dtype=I32)[:, None, None]
    pos = rank + jnp.sum(jnp.where(idx[None] == experts, (ends - padded)[:, None, None], 0), axis=0)
    n_tiles = TOP_K * t // TM_EXP + N_EXPERTS
    tile_ids = jnp.arange(n_tiles, dtype=I32)
    tile_expert = jnp.minimum(
        jnp.sum((tile_ids[:, None] >= (ends // TM_EXP)[None, :]).astype(I32), axis=1), N_EXPERTS - 1).astype(I32)
    n_valid = (ends[-1:] // TM_EXP).astype(I32)

    n_rows = n_tiles * TM_EXP
    return dict(x_packed=x_packed, pos0=pos[0], pos1=pos[1], gates=gates.T, tile_expert=tile_expert, n_valid=n_valid,
                unused=_unused_rows(counts, padded, ends, n_rows), n_rows=n_rows)


def _moe_layer(x, w_router, w_gate, w_up, w_down, layer, g, b):
    r = _moe_route(x, w_router)
    xs = _sc_dispatch(r["x_packed"], r["pos0"], r["pos1"], r["unused"], r["n_rows"])
    ys = _expert_ffn(xs, r["tile_expert"], r["n_valid"], w_gate, w_up, w_down, layer)
    y0, y1 = _sc_gather2(ys, r["pos0"], r["pos1"])
    return _moe_sum(x, y0, y1, r["gates"], g, b)


SC_CORES = 2
SC_SUBCORES = 16
SC_ROWS = 128


def _sc_worker_base(per_worker):
    return (lax.axis_index("subcore") * SC_CORES + lax.axis_index("core")) * per_worker


def _unused_rows(counts, padded, ends, n_rows):
    lane = jnp.arange(SC_ROWS, dtype=I32)
    chunk0 = jnp.arange(TM_EXP // SC_ROWS, dtype=I32) * SC_ROWS
    spare = n_rows + lane
    pad = (ends - padded + counts)[:, None, None] + chunk0[None, :, None] + lane[None, None, :]
    pad = jnp.where(pad < ends[:, None, None], pad, spare[None, None, :])
    tail = ends[-1] + jnp.arange(N_EXPERTS * TM_EXP // SC_ROWS, dtype=I32)[:, None] * SC_ROWS + lane[None, :]
    tail = jnp.where(tail < n_rows, tail, spare[None, :])
    return jnp.concatenate([pad.reshape(-1), tail.reshape(-1)]).astype(I32)


def _sc_dispatch(x, pos0, pos1, unused, n_rows):
    t, d = x.shape
    workers = SC_CORES * SC_SUBCORES
    per_worker = t // workers
    zero_chunks = unused.shape[0] // SC_ROWS // workers
    mesh = plsc.VectorSubcoreMesh(core_axis_name="core", subcore_axis_name="subcore")

    @pl.kernel(out_type=jax.ShapeDtypeStruct((n_rows + SC_ROWS, d), x.dtype), mesh=mesh,
               scratch_types=[pltpu.VMEM((SC_ROWS,), I32), pltpu.VMEM((SC_ROWS,), I32),
                              pltpu.VMEM((SC_ROWS, d), x.dtype)])
    def run(x_hbm, p0_hbm, p1_hbm, unused_hbm, zeros_hbm, o_hbm, i0_v, i1_v, rows_v):
        pltpu.sync_copy(zeros_hbm, rows_v)
        zbase = _sc_worker_base(zero_chunks * SC_ROWS)

        @pl.loop(0, zero_chunks)
        def _(c):
            pltpu.sync_copy(unused_hbm.at[pl.ds(zbase + c * SC_ROWS, SC_ROWS)], i0_v)
            pltpu.sync_copy(rows_v, o_hbm.at[i0_v])

        base = _sc_worker_base(per_worker)

        @pl.loop(0, per_worker // SC_ROWS)
        def _(c):
            off = base + c * SC_ROWS
            pltpu.sync_copy(p0_hbm.at[pl.ds(off, SC_ROWS)], i0_v)
            pltpu.sync_copy(p1_hbm.at[pl.ds(off, SC_ROWS)], i1_v)
            pltpu.sync_copy(x_hbm.at[pl.ds(off, SC_ROWS)], rows_v)
            pltpu.sync_copy(rows_v, o_hbm.at[i0_v])
            pltpu.sync_copy(rows_v, o_hbm.at[i1_v])

    return run(x, pos0, pos1, unused, jnp.zeros((SC_ROWS, d), x.dtype))


def _sc_gather2(ys, pos0, pos1):
    t = pos0.shape[0]
    d = ys.shape[1]
    per_worker = t // (SC_CORES * SC_SUBCORES)
    mesh = plsc.VectorSubcoreMesh(core_axis_name="core", subcore_axis_name="subcore")
    out = jax.ShapeDtypeStruct((t, d), ys.dtype)

    @pl.kernel(out_type=(out, out), mesh=mesh,
               scratch_types=[pltpu.VMEM((SC_ROWS,), I32), pltpu.VMEM((SC_ROWS, d), ys.dtype)])
    def run(y_hbm, p0_hbm, p1_hbm, o0_hbm, o1_hbm, i_v, rows_v):
        base = _sc_worker_base(per_worker)

        @pl.loop(0, per_worker // SC_ROWS)
        def _(c):
            off = base + c * SC_ROWS
            for p_hbm, o_hbm in ((p0_hbm, o0_hbm), (p1_hbm, o1_hbm)):
                pltpu.sync_copy(p_hbm.at[pl.ds(off, SC_ROWS)], i_v)
                pltpu.sync_copy(y_hbm.at[i_v], rows_v)
                pltpu.sync_copy(rows_v, o_hbm.at[pl.ds(off, SC_ROWS)])

    return run(ys, pos0, pos1)


def _moe_sum_kernel(x_ref, y0_ref, y1_ref, gate_ref, g_ref, b_ref, o_ref):
    gates = gate_ref[...]
    ff = gates[:, 0:1] * _unpack_bf16_pairs(y0_ref[...]) + gates[:, 1:2] * _unpack_bf16_pairs(y1_ref[...])
    o_ref[...] = _layer_norm(ALPHA * x_ref[...] + ff, g_ref[...], b_ref[...])


def _moe_sum(x, y0, y1, gates, g, b):
    t, d = x.shape
    tm = TM_WIDE
    const = lambda i: (0, 0)
    row = lambda i: (i, 0)
    return pl.pallas_call(
        _moe_sum_kernel,
        grid=(t // tm,),
        in_specs=[pl.BlockSpec((tm, d), row), pl.BlockSpec((tm, d // 2), row), pl.BlockSpec((tm, d // 2), row),
                  pl.BlockSpec((tm, TOP_K), row), pl.BlockSpec((1, d), const), pl.BlockSpec((1, d), const)],
        out_specs=pl.BlockSpec((tm, d), row),
        out_shape=jax.ShapeDtypeStruct((t, d), F32),
        compiler_params=_params("arbitrary"),
        name="moe_sum",
    )(x, y0, y1, gates, g, b)


MAX_QBLKS_PER_STEP = 8


def _attn_kernel(q_ref, kp_ref, kc_ref, vp_ref, vc_ref, o_ref, m_ref, l_ref, kcat_ref, vcat_ref):
    n_pairs = q_ref.shape[-1] // LANES
    qi = lax.broadcasted_iota(I32, (QBLK, 2 * QBLK), 0)
    ki = lax.broadcasted_iota(I32, (QBLK, 2 * QBLK), 1)
    in_band = ki - QBLK <= qi
    lane = lax.broadcasted_iota(I32, (QBLK, LANES), 1)
    low_half = lane < HEAD_DIM
    for pr in range(n_pairs):
        vcat_ref[:, :, (2 * pr + 1) * LANES:(2 * pr + 2) * LANES] = jnp.ones(vcat_ref.shape[:2] + (LANES,), BF16)
    m_ref[...] = jnp.zeros(m_ref.shape, F32)
    l_ref[...] = jnp.ones(l_ref.shape, F32)

    def block(r, j):
        q_r, o_r, m_r, l_r = q_ref.at[r], o_ref.at[r], m_ref.at[r], l_ref.at[r]
        rows = slice(j * QBLK, (j + 1) * QBLK)
        keys = slice(j * QBLK, (j + 2) * QBLK)
        first_ok = qi if j > 0 else qi + jnp.where(pl.program_id(2) > 0, 0, QBLK)
        mask = jnp.logical_and(jnp.logical_or(ki >= first_ok, ki >= QBLK), in_band)
        for pr in range(n_pairs):
            cols = slice(pr * LANES, (pr + 1) * LANES)
            q2 = q_r[rows, cols]
            k2 = kcat_ref[r, keys, cols]
            v_ext = vcat_ref[r, keys, 2 * pr * LANES:(2 * pr + 2) * LANES]
            o_pair = None
            for a in range(2):
                own = low_half if a == 0 else jnp.logical_not(low_half)
                qm = jnp.where(own, q2, jnp.zeros_like(q2))
                s = lax.dot_general(qm, k2, _NT, preferred_element_type=F32)
                s = jnp.where(mask, s, NEG_BIG)
                m = jnp.max(s, axis=-1, keepdims=True)
                p = jnp.exp(s - m).astype(BF16)
                oe = jnp.dot(p, v_ext, preferred_element_type=F32)
                o_pair = oe[:, :LANES] if a == 0 else jnp.where(low_half, o_pair, oe[:, :LANES])
                h = 2 * pr + a
                m_r[rows, h:h + 1] = m
                l_r[rows, h:h + 1] = oe[:, LANES + h:LANES + h + 1]
            o_r[rows, cols] = o_pair.astype(o_r.dtype)

    for r in range(q_ref.shape[0]):
        kcat_ref[r, 0:QBLK, :] = kp_ref[r]
        kcat_ref[r, QBLK:, :] = kc_ref[r]
        for pr in range(n_pairs):
            vcat_ref[r, 0:QBLK, 2 * pr * LANES:(2 * pr + 1) * LANES] = vp_ref[r, :, pr * LANES:(pr + 1) * LANES]
            vcat_ref[r, QBLK:, 2 * pr * LANES:(2 * pr + 1) * LANES] = vc_ref[r, :, pr * LANES:(pr + 1) * LANES]
    for r in range(q_ref.shape[0]):
        for j in range(q_ref.shape[1] // QBLK):
            block(r, j)


def _group_attention(q, k, v):
    bsz, dil, length, width = q.shape
    blocks_per_step = min(MAX_QBLKS_PER_STEP, length // QBLK)
    classes_per_step = min(dil, MAX_QBLKS_PER_STEP // blocks_per_step)
    step = blocks_per_step * QBLK
    blk = (None, classes_per_step, step, width)
    cur = lambda b, r, n: (b, r, n, 0)
    prev_blk = (None, classes_per_step, QBLK, width)
    prev = lambda b, r, n: (b, r, jnp.maximum(blocks_per_step * n - 1, 0), 0)
    stat_spec = pl.BlockSpec((None, classes_per_step, step, LANES), cur)
    stat_shape = jax.ShapeDtypeStruct((bsz, dil, length, LANES), F32)
    return pl.pallas_call(
        _attn_kernel,
        grid=(bsz, dil // classes_per_step, length // step),
        in_specs=[
            pl.BlockSpec(blk, cur),
            pl.BlockSpec(prev_blk, prev),
            pl.BlockSpec(blk, cur),
            pl.BlockSpec(prev_blk, prev),
            pl.BlockSpec(blk, cur),
        ],
        out_specs=[pl.BlockSpec(blk, cur), stat_spec, stat_spec],
        out_shape=[jax.ShapeDtypeStruct((bsz, dil, length, width), BF16), stat_shape, stat_shape],
        scratch_shapes=[pltpu.VMEM((classes_per_step, step + QBLK, width), BF16),
                        pltpu.VMEM((classes_per_step, step + QBLK, 2 * width), BF16)],
        compiler_params=_params("arbitrary", "arbitrary", "arbitrary"),
        name="group_attention",
    )(q, k, k, v, v)


def _to_token_order(src_ref, dst_ref):
    dil, n, w = src_ref.shape
    for r in range(dil):
        for c in range(w // LANES):
            dst_ref[c, pl.ds(r, n, stride=dil), :] = src_ref[r, :, c * LANES:(c + 1) * LANES].astype(dst_ref.dtype)
    return jnp.concatenate([dst_ref[c] for c in range(w // LANES)], axis=1)


def _attn_out_kernel(x_ref, o0_ref, o1_ref, o2_ref, m0_ref, m1_ref, m2_ref, l0_ref, l1_ref, l2_ref,
                     wo_ref, g_ref, b_ref, out_ref, o_tok, stat_tok):
    width = o0_ref.shape[-1]

    def token_order(ref, scratch):
        dil, n, w = ref.shape
        if dil == 1:
            return ref[0].astype(F32)
        if dil % SUBLANES == 0:
            return jnp.swapaxes(ref[...].astype(F32), 0, 1).reshape(dil * n, w)
        return _to_token_order(ref, scratch)

    ms = [token_order(r, stat_tok.at[i]) for i, r in enumerate((m0_ref, m1_ref, m2_ref))]
    ls = [token_order(r, stat_tok.at[N_GROUPS + i]) for i, r in enumerate((l0_ref, l1_ref, l2_ref))]
    top = jnp.maximum(jnp.maximum(ms[0], ms[1]), ms[2])
    es = [jnp.exp(m - top) for m in ms]
    den = es[0] * ls[0] + es[1] * ls[1] + es[2] * ls[2]
    head = lax.broadcasted_iota(I32, (LANES, width), 0)
    lane = lax.broadcasted_iota(I32, (LANES, width), 1)
    spread = (lane // HEAD_DIM == head).astype(BF16)
    mixed = jnp.zeros(out_ref.shape[:1] + (width,), F32)
    for e, o_ref in zip(es, (o0_ref, o1_ref, o2_ref)):
        hi, lo = _split_bf16(e / den)
        wide = (jnp.dot(hi, spread, preferred_element_type=F32)
                + jnp.dot(lo, spread, preferred_element_type=F32))
        mixed = mixed + wide * token_order(o_ref, o_tok)
    y = jnp.dot(mixed.astype(BF16), wo_ref[...], preferred_element_type=F32)
    out_ref[...] = _layer_norm(ALPHA * x_ref[...] + y, g_ref[...], b_ref[...])


def _attn_out(x, bsz, outs, stats, w_o, g, b):
    t, d = x.shape
    width = w_o.shape[0]
    tm = TM_WIDE
    tiles_per_seq = t // bsz // tm
    const = lambda bi, c: (0, 0)
    row = lambda bi, c: (bi * tiles_per_seq + c, 0)

    def res_spec(a):
        dil, w = a.shape[1], a.shape[3]
        return pl.BlockSpec((None, dil, tm // dil, w), lambda bi, c: (bi, 0, c, 0))

    return pl.pallas_call(
        _attn_out_kernel,
        grid=(bsz, tiles_per_seq),
        in_specs=([pl.BlockSpec((tm, d), row)] + [res_spec(a) for a in outs] + [res_spec(a) for a in stats]
                  + [pl.BlockSpec((width, d), const), pl.BlockSpec((1, d), const), pl.BlockSpec((1, d), const)]),
        out_specs=pl.BlockSpec((tm, d), row),
        out_shape=jax.ShapeDtypeStruct((t, d), F32),
        scratch_shapes=[pltpu.VMEM((width // LANES, tm, LANES), F32), pltpu.VMEM((2 * N_GROUPS, 1, tm, LANES), F32)],
        compiler_params=_params("arbitrary", "arbitrary"),
        name="attn_out",
    )(x, *outs, *stats, w_o, g, b)


def _proj_residue_kernel(x_ref, w_ref, *rest, dils, scale):
    outs, y_ref = rest[:-1], rest[-1]
    _, n_slabs, tm, _ = y_ref.shape
    width = n_slabs * LANES
    xb = x_ref[...].astype(BF16)
    for c, (o_ref, dil) in enumerate(zip(outs, dils)):
        y = jnp.dot(xb, w_ref[:, c * width:(c + 1) * width], preferred_element_type=F32)
        if scale != 1.0:
            y = y * scale
        if dil == 1:
            o_ref[0] = y.astype(o_ref.dtype)
        elif dil % SUBLANES == 0:
            o_ref[...] = jnp.swapaxes(y.reshape(tm // dil, dil, width), 0, 1).astype(o_ref.dtype)
        else:
            for s in range(n_slabs):
                y_ref[c, s] = y[:, s * LANES:(s + 1) * LANES]
            for r in range(dil):
                rows = [y_ref[c, s, pl.ds(r, tm // dil, stride=dil), :] for s in range(n_slabs)]
                o_ref[r] = jnp.concatenate(rows, axis=1).astype(o_ref.dtype)


def _proj_residue(x, bsz, w, dils, scale=1.0):
    t, d = x.shape
    seq = t // bsz
    width = w.shape[1] // len(dils)
    tm = TM_WIDE
    tiles_per_seq = seq // tm
    return pl.pallas_call(
        functools.partial(_proj_residue_kernel, dils=dils, scale=scale),
        grid=(bsz, tiles_per_seq),
        in_specs=[pl.BlockSpec((tm, d), lambda bi, c: (bi * tiles_per_seq + c, 0)),
                  pl.BlockSpec(w.shape, lambda bi, c: (0, 0))],
        out_specs=[pl.BlockSpec((None, dil, tm // dil, width), lambda bi, c: (bi, 0, c, 0)) for dil in dils],
        out_shape=[jax.ShapeDtypeStruct((bsz, dil, seq // dil, width), BF16) for dil in dils],
        scratch_shapes=[pltpu.VMEM((len(dils), width // LANES, tm, LANES), F32)],
        compiler_params=_params("arbitrary", "arbitrary"),
        name="proj_residue",
    )(x, w)


def _attention_layer(x, bsz, w_q, w_o, kv_groups, g, b):
    dils = tuple(dil for _, dil in ATT_GROUPS)
    qs = _proj_residue(x, bsz, w_q, dils, scale=HEAD_DIM ** -0.5)
    outs, maxes, dens = [], [], []
    for q, (k_res, v_res) in zip(qs, kv_groups):
        o, m, l = _group_attention(q, k_res, v_res)
        outs.append(o)
        maxes.append(m)
        dens.append(l)
    return _attn_out(x, bsz, outs, maxes + dens, w_o, g, b)


def kernel(x, a_w_in, a_conv, a_w_out, kv_w, b_w_q, b_w_o, ffn_w_gate, ffn_w_up, ffn_w_down,
           moe_w_router, moe_w_gate, moe_w_up, moe_w_down, ln_g, ln_b):
    bsz, seq, d = x.shape
    h = x.reshape(bsz * seq, d)
    ln_g = ln_g.reshape(DEPTH, 2, 1, d)
    ln_b = ln_b.reshape(DEPTH, 2, 1, d)
    kv_dils = tuple(dil for _, dil in ATT_GROUPS for _ in range(2))
    kv_groups = None
    for l in range(DEPTH):
        if l < N_A_LAYERS:
            h = _a_mixer(h, a_w_in[l].astype(BF16), a_conv[l], a_w_out[l].astype(BF16), ln_g[l, 0], ln_b[l, 0], seq)
        else:
            if kv_groups is None:
                kv = _proj_residue(h, bsz, kv_w.astype(BF16), kv_dils)
                kv_groups = [(kv[2 * gi], kv[2 * gi + 1]) for gi in range(N_GROUPS)]
            j = l - N_A_LAYERS
            h = _attention_layer(h, bsz, b_w_q[j].astype(BF16), b_w_o[j].astype(BF16), kv_groups,
                                 ln_g[l, 0], ln_b[l, 0])
        i = l // 2
        if l % 2 == 0:
            w_gate, w_up, w_down = (w[i].astype(BF16) for w in (ffn_w_gate, ffn_w_up, ffn_w_down))
            h = _dense_ffn(h, w_gate, w_up, w_down, ln_g[l, 1], ln_b[l, 1])
        else:
            h = _moe_layer(h, moe_w_router[i], moe_w_gate.astype(BF16), moe_w_up.astype(BF16),
                           moe_w_down.astype(BF16), i, ln_g[l, 1], ln_b[l, 1])
    return h.reshape(bsz, seq, d)
```

```python
import functools

import jax
import jax.numpy as jnp
from jax import lax
from jax.experimental import pallas as pl
from jax.experimental.pallas import tpu as pltpu
from jax.experimental.pallas import tpu_sc as plsc

F32 = jnp.float32
U32 = jnp.uint32
BF16 = jnp.bfloat16
I32 = jnp.int32

DEPTH = 4
N_A_LAYERS = DEPTH // 2
CONV_WIDTH = 3
ATT_GROUPS = ((128, 1), (512, 4), (2048, 16))
N_GROUPS = len(ATT_GROUPS)
HEAD_DIM = 64
N_EXPERTS = 8
TOP_K = 2
ALPHA = (2.0 * DEPTH) ** 0.25
LN_EPS = 1e-5

QBLK = 128
assert all(w // d == QBLK for w, d in ATT_GROUPS)

LANES = 128
SUBLANES = 8
VMEM_LIMIT = 56 * 1024 * 1024
NEG_BIG = -1e30

TM_FFN = 512
TM_EXP = 512
TM_WIDE = 1024

_NT = (((1,), (1,)), ((), ()))


def _layer_norm(z, g, b):
    mu = jnp.mean(z, axis=-1, keepdims=True)
    zc = z - mu
    var = jnp.mean(zc * zc, axis=-1, keepdims=True)
    return zc * lax.rsqrt(var + LN_EPS) * g + b


def _params(*semantics):
    return pltpu.CompilerParams(dimension_semantics=semantics, vmem_limit_bytes=VMEM_LIMIT)


def _a_mixer_kernel(x_ref, win_ref, conv_ref, wout_ref, g_ref, b_ref, o_ref, ubuf, *, tiles_per_seq):
    tm, d = x_ref.shape
    piece = tm // A_MIXER_PIECES

    @pl.when(pl.program_id(0) % tiles_per_seq == 0)
    def _():
        ubuf[0:SUBLANES, :] = jnp.zeros((SUBLANES, d), F32)

    cw = conv_ref[...]
    for h in range(A_MIXER_PIECES):
        rows = slice(h * piece, (h + 1) * piece)
        first = SUBLANES + h * piece
        x = x_ref[rows, :]
        p = jnp.dot(x.astype(BF16), win_ref[...], preferred_element_type=F32)
        ubuf[first:first + piece, :] = p[:, d:2 * d] * p[:, 2 * d:]
        conv = (cw[2:3, :] * ubuf[first:first + piece, :]
                + cw[1:2, :] * ubuf[first - 1:first - 1 + piece, :]
                + cw[0:1, :] * ubuf[first - 2:first - 2 + piece, :])
        y = jnp.dot((p[:, :d] * conv).astype(BF16), wout_ref[...], preferred_element_type=F32)
        o_ref[rows, :] = _layer_norm(ALPHA * x + y, g_ref[...], b_ref[...])
    ubuf[0:SUBLANES, :] = ubuf[tm:tm + SUBLANES, :]


A_MIXER_PIECES = 2


def _a_mixer(x, w_in, conv_w, w_out, g, b, seq_len):
    t, d = x.shape
    tm = TM_WIDE
    const = lambda i: (0, 0)
    resident = dict(pipeline_mode=pl.Buffered(1))
    return pl.pallas_call(
        functools.partial(_a_mixer_kernel, tiles_per_seq=seq_len // tm),
        grid=(t // tm,),
        in_specs=[
            pl.BlockSpec((tm, d), lambda i: (i, 0)),
            pl.BlockSpec((d, 3 * d), const, **resident),
            pl.BlockSpec((CONV_WIDTH, d), const),
            pl.BlockSpec((d, d), const, **resident),
            pl.BlockSpec((1, d), const),
            pl.BlockSpec((1, d), const),
        ],
        out_specs=pl.BlockSpec((tm, d), lambda i: (i, 0)),
        out_shape=jax.ShapeDtypeStruct((t, d), F32),
        scratch_shapes=[pltpu.VMEM((tm + SUBLANES, d), F32)],
        compiler_params=_params("arbitrary"),
        name="a_mixer",
    )(x, w_in, conv_w, w_out, g, b)


def _swiglu(xb, wg_ref, wu_ref, wd_ref):
    gate = jnp.dot(xb, wg_ref[...], preferred_element_type=F32)
    up = jnp.dot(xb, wu_ref[...], preferred_element_type=F32)
    h = (gate * jax.nn.sigmoid(gate) * up).astype(BF16)
    return jnp.dot(h, wd_ref[...], preferred_element_type=F32)


def _dense_ffn_kernel(x_ref, wg_ref, wu_ref, wd_ref, g_ref, b_ref, o_ref):
    x = x_ref[...]
    y = _swiglu(x.astype(BF16), wg_ref, wu_ref, wd_ref)
    o_ref[...] = _layer_norm(ALPHA * x + y, g_ref[...], b_ref[...])


def _dense_ffn(x, w_gate, w_up, w_down, g, b):
    t, d = x.shape
    d_ff = w_gate.shape[1]
    tm = TM_FFN
    const = lambda i: (0, 0)
    resident = dict(pipeline_mode=pl.Buffered(1))
    return pl.pallas_call(
        _dense_ffn_kernel,
        grid=(t // tm,),
        in_specs=[
            pl.BlockSpec((tm, d), lambda i: (i, 0)),
            pl.BlockSpec((d, d_ff), const, **resident),
            pl.BlockSpec((d, d_ff), const, **resident),
            pl.BlockSpec((d_ff, d), const, **resident),
            pl.BlockSpec((1, d), const),
            pl.BlockSpec((1, d), const),
        ],
        out_specs=pl.BlockSpec((tm, d), lambda i: (i, 0)),
        out_shape=jax.ShapeDtypeStruct((t, d), F32),
        compiler_params=_params("arbitrary"),
        name="dense_ffn",
    )(x, w_gate, w_up, w_down, g, b)


EXPERT_COL_SPLITS = 2


def _expert_ffn_kernel(te_ref, nv_ref, x_ref, wg_ref, wu_ref, wd_ref, o_ref):
    del te_ref
    valid = pl.program_id(0) < nv_ref[0]

    @pl.when(valid)
    def _():
        xb = _unpack_bf16_pairs(x_ref[...]).astype(BF16)
        piece = wg_ref.shape[1] // EXPERT_COL_SPLITS
        y = None
        for c in range(EXPERT_COL_SPLITS):
            cols = slice(c * piece, (c + 1) * piece)
            gate = jnp.dot(xb, wg_ref[:, cols], preferred_element_type=F32)
            up = jnp.dot(xb, wu_ref[:, cols], preferred_element_type=F32)
            h = (gate * jax.nn.sigmoid(gate) * up).astype(BF16)
            part = jnp.dot(h, wd_ref[cols, :], preferred_element_type=F32)
            y = part if y is None else y + part
        o_ref[...] = _pack_bf16_pairs(y)

    @pl.when(jnp.logical_not(valid))
    def _():
        o_ref[...] = jnp.zeros(o_ref.shape, U32)


def _expert_ffn(xs, tile_expert, n_valid, w_gate, w_up, w_down, layer):
    d = w_gate.shape[2]
    d_exp = w_gate.shape[3]
    tm = TM_EXP
    n_rows = xs.shape[0] // tm * tm
    weights_of_tile = lambda i, te, nv: (layer, te[i], 0, 0)
    resident = dict(pipeline_mode=pl.Buffered(1))
    return pl.pallas_call(
        _expert_ffn_kernel,
        grid_spec=pltpu.PrefetchScalarGridSpec(
            num_scalar_prefetch=2,
            grid=(n_rows // tm,),
            in_specs=[
                pl.BlockSpec((tm, d // 2), lambda i, te, nv: (i, 0)),
                pl.BlockSpec((None, None, d, d_exp), weights_of_tile, **resident),
                pl.BlockSpec((None, None, d, d_exp), weights_of_tile, **resident),
                pl.BlockSpec((None, None, d_exp, d), weights_of_tile, **resident),
            ],
            out_specs=pl.BlockSpec((tm, d // 2), lambda i, te, nv: (i, 0)),
        ),
        out_shape=jax.ShapeDtypeStruct((n_rows, d // 2), U32),
        compiler_params=_params("arbitrary"),
        name="expert_ffn",
    )(tile_expert, n_valid, xs, w_gate, w_up, w_down)


def _split_bf16(v):
    hi = v.astype(BF16)
    lo = (v - hi.astype(F32)).astype(BF16)
    return hi, lo


def _pack_bf16_pairs(v):
    w = v.shape[1] // 2
    lo = lax.bitcast_convert_type(v[:, :w].astype(BF16).astype(F32), U32)
    hi = lax.bitcast_convert_type(v[:, w:].astype(BF16).astype(F32), U32)
    return (lo >> 16) | hi


def _unpack_bf16_pairs(p):
    lo = lax.bitcast_convert_type(p << 16, F32)
    hi = lax.bitcast_convert_type(p & jnp.uint32(0xFFFF0000), F32)
    return jnp.concatenate([lo, hi], axis=1)


def _router_kernel(x_ref, wr_ref, idx_ref, gate_ref, xp_ref, rank_ref, count_ref, run_ref):
    xp_ref[...] = _pack_bf16_pairs(x_ref[...])
    xh, xl = _split_bf16(x_ref[...])
    wh, wl = _split_bf16(wr_ref[...])
    logits = (lax.dot_general(wh, xh, _NT, preferred_element_type=F32)
              + lax.dot_general(wh, xl, _NT, preferred_element_type=F32)
              + lax.dot_general(wl, xh, _NT, preferred_element_type=F32))
    e = lax.broadcasted_iota(I32, logits.shape, 0)
    m1 = jnp.max(logits, axis=0, keepdims=True)
    i1 = jnp.min(jnp.where(logits == m1, e, N_EXPERTS), axis=0, keepdims=True)
    rest = jnp.where(e == i1, -jnp.inf, logits)
    m2 = jnp.max(rest, axis=0, keepdims=True)
    i2 = jnp.min(jnp.where(rest == m2, e, N_EXPERTS), axis=0, keepdims=True)
    r = jnp.exp(m2 - m1)
    idx_ref[...] = jnp.concatenate([i1, i2], axis=0)
    gate_ref[...] = jnp.concatenate([1.0 / (1.0 + r), r / (1.0 + r)], axis=0)

    @pl.when(pl.program_id(0) == 0)
    def _():
        run_ref[...] = jnp.zeros(run_ref.shape, F32)

    tm = logits.shape[1]
    uses = jnp.where(jnp.logical_or(e == i1, e == i2), 1.0, 0.0)
    earlier = lax.broadcasted_iota(I32, (tm, tm), 0) < lax.broadcasted_iota(I32, (tm, tm), 1)
    prefix = jnp.dot(uses.astype(BF16), jnp.where(earlier, 1.0, 0.0).astype(BF16), preferred_element_type=F32)
    prefix = prefix + run_ref[:, 0:1]
    ranks = [jnp.sum(jnp.where(e == i, prefix, 0.0), axis=0, keepdims=True) for i in (i1, i2)]
    rank_ref[...] = jnp.concatenate(ranks, axis=0).astype(I32)
    run_ref[...] = run_ref[...] + jnp.sum(uses, axis=1, keepdims=True)
    count_ref[...] = run_ref[...]


def _router(x, w_router_t):
    t, d = x.shape
    tm = TM_WIDE
    return pl.pallas_call(
        _router_kernel,
        grid=(t // tm,),
        in_specs=[
            pl.BlockSpec((tm, d), lambda i: (i, 0)),
            pl.BlockSpec((N_EXPERTS, d), lambda i: (0, 0)),
        ],
        out_specs=[
            pl.BlockSpec((TOP_K, tm), lambda i: (0, i)),
            pl.BlockSpec((TOP_K, tm), lambda i: (0, i)),
            pl.BlockSpec((tm, d // 2), lambda i: (i, 0)),
            pl.BlockSpec((TOP_K, tm), lambda i: (0, i)),
            pl.BlockSpec((N_EXPERTS, LANES), lambda i: (0, 0)),
        ],
        out_shape=[jax.ShapeDtypeStruct((TOP_K, t), I32), jax.ShapeDtypeStruct((TOP_K, t), F32),
                   jax.ShapeDtypeStruct((t, d // 2), U32), jax.ShapeDtypeStruct((TOP_K, t), I32),
                   jax.ShapeDtypeStruct((N_EXPERTS, LANES), F32)],
        scratch_shapes=[pltpu.VMEM((N_EXPERTS, LANES), F32)],
        compiler_params=_params("arbitrary"),
        name="router",
    )(x, w_router_t)


def _moe_route(x, w_router):
    t, d = x.shape
    idx, gates, x_packed, rank, count = _router(x, w_router.T)

    counts = count[:, 0].astype(I32)
    padded = ((counts + TM_EXP - 1) // TM_EXP) * TM_EXP
    ends = jnp.cumsum(padded)
    experts = jnp.arange(N_EXPERTS, dtype=I32)[:, None, None]
    pos = rank + jnp.sum(jnp.where(idx[None] == experts, (ends - padded)[:, None, None], 0), axis=0)
    n_tiles = TOP_K * t // TM_EXP + N_EXPERTS
    tile_ids = jnp.arange(n_tiles, dtype=I32)
    tile_expert = jnp.minimum(
        jnp.sum((tile_ids[:, None] >= (ends // TM_EXP)[None, :]).astype(I32), axis=1), N_EXPERTS - 1).astype(I32)
    n_valid = (ends[-1:] // TM_EXP).astype(I32)

    n_rows = n_tiles * TM_EXP
    return dict(x_packed=x_packed, pos0=pos[0], pos1=pos[1], gates=gates.T, tile_expert=tile_expert, n_valid=n_valid,
                unused=_unused_rows(counts, padded, ends, n_rows), n_rows=n_rows)


def _moe_layer(x, w_router, w_gate, w_up, w_down, layer, g, b):
    r = _moe_route(x, w_router)
    xs = _sc_dispatch(r["x_packed"], r["pos0"], r["pos1"], r["unused"], r["n_rows"])
    ys = _expert_ffn(xs, r["tile_expert"], r["n_valid"], w_gate, w_up, w_down, layer)
    y0, y1 = _sc_gather2(ys, r["pos0"], r["pos1"])
    return _moe_sum(x, y0, y1, r["gates"], g, b)


SC_CORES = 2
SC_SUBCORES = 16
SC_ROWS = 128


def _sc_worker_base(per_worker):
    return (lax.axis_index("subcore") * SC_CORES + lax.axis_index("core")) * per_worker


def _unused_rows(counts, padded, ends, n_rows):
    lane = jnp.arange(SC_ROWS, dtype=I32)
    chunk0 = jnp.arange(TM_EXP // SC_ROWS, dtype=I32) * SC_ROWS
    spare = n_rows + lane
    pad = (ends - padded + counts)[:, None, None] + chunk0[None, :, None] + lane[None, None, :]
    pad = jnp.where(pad < ends[:, None, None], pad, spare[None, None, :])
    tail = ends[-1] + jnp.arange(N_EXPERTS * TM_EXP // SC_ROWS, dtype=I32)[:, None] * SC_ROWS + lane[None, :]
    tail = jnp.where(tail < n_rows, tail, spare[None, :])
    return jnp.concatenate([pad.reshape(-1), tail.reshape(-1)]).astype(I32)


def _sc_dispatch(x, pos0, pos1, unused, n_rows):
    t, d = x.shape
    workers = SC_CORES * SC_SUBCORES
    per_worker = t // workers
    zero_chunks = unused.shape[0] // SC_ROWS // workers
    mesh = plsc.VectorSubcoreMesh(core_axis_name="core", subcore_axis_name="subcore")

    @pl.kernel(out_type=jax.ShapeDtypeStruct((n_rows + SC_ROWS, d), x.dtype), mesh=mesh,
               scratch_types=[pltpu.VMEM((SC_ROWS,), I32), pltpu.VMEM((SC_ROWS,), I32),
                              pltpu.VMEM((SC_ROWS, d), x.dtype)])
    def run(x_hbm, p0_hbm, p1_hbm, unused_hbm, zeros_hbm, o_hbm, i0_v, i1_v, rows_v):
        pltpu.sync_copy(zeros_hbm, rows_v)
        zbase = _sc_worker_base(zero_chunks * SC_ROWS)

        @pl.loop(0, zero_chunks)
        def _(c):
            pltpu.sync_copy(unused_hbm.at[pl.ds(zbase + c * SC_ROWS, SC_ROWS)], i0_v)
            pltpu.sync_copy(rows_v, o_hbm.at[i0_v])

        base = _sc_worker_base(per_worker)

        @pl.loop(0, per_worker // SC_ROWS)
        def _(c):
            off = base + c * SC_ROWS
            pltpu.sync_copy(p0_hbm.at[pl.ds(off, SC_ROWS)], i0_v)
            pltpu.sync_copy(p1_hbm.at[pl.ds(off, SC_ROWS)], i1_v)
            pltpu.sync_copy(x_hbm.at[pl.ds(off, SC_ROWS)], rows_v)
            pltpu.sync_copy(rows_v, o_hbm.at[i0_v])
            pltpu.sync_copy(rows_v, o_hbm.at[i1_v])

    return run(x, pos0, pos1, unused, jnp.zeros((SC_ROWS, d), x.dtype))


def _sc_gather2(ys, pos0, pos1):
    t = pos0.shape[0]
    d = ys.shape[1]
    per_worker = t // (SC_CORES * SC_SUBCORES)
    mesh = plsc.VectorSubcoreMesh(core_axis_name="core", subcore_axis_name="subcore")
    out = jax.ShapeDtypeStruct((t, d), ys.dtype)

    @pl.kernel(out_type=(out, out), mesh=mesh,
               scratch_types=[pltpu.VMEM((SC_ROWS,), I32), pltpu.VMEM((SC_ROWS, d), ys.dtype)])
    def run(y_hbm, p0_hbm, p1_hbm, o0_hbm, o1_hbm, i_v, rows_v):
        base = _sc_worker_base(per_worker)

        @pl.loop(0, per_worker // SC_ROWS)
        def _(c):
            off = base + c * SC_ROWS
            for p_hbm, o_hbm in ((p0_hbm, o0_hbm), (p1_hbm, o1_hbm)):
                pltpu.sync_copy(p_hbm.at[pl.ds(off, SC_ROWS)], i_v)
                pltpu.sync_copy(y_hbm.at[i_v], rows_v)
                pltpu.sync_copy(rows_v, o_hbm.at[pl.ds(off, SC_ROWS)])

    return run(ys, pos0, pos1)


def _moe_sum_kernel(x_ref, y0_ref, y1_ref, gate_ref, g_ref, b_ref, o_ref):
    gates = gate_ref[...]
    ff = gates[:, 0:1] * _unpack_bf16_pairs(y0_ref[...]) + gates[:, 1:2] * _unpack_bf16_pairs(y1_ref[...])
    o_ref[...] = _layer_norm(ALPHA * x_ref[...] + ff, g_ref[...], b_ref[...])


def _moe_sum(x, y0, y1, gates, g, b):
    t, d = x.shape
    tm = TM_WIDE
    const = lambda i: (0, 0)
    row = lambda i: (i, 0)
    return pl.pallas_call(
        _moe_sum_kernel,
        grid=(t // tm,),
        in_specs=[pl.BlockSpec((tm, d), row), pl.BlockSpec((tm, d // 2), row), pl.BlockSpec((tm, d // 2), row),
                  pl.BlockSpec((tm, TOP_K), row), pl.BlockSpec((1, d), const), pl.BlockSpec((1, d), const)],
        out_specs=pl.BlockSpec((tm, d), row),
        out_shape=jax.ShapeDtypeStruct((t, d), F32),
        compiler_params=_params("arbitrary"),
        name="moe_sum",
    )(x, y0, y1, gates, g, b)


MAX_QBLKS_PER_STEP = 8


def _attn_kernel(q_ref, kp_ref, kc_ref, vp_ref, vc_ref, o_ref, m_ref, l_ref, kcat_ref, vcat_ref):
    n_pairs = q_ref.shape[-1] // LANES
    qi = lax.broadcasted_iota(I32, (QBLK, 2 * QBLK), 0)
    ki = lax.broadcasted_iota(I32, (QBLK, 2 * QBLK), 1)
    in_band = ki - QBLK <= qi
    lane = lax.broadcasted_iota(I32, (QBLK, LANES), 1)
    low_half = lane < HEAD_DIM
    for pr in range(n_pairs):
        vcat_ref[:, :, (2 * pr + 1) * LANES:(2 * pr + 2) * LANES] = jnp.ones(vcat_ref.shape[:2] + (LANES,), BF16)
    m_ref[...] = jnp.zeros(m_ref.shape, F32)
    l_ref[...] = jnp.ones(l_ref.shape, F32)

    def block(r, j):
        q_r, o_r, m_r, l_r = q_ref.at[r], o_ref.at[r], m_ref.at[r], l_ref.at[r]
        rows = slice(j * QBLK, (j + 1) * QBLK)
        keys = slice(j * QBLK, (j + 2) * QBLK)
        first_ok = qi if j > 0 else qi + jnp.where(pl.program_id(2) > 0, 0, QBLK)
        mask = jnp.logical_and(jnp.logical_or(ki >= first_ok, ki >= QBLK), in_band)
        for pr in range(n_pairs):
            cols = slice(pr * LANES, (pr + 1) * LANES)
            q2 = q_r[rows, cols]
            k2 = kcat_ref[r, keys, cols]
            v_ext = vcat_ref[r, keys, 2 * pr * LANES:(2 * pr + 2) * LANES]
            o_pair = None
            for a in range(2):
                own = low_half if a == 0 else jnp.logical_not(low_half)
                qm = jnp.where(own, q2, jnp.zeros_like(q2))
                s = lax.dot_general(qm, k2, _NT, preferred_element_type=F32)
                s = jnp.where(mask, s, NEG_BIG)
                m = jnp.max(s, axis=-1, keepdims=True)
                p = jnp.exp(s - m).astype(BF16)
                oe = jnp.dot(p, v_ext, preferred_element_type=F32)
                o_pair = oe[:, :LANES] if a == 0 else jnp.where(low_half, o_pair, oe[:, :LANES])
                h = 2 * pr + a
                m_r[rows, h:h + 1] = m
                l_r[rows, h:h + 1] = oe[:, LANES + h:LANES + h + 1]
            o_r[rows, cols] = o_pair.astype(o_r.dtype)

    for r in range(q_ref.shape[0]):
        kcat_ref[r, 0:QBLK, :] = kp_ref[r]
        kcat_ref[r, QBLK:, :] = kc_ref[r]
        for pr in range(n_pairs):
            vcat_ref[r, 0:QBLK, 2 * pr * LANES:(2 * pr + 1) * LANES] = vp_ref[r, :, pr * LANES:(pr + 1) * LANES]
            vcat_ref[r, QBLK:, 2 * pr * LANES:(2 * pr + 1) * LANES] = vc_ref[r, :, pr * LANES:(pr + 1) * LANES]
    for r in range(q_ref.shape[0]):
        for j in range(q_ref.shape[1] // QBLK):
            block(r, j)


def _group_attention(q, k, v):
    bsz, dil, length, width = q.shape
    blocks_per_step = min(MAX_QBLKS_PER_STEP, length // QBLK)
    classes_per_step = min(dil, MAX_QBLKS_PER_STEP // blocks_per_step)
    step = blocks_per_step * QBLK
    blk = (None, classes_per_step, step, width)
    cur = lambda b, r, n: (b, r, n, 0)
    prev_blk = (None, classes_per_step, QBLK, width)
    prev = lambda b, r, n: (b, r, jnp.maximum(blocks_per_step * n - 1, 0), 0)
    stat_spec = pl.BlockSpec((None, classes_per_step, step, LANES), cur)
    stat_shape = jax.ShapeDtypeStruct((bsz, dil, length, LANES), F32)
    return pl.pallas_call(
        _attn_kernel,
        grid=(bsz, dil // classes_per_step, length // step),
        in_specs=[
            pl.BlockSpec(blk, cur),
            pl.BlockSpec(prev_blk, prev),
            pl.BlockSpec(blk, cur),
            pl.BlockSpec(prev_blk, prev),
            pl.BlockSpec(blk, cur),
        ],
        out_specs=[pl.BlockSpec(blk, cur), stat_spec, stat_spec],
        out_shape=[jax.ShapeDtypeStruct((bsz, dil, length, width), BF16), stat_shape, stat_shape],
        scratch_shapes=[pltpu.VMEM((classes_per_step, step + QBLK, width), BF16),
                        pltpu.VMEM((classes_per_step, step + QBLK, 2 * width), BF16)],
        compiler_params=_params("arbitrary", "arbitrary", "arbitrary"),
        name="group_attention",
    )(q, k, k, v, v)


def _to_token_order(src_ref, dst_ref):
    dil, n, w = src_ref.shape
    for r in range(dil):
        for c in range(w // LANES):
            dst_ref[c, pl.ds(r, n, stride=dil), :] = src_ref[r, :, c * LANES:(c + 1) * LANES].astype(dst_ref.dtype)
    return jnp.concatenate([dst_ref[c] for c in range(w // LANES)], axis=1)


def _attn_out_kernel(x_ref, o0_ref, o1_ref, o2_ref, m0_ref, m1_ref, m2_ref, l0_ref, l1_ref, l2_ref,
                     wo_ref, g_ref, b_ref, out_ref, o_tok, stat_tok):
    width = o0_ref.shape[-1]

    def token_order(ref, scratch):
        dil, n, w = ref.shape
        if dil == 1:
            return ref[0].astype(F32)
        if dil % SUBLANES == 0:
            return jnp.swapaxes(ref[...].astype(F32), 0, 1).reshape(dil * n, w)
        return _to_token_order(ref, scratch)

    ms = [token_order(r, stat_tok.at[i]) for i, r in enumerate((m0_ref, m1_ref, m2_ref))]
    ls = [token_order(r, stat_tok.at[N_GROUPS + i]) for i, r in enumerate((l0_ref, l1_ref, l2_ref))]
    top = jnp.maximum(jnp.maximum(ms[0], ms[1]), ms[2])
    es = [jnp.exp(m - top) for m in ms]
    den = es[0] * ls[0] + es[1] * ls[1] + es[2] * ls[2]
    head = lax.broadcasted_iota(I32, (LANES, width), 0)
    lane = lax.broadcasted_iota(I32, (LANES, width), 1)
    spread = (lane // HEAD_DIM == head).astype(BF16)
    mixed = jnp.zeros(out_ref.shape[:1] + (width,), F32)
    for e, o_ref in zip(es, (o0_ref, o1_ref, o2_ref)):
        hi, lo = _split_bf16(e / den)
        wide = (jnp.dot(hi, spread, preferred_element_type=F32)
                + jnp.dot(lo, spread, preferred_element_type=F32))
        mixed = mixed + wide * token_order(o_ref, o_tok)
    y = jnp.dot(mixed.astype(BF16), wo_ref[...], preferred_element_type=F32)
    out_ref[...] = _layer_norm(ALPHA * x_ref[...] + y, g_ref[...], b_ref[...])


def _attn_out(x, bsz, outs, stats, w_o, g, b):
    t, d = x.shape
    width = w_o.shape[0]
    tm = TM_WIDE
    tiles_per_seq = t // bsz // tm
    const = lambda bi, c: (0, 0)
    row = lambda bi, c: (bi * tiles_per_seq + c, 0)

    def res_spec(a):
        dil, w = a.shape[1], a.shape[3]
        return pl.BlockSpec((None, dil, tm // dil, w), lambda bi, c: (bi, 0, c, 0))

    return pl.pallas_call(
        _attn_out_kernel,
        grid=(bsz, tiles_per_seq),
        in_specs=([pl.BlockSpec((tm, d), row)] + [res_spec(a) for a in outs] + [res_spec(a) for a in stats]
                  + [pl.BlockSpec((width, d), const), pl.BlockSpec((1, d), const), pl.BlockSpec((1, d), const)]),
        out_specs=pl.BlockSpec((tm, d), row),
        out_shape=jax.ShapeDtypeStruct((t, d), F32),
        scratch_shapes=[pltpu.VMEM((width // LANES, tm, LANES), F32), pltpu.VMEM((2 * N_GROUPS, 1, tm, LANES), F32)],
        compiler_params=_params("arbitrary", "arbitrary"),
        name="attn_out",
    )(x, *outs, *stats, w_o, g, b)


def _proj_residue_kernel(x_ref, w_ref, *rest, dils, scales):
    outs, y_ref = rest[:-1], rest[-1]
    _, n_slabs, tm, _ = y_ref.shape
    width = n_slabs * LANES
    xb = x_ref[...].astype(BF16)
    strided = [c for c, dil in enumerate(dils) if dil > 1 and dil % SUBLANES]
    for c, (o_ref, dil) in enumerate(zip(outs, dils)):
        y = jnp.dot(xb, w_ref[:, c * width:(c + 1) * width], preferred_element_type=F32)
        if scales[c] != 1.0:
            y = y * scales[c]
        if dil == 1:
            o_ref[0] = y.astype(o_ref.dtype)
        elif dil % SUBLANES == 0:
            o_ref[...] = jnp.swapaxes(y.reshape(tm // dil, dil, width), 0, 1).astype(o_ref.dtype)
        else:
            for s in range(n_slabs):
                y_ref[strided.index(c), s] = y[:, s * LANES:(s + 1) * LANES]
            for r in range(dil):
                rows = [y_ref[strided.index(c), s, pl.ds(r, tm // dil, stride=dil), :] for s in range(n_slabs)]
                o_ref[r] = jnp.concatenate(rows, axis=1).astype(o_ref.dtype)


def _proj_residue(x, bsz, w, dils, scales):
    t, d = x.shape
    seq = t // bsz
    width = w.shape[1] // len(dils)
    tm = TM_WIDE
    tiles_per_seq = seq // tm
    return pl.pallas_call(
        functools.partial(_proj_residue_kernel, dils=dils, scales=scales),
        grid=(bsz, tiles_per_seq),
        in_specs=[pl.BlockSpec((tm, d), lambda bi, c: (bi * tiles_per_seq + c, 0)),
                  pl.BlockSpec(w.shape, lambda bi, c: (0, 0), pipeline_mode=pl.Buffered(1))],
        out_specs=[pl.BlockSpec((None, dil, tm // dil, width), lambda bi, c: (bi, 0, c, 0)) for dil in dils],
        out_shape=[jax.ShapeDtypeStruct((bsz, dil, seq // dil, width), BF16) for dil in dils],
        scratch_shapes=[pltpu.VMEM((max(1, sum(1 for dil in dils if dil > 1 and dil % SUBLANES)), width // LANES, tm, LANES),
                                   F32)],
        compiler_params=_params("arbitrary", "arbitrary"),
        name="proj_residue",
    )(x, w)


Q_DILS = tuple(dil for _, dil in ATT_GROUPS)
KV_DILS = tuple(dil for _, dil in ATT_GROUPS for _ in range(2))
Q_SCALES = (HEAD_DIM ** -0.5,) * N_GROUPS


def _attention_layer(x, bsz, qs, w_o, kv_groups, g, b):
    outs, maxes, dens = [], [], []
    for q, (k_res, v_res) in zip(qs, kv_groups):
        o, m, l = _group_attention(q, k_res, v_res)
        outs.append(o)
        maxes.append(m)
        dens.append(l)
    return _attn_out(x, bsz, outs, maxes + dens, w_o, g, b)


def kernel(x, a_w_in, a_conv, a_w_out, kv_w, b_w_q, b_w_o, ffn_w_gate, ffn_w_up, ffn_w_down,
           moe_w_router, moe_w_gate, moe_w_up, moe_w_down, ln_g, ln_b):
    bsz, seq, d = x.shape
    h = x.reshape(bsz * seq, d)
    ln_g = ln_g.reshape(DEPTH, 2, 1, d)
    ln_b = ln_b.reshape(DEPTH, 2, 1, d)
    kv_groups = None
    for l in range(DEPTH):
        if l < N_A_LAYERS:
            h = _a_mixer(h, a_w_in[l].astype(BF16), a_conv[l], a_w_out[l].astype(BF16), ln_g[l, 0], ln_b[l, 0], seq)
        else:
            j = l - N_A_LAYERS
            if kv_groups is None:
                w_kvq = jnp.concatenate([kv_w, b_w_q[j]], axis=1).astype(BF16)
                kvq = _proj_residue(h, bsz, w_kvq, KV_DILS + Q_DILS, (1.0,) * len(KV_DILS) + Q_SCALES)
                kv_groups = [(kvq[2 * gi], kvq[2 * gi + 1]) for gi in range(N_GROUPS)]
                qs = kvq[len(KV_DILS):]
            else:
                qs = _proj_residue(h, bsz, b_w_q[j].astype(BF16), Q_DILS, Q_SCALES)
            h = _attention_layer(h, bsz, qs, b_w_o[j].astype(BF16), kv_groups, ln_g[l, 0], ln_b[l, 0])
        i = l // 2
        if l % 2 == 0:
            w_gate, w_up, w_down = (w[i].astype(BF16) for w in (ffn_w_gate, ffn_w_up, ffn_w_down))
            h = _dense_ffn(h, w_gate, w_up, w_down, ln_g[l, 1], ln_b[l, 1])
        else:
            h = _moe_layer(h, moe_w_router[i], moe_w_gate.astype(BF16), moe_w_up.astype(BF16),
                           moe_w_down.astype(BF16), i, ln_g[l, 1], ln_b[l, 1])
    return h.reshape(bsz, seq, d)
```
